```python
import math
import jax, jax.numpy as jnp
from jax import lax
import numpy as np

D_MODEL = 1024
BATCH = 8
SEQ = 2048
DEPTH = 1
DEC_BATCH = 128
DEC_SEQ = 1
PAST_LEN = 16384
PAGE_SIZE = 128

N_GLA_HEADS = 4
GLA_DK_HEAD = D_MODEL // (2 * N_GLA_HEADS)
GLA_DV_HEAD = D_MODEL // N_GLA_HEADS
GLA_QK_WIDTH = N_GLA_HEADS * GLA_DK_HEAD
GLA_V_WIDTH = N_GLA_HEADS * GLA_DV_HEAD
GATE_LOW_RANK = 16
GLA_GATE_TAU = 16.0
GLA_CHUNK = 64
D_CONV = D_MODEL
CONV_WIDTH = 3
D_FF = 2816
MACARON_WEIGHT = 0.5
RMS_EPS = 1e-6
N_SUBLAYERS = 3

_O_K = GLA_QK_WIDTH
_O_V = _O_K + GLA_QK_WIDTH
_O_R = _O_V + GLA_V_WIDTH
_O_Z = _O_R + GLA_V_WIDTH
_O_B = _O_Z + GATE_LOW_RANK
_O_C = _O_B + D_CONV
_O_H = _O_C + D_CONV
_O_GA = _O_H + D_CONV
_O_GB = _O_GA + D_MODEL
MIX_IN_WIDTH = _O_GB + D_MODEL
SPLIT_POINTS = (_O_K, _O_V, _O_R, _O_Z, _O_B, _O_C, _O_H, _O_GA, _O_GB)

kernel_name = 'gla_shortconv_macaron_adaln_decoder_step'


def _rmsnorm(x, g):
    x32 = x.astype(jnp.float32)
    y = x32 * lax.rsqrt(jnp.mean(x32 * x32, axis=-1, keepdims=True) + RMS_EPS)
    return (y * g.astype(jnp.float32)).astype(x.dtype)


def _swiglu(h, w_in, w_out):
    gate, up = jnp.split(h @ w_in, 2, axis=-1)
    return (jax.nn.silu(gate) * up) @ w_out


def _gla(q, k, v, log_a, s0):
    n, l = q.shape[0], q.shape[1]
    c = math.gcd(l, GLA_CHUNK)
    nc = l // c

    def to_chunks(t):
        return t.reshape(n, nc, c, N_GLA_HEADS, t.shape[-1]).transpose(1, 0, 3, 2, 4).astype(jnp.float32)

    causal = jnp.tril(jnp.ones((c, c), dtype=bool))[:, :, None]

    def step(s, inp):
        qc, kc, vc, ac = inp
        b = jnp.cumsum(ac, axis=2)
        diff = jnp.where(causal, b[:, :, :, None, :] - b[:, :, None, :, :], -jnp.inf)
        scores = jnp.einsum('nhtk,nhsk,nhtsk->nhts', qc, kc, jnp.exp(diff))
        o = jnp.einsum('nhts,nhsv->nhtv', scores, vc) + jnp.einsum('nhtk,nhkv->nhtv', qc * jnp.exp(b), s)
        b_last = b[:, :, -1:, :]
        s_new = jnp.exp(b_last[:, :, 0, :])[..., None] * s + jnp.einsum('nhsk,nhsv->nhkv', kc * jnp.exp(b_last - b), vc)
        return s_new, o

    s_fin, o = lax.scan(step, s0.astype(jnp.float32), (to_chunks(q), to_chunks(k), to_chunks(v), to_chunks(log_a)))
    o = o.transpose(1, 0, 3, 2, 4).reshape(n, l, N_GLA_HEADS, GLA_DV_HEAD)
    return o, s_fin.astype(s0.dtype)


def _token_mixer(h, s_gla, s_conv, w_mix_in, w_alpha, b_alpha, g_gla_norm, w_conv, w_branch_out, w_mix_out):
    n, l, _ = h.shape
    proj = h @ w_mix_in
    q, k, v, r, z, cb, cc, ch, ga, gb = jnp.split(proj, SPLIT_POINTS, axis=-1)

    def heads(t):
        return t.reshape(n, l, N_GLA_HEADS, -1)

    log_a = jax.nn.log_sigmoid((z @ w_alpha + b_alpha).astype(jnp.float32)) / GLA_GATE_TAU
    o, s_gla_new = _gla(heads(q) * (GLA_DK_HEAD ** -0.5), heads(k), heads(v), heads(log_a), s_gla)
    o = _rmsnorm(o.astype(h.dtype), g_gla_norm.reshape(N_GLA_HEADS, GLA_DV_HEAD)).reshape(n, l, GLA_V_WIDTH)
    y_gla = o * jax.nn.silu(r)

    u = cc * ch
    padded = jnp.concatenate([s_conv.astype(u.dtype), u], axis=1)
    conv = (w_conv[0] * padded[:, 0:l] + w_conv[1] * padded[:, 1:l + 1] + w_conv[2] * padded[:, 2:l + 2])
    s_conv_new = padded[:, l:]
    y_conv = cb * conv

    branches = jnp.stack([y_gla, y_conv], axis=-2)
    proj_b = jnp.einsum('nlbw,bwd->nlbd', branches, w_branch_out)
    gates = jax.nn.sigmoid(jnp.stack([ga, gb], axis=-2))
    merged = jnp.sum(gates * proj_b, axis=-2)
    return merged @ w_mix_out, s_gla_new, s_conv_new


def _decoder_layer(x, c, s_gla, s_conv, w_ada, b_ada, g_pre, g_post, w_ffn1_in, w_ffn1_out,
                   w_ffn2_in, w_ffn2_out, w_mix_in, w_alpha, b_alpha, g_gla_norm, w_conv,
                   w_branch_out, w_mix_out):
    ada = (jax.nn.silu(c) @ w_ada + b_ada).reshape(c.shape[0], N_SUBLAYERS, 3, 1, D_MODEL)
    shift, scale, gate = ada[:, :, 0], ada[:, :, 1], ada[:, :, 2]

    def pre(i, t):
        return _rmsnorm(t, g_pre[i]) * (1.0 + scale[:, i]) + shift[:, i]

    def post(i, t, out, weight):
        return t + weight * gate[:, i] * _rmsnorm(out, g_post[i])

    x = post(0, x, _swiglu(pre(0, x), w_ffn1_in, w_ffn1_out), MACARON_WEIGHT)
    mix, s_gla_new, s_conv_new = _token_mixer(pre(1, x), s_gla, s_conv, w_mix_in, w_alpha, b_alpha,
                                              g_gla_norm, w_conv, w_branch_out, w_mix_out)
    x = post(1, x, mix, 1.0)
    x = post(2, x, _swiglu(pre(2, x), w_ffn2_in, w_ffn2_out), MACARON_WEIGHT)
    return x, s_gla_new, s_conv_new


def setup_inputs(seed: int = 0) -> dict:
    key = jax.random.key(seed)
    ks = jax.random.split(key, 24)

    def nrm(k, shape, scale):
        return jax.random.normal(k, shape, jnp.float32) * scale

    return {
        'x_prompt': nrm(ks[0], (BATCH, SEQ, D_MODEL), 1.0),
        'x_sample': nrm(ks[1], (DEC_BATCH, DEC_SEQ, D_MODEL), 1.0),
        'state_gla': nrm(ks[2], (DEPTH, DEC_BATCH, N_GLA_HEADS, GLA_DK_HEAD, GLA_DV_HEAD), 1.0),
        'state_conv': nrm(ks[3], (DEPTH, DEC_BATCH, CONV_WIDTH - 1, D_CONV), 1.0),
        'c_prompt': nrm(ks[4], (BATCH, D_MODEL), 1.0),
        'c_sample': nrm(ks[5], (DEC_BATCH, D_MODEL), 1.0),
        'w_ada': nrm(ks[6], (DEPTH, D_MODEL, N_SUBLAYERS * 3 * D_MODEL), D_MODEL ** -0.5),
        'b_ada': nrm(ks[7], (DEPTH, N_SUBLAYERS * 3 * D_MODEL), 0.02),
        'g_pre': 1.0 + nrm(ks[8], (DEPTH, N_SUBLAYERS, D_MODEL), 0.02),
        'g_post': 1.0 + nrm(ks[9], (DEPTH, N_SUBLAYERS, D_MODEL), 0.02),
        'w_ffn1_in': nrm(ks[10], (DEPTH, D_MODEL, 2 * D_FF), D_MODEL ** -0.5),
        'w_ffn1_out': nrm(ks[11], (DEPTH, D_FF, D_MODEL), D_FF ** -0.5),
        'w_ffn2_in': nrm(ks[12], (DEPTH, D_MODEL, 2 * D_FF), D_MODEL ** -0.5),
        'w_ffn2_out': nrm(ks[13], (DEPTH, D_FF, D_MODEL), D_FF ** -0.5),
        'w_mix_in': nrm(ks[14], (DEPTH, D_MODEL, MIX_IN_WIDTH), D_MODEL ** -0.5),
        'w_alpha': nrm(ks[15], (DEPTH, GATE_LOW_RANK, GLA_QK_WIDTH), GATE_LOW_RANK ** -0.5),
        'b_alpha': nrm(ks[16], (DEPTH, GLA_QK_WIDTH), 0.02),
        'g_gla_norm': 1.0 + nrm(ks[17], (DEPTH, GLA_V_WIDTH), 0.02),
        'w_conv': nrm(ks[18], (DEPTH, CONV_WIDTH, D_CONV), CONV_WIDTH ** -0.5),
        'w_branch_out': nrm(ks[19], (DEPTH, 2, D_MODEL, D_MODEL), D_MODEL ** -0.5),
        'w_mix_out': nrm(ks[20], (DEPTH, D_MODEL, D_MODEL), D_MODEL ** -0.5),
    }


def reference(x_prompt, x_sample, state_gla, state_conv, c_prompt, c_sample, w_ada, b_ada, g_pre,
              g_post, w_ffn1_in, w_ffn1_out, w_ffn2_in, w_ffn2_out, w_mix_in, w_alpha, b_alpha,
              g_gla_norm, w_conv, w_branch_out, w_mix_out):
    yp, ys = x_prompt, x_sample
    n_p = x_prompt.shape[0]
    gla_p, conv_p, gla_s, conv_s = [], [], [], []
    for i in range(DEPTH):
        wl = (w_ada[i], b_ada[i], g_pre[i], g_post[i], w_ffn1_in[i], w_ffn1_out[i], w_ffn2_in[i],
              w_ffn2_out[i], w_mix_in[i], w_alpha[i], b_alpha[i], g_gla_norm[i], w_conv[i],
              w_branch_out[i], w_mix_out[i])
        s_gla0 = jnp.zeros((n_p, N_GLA_HEADS, GLA_DK_HEAD, GLA_DV_HEAD), x_prompt.dtype)
        s_conv0 = jnp.zeros((n_p, CONV_WIDTH - 1, D_CONV), x_prompt.dtype)
        yp, sg, sc = _decoder_layer(yp, c_prompt, s_gla0, s_conv0, *wl)
        gla_p.append(sg)
        conv_p.append(sc)
        ys, sg, sc = _decoder_layer(ys, c_sample, state_gla[i], state_conv[i], *wl)
        gla_s.append(sg)
        conv_s.append(sc)
    return (yp, ys, jnp.stack(gla_p), jnp.stack(conv_p), jnp.stack(gla_s), jnp.stack(conv_s))
```

```python
import functools

import jax
import jax.numpy as jnp
from jax import lax
from jax.experimental import pallas as pl
from jax.experimental.pallas import tpu as pltpu

F32 = jnp.float32
BF16 = jnp.bfloat16

D_MODEL = 1024
D_FF = 2816
N_SUBLAYERS = 3
N_HEADS = 4
DK = 128
DV = 256
QK_WIDTH = N_HEADS * DK
V_WIDTH = N_HEADS * DV
GATE_RANK = 16
GATE_TAU = 16.0
D_CONV = D_MODEL
CONV_WIDTH = 3
RMS_EPS = 1e-6
MACARON_WEIGHT = 0.5

LANES = 128
SUBLANES = 8
GLA_CHUNK = 64
VMEM_LIMIT_BYTES = 56 * 1024 * 1024

FFN_TM = 512
MIX_TM = 256
ADA_TN = 1536
SAMPLE_BLOCK = 8

_A_WIDTH = 2 * QK_WIDTH + 2 * V_WIDTH
_B_WIDTH = 3 * D_CONV + 2 * D_MODEL


def _rms(x, g):
    return x * lax.rsqrt(jnp.mean(x * x, axis=-1, keepdims=True) + RMS_EPS) * g


def _silu(x):
    return x * jax.nn.sigmoid(x)


def _dot(a, b):
    return jnp.dot(a, b, preferred_element_type=F32)


def _dot_nt(a, b):
    return lax.dot_general(a, b, (((1,), (1,)), ((), ())), preferred_element_type=F32)


def _dot_tn(a, b):
    return lax.dot_general(a, b, (((0,), (0,)), ((), ())), preferred_element_type=F32)


def _resident(shape):
    zeros = (0,) * len(shape)
    return pl.BlockSpec(shape, lambda *_: zeros, pipeline_mode=pl.Buffered(1))


def _params(semantics):
    return pltpu.CompilerParams(dimension_semantics=semantics, vmem_limit_bytes=VMEM_LIMIT_BYTES)


def _ada_kernel(c_ref, w_ref, b_ref, o_ref):
    c = c_ref[...]
    o_ref[...] = _dot(_silu(c).astype(BF16), w_ref[...].astype(BF16)) + b_ref[...]


def _ada(c_all, w_ada, b_ada):
    rows = c_all.shape[0]
    width = w_ada.shape[1]
    return pl.pallas_call(
        _ada_kernel,
        grid=(width // ADA_TN,),
        in_specs=[
            pl.BlockSpec((rows, D_MODEL), lambda j: (0, 0)),
            pl.BlockSpec((D_MODEL, ADA_TN), lambda j: (0, j)),
            pl.BlockSpec((1, ADA_TN), lambda j: (0, j)),
        ],
        out_specs=pl.BlockSpec((rows, ADA_TN), lambda j: (0, j)),
        out_shape=jax.ShapeDtypeStruct((rows, width), F32),
        compiler_params=_params(("arbitrary",)),
        name="ada",
    )(c_all, w_ada, b_ada.reshape(1, width))


def _mod_specs(sub, per_token, prompt_row_block):
    specs = []
    for j in range(3):
        col = sub * 3 + j
        if per_token:
            specs.append(pl.BlockSpec((LANES, D_MODEL), lambda n, l, col=col: (0, col)))
        else:
            specs.append(pl.BlockSpec((SUBLANES, D_MODEL),
                                      lambda n, l, col=col: (prompt_row_block, col)))
    return specs


def _load_mod(ref, per_token):
    if per_token:
        return ref[...]
    return ref[pl.ds(pl.program_id(0), 1), :]


def _ffn_kernel(x_ref, sh_ref, sc_ref, gt_ref, gpre_ref, gpost_ref, win_ref, wout_ref, o_ref,
                *, per_token, weight):
    x = x_ref[0]
    shift = _load_mod(sh_ref, per_token)
    scale = _load_mod(sc_ref, per_token)
    gate = _load_mod(gt_ref, per_token)
    h = (_rms(x, gpre_ref[...]) * (1.0 + scale) + shift).astype(BF16)
    g = _dot(h, win_ref[:, :D_FF])
    u = _dot(h, win_ref[:, D_FF:])
    a = (_silu(g) * u).astype(BF16)
    out = _dot(a, wout_ref[...])
    o_ref[0] = x + weight * gate * _rms(out, gpost_ref[...])


def _ffn(x, ada, sub, g_pre, g_post, w_in, w_out, *, per_token, tm, prompt_row_block):
    n, l, _ = x.shape
    kern = functools.partial(_ffn_kernel, per_token=per_token, weight=MACARON_WEIGHT)
    return pl.pallas_call(
        kern,
        grid=(n, l // tm),
        in_specs=[pl.BlockSpec((1, tm, D_MODEL), lambda n, l: (n, l, 0))]
        + _mod_specs(sub, per_token, prompt_row_block)
        + [_resident((1, D_MODEL)), _resident((1, D_MODEL)),
           _resident((D_MODEL, 2 * D_FF)), _resident((D_FF, D_MODEL))],
        out_specs=pl.BlockSpec((1, tm, D_MODEL), lambda n, l: (n, l, 0)),
        out_shape=jax.ShapeDtypeStruct(x.shape, F32),
        compiler_params=_params(("arbitrary", "arbitrary")),
        name="ffn",
    )(x, ada, ada, ada, g_pre, g_post, w_in, w_out)


def _log_decay(hb, wz_ref, walpha_ref, balpha_ref):
    z = _dot(hb, wz_ref[...]).astype(BF16)
    xg = _dot(z, walpha_ref[...]) + balpha_ref[...]
    return jax.nn.log_sigmoid(xg) * (1.0 / GATE_TAU)


def _merge_and_project(y_gla, y_conv, sig_a, sig_b, wbo_ref, wmo_ref):
    pg = _dot(y_gla, wbo_ref[0])
    pc = _dot(y_conv, wbo_ref[1])
    merged = (sig_a * pg + sig_b * pc).astype(BF16)
    return _dot(merged, wmo_ref[...])


def _mixer_kernel(x_ref, sh_ref, sc_ref, gt_ref, gpre_ref, gpost_ref, wa_ref, wz_ref, wb_ref,
                  walpha_ref, balpha_ref, ggla_ref, wconv_ref, wbo_ref, wmo_ref,
                  y_ref, sgla_ref, sconv_ref, st_scr, u_scr, ygla_scr, *, tm):
    l = pl.program_id(1)
    n_l = pl.num_programs(1)

    @pl.when(l == 0)
    def _():
        st_scr[...] = jnp.zeros_like(st_scr)
        u_scr[0:SUBLANES, :] = jnp.zeros((SUBLANES, D_CONV), F32)

    x = x_ref[0]
    shift = _load_mod(sh_ref, False)
    scale = _load_mod(sc_ref, False)
    gate = _load_mod(gt_ref, False)
    hb = (_rms(x, gpre_ref[...]) * (1.0 + scale) + shift).astype(BF16)

    pa = _dot(hb, wa_ref[...])
    q = pa[:, 0:QK_WIDTH]
    k = pa[:, QK_WIDTH:2 * QK_WIDTH]
    v = pa[:, 2 * QK_WIDTH:2 * QK_WIDTH + V_WIDTH]
    r = pa[:, 2 * QK_WIDTH + V_WIDTH:]
    log_a = _log_decay(hb, wz_ref, walpha_ref, balpha_ref)

    row = lax.broadcasted_iota(jnp.int32, (GLA_CHUNK, GLA_CHUNK), 0)
    col = lax.broadcasted_iota(jnp.int32, (GLA_CHUNK, GLA_CHUNK), 1)
    causal = row >= col
    tri = jnp.where(causal, 1.0, 0.0).astype(BF16)
    la_hi = log_a.astype(BF16)
    rem = log_a - la_hi.astype(F32)
    la_mid = rem.astype(BF16)
    la_lo = (rem - la_mid.astype(F32)).astype(BF16)

    q_scale = DK ** -0.5
    for c in range(tm // GLA_CHUNK):
        rows = slice(c * GLA_CHUNK, (c + 1) * GLA_CHUNK)
        b_all = _dot(tri, la_hi[rows]) + _dot(tri, la_mid[rows]) + _dot(tri, la_lo[rows])
        for hd in range(N_HEADS):
            kc = slice(hd * DK, (hd + 1) * DK)
            vc = slice(hd * DV, (hd + 1) * DV)
            b = b_all[:, kc]
            b_last = b[GLA_CHUNK - 1:GLA_CHUNK, :]
            qd = (q[rows, kc] * (jnp.exp(b) * q_scale)).astype(BF16)
            kd = (k[rows, kc] * jnp.exp(-b)).astype(BF16)
            kl = (k[rows, kc] * jnp.exp(b_last - b)).astype(BF16)
            vb = v[rows, vc].astype(BF16)
            s = jnp.where(causal, _dot_nt(qd, kd), 0.0).astype(BF16)
            st = st_scr[hd]
            o = _dot(s, vb) + _dot_nt(qd, st.astype(BF16))
            st_scr[hd] = st * jnp.exp(b_last) + _dot_tn(vb, kl)
            on = _rms(o, ggla_ref[:, vc])
            ygla_scr[rows, vc] = (on * _silu(r[rows, vc])).astype(BF16)

    pb = _dot(hb, wb_ref[...])
    cb = pb[:, 0:D_CONV]
    u = pb[:, D_CONV:2 * D_CONV] * pb[:, 2 * D_CONV:3 * D_CONV]
    sig_a = jax.nn.sigmoid(pb[:, 3 * D_CONV:3 * D_CONV + D_MODEL])
    sig_b = jax.nn.sigmoid(pb[:, 3 * D_CONV + D_MODEL:])
    u_scr[SUBLANES:SUBLANES + tm, :] = u
    conv = (wconv_ref[0:1, :] * u_scr[SUBLANES - 2:SUBLANES - 2 + tm, :]
            + wconv_ref[1:2, :] * u_scr[SUBLANES - 1:SUBLANES - 1 + tm, :]
            + wconv_ref[2:3, :] * u)
    u_scr[0:SUBLANES, :] = u_scr[tm:tm + SUBLANES, :]
    y_conv = (cb * conv).astype(BF16)

    mix = _merge_and_project(ygla_scr[...], y_conv, sig_a, sig_b, wbo_ref, wmo_ref)
    y_ref[0] = x + gate * _rms(mix, gpost_ref[...])

    @pl.when(l == n_l - 1)
    def _():
        for hd in range(N_HEADS):
            sgla_ref[0, 0, hd] = st_scr[hd].T
        sconv_ref[0, 0] = u_scr[SUBLANES - 2:SUBLANES, :]


def _mixer(x, ada, g_pre, g_post, wts, *, prompt_row_block):
    n, l, _ = x.shape
    tm = MIX_TM
    kern = functools.partial(_mixer_kernel, tm=tm)
    wa, wz, wb, walpha, balpha, ggla, wconv, wbo, wmo = wts
    return pl.pallas_call(
        kern,
        grid=(n, l // tm),
        in_specs=[pl.BlockSpec((1, tm, D_MODEL), lambda n, l: (n, l, 0))]
        + _mod_specs(1, False, prompt_row_block)
        + [_resident(a.shape) for a in (g_pre, g_post, wa, wz, wb, walpha, balpha, ggla, wconv,
                                        wbo, wmo)],
        out_specs=[
            pl.BlockSpec((1, tm, D_MODEL), lambda n, l: (n, l, 0)),
            pl.BlockSpec((1, 1, N_HEADS, DK, DV), lambda n, l: (0, n, 0, 0, 0)),
            pl.BlockSpec((1, 1, CONV_WIDTH - 1, D_CONV), lambda n, l: (0, n, 0, 0)),
        ],
        out_shape=[
            jax.ShapeDtypeStruct(x.shape, F32),
            jax.ShapeDtypeStruct((1, n, N_HEADS, DK, DV), F32),
            jax.ShapeDtypeStruct((1, n, CONV_WIDTH - 1, D_CONV), F32),
        ],
        scratch_shapes=[
            pltpu.VMEM((N_HEADS, DV, DK), F32),
            pltpu.VMEM((tm + SUBLANES, D_CONV), F32),
            pltpu.VMEM((tm, V_WIDTH), BF16),
        ],
        compiler_params=_params(("arbitrary", "arbitrary")),
        name="mixer",
    )(x, ada, ada, ada, g_pre, g_post, wa, wz, wb, walpha, balpha, ggla, wconv, wbo, wmo)


def _smix_pre_kernel(x_ref, sh_ref, sc_ref, gpre_ref, wa_ref, wz_ref, wb_ref, walpha_ref,
                     balpha_ref, wconv_ref, sconv_ref,
                     qt_ref, kt_ref, at_ref, v_ref, sr_ref, yconv_ref, sga_ref, sgb_ref,
                     sconv_new_ref):
    x = x_ref[...]
    hb = (_rms(x, gpre_ref[...]) * (1.0 + sc_ref[...]) + sh_ref[...]).astype(BF16)
    pa = _dot(hb, wa_ref[...])
    q = pa[:, 0:QK_WIDTH] * (DK ** -0.5)
    k = pa[:, QK_WIDTH:2 * QK_WIDTH]
    v_ref[...] = pa[:, 2 * QK_WIDTH:2 * QK_WIDTH + V_WIDTH]
    sr_ref[...] = _silu(pa[:, 2 * QK_WIDTH + V_WIDTH:])
    a = jnp.exp(_log_decay(hb, wz_ref, walpha_ref, balpha_ref))
    for hd in range(N_HEADS):
        kc = slice(hd * DK, (hd + 1) * DK)
        qt_ref[hd] = q[:, kc].T
        kt_ref[hd] = k[:, kc].T
        at_ref[hd] = a[:, kc].T

    pb = _dot(hb, wb_ref[...])
    u = pb[:, D_CONV:2 * D_CONV] * pb[:, 2 * D_CONV:3 * D_CONV]
    conv = (wconv_ref[0:1, :] * sconv_ref[0] + wconv_ref[1:2, :] * sconv_ref[1]
            + wconv_ref[2:3, :] * u)
    yconv_ref[...] = pb[:, 0:D_CONV] * conv
    sga_ref[...] = jax.nn.sigmoid(pb[:, 3 * D_CONV:3 * D_CONV + D_MODEL])
    sgb_ref[...] = jax.nn.sigmoid(pb[:, 3 * D_CONV + D_MODEL:])
    sconv_new_ref[0] = sconv_ref[1]
    sconv_new_ref[1] = u


def _smix_state_kernel(s_ref, qt_ref, kt_ref, at_ref, v_ref, snew_ref, o_ref):
    base = pl.program_id(0) * SAMPLE_BLOCK
    lane = lax.broadcasted_iota(jnp.int32, (1, LANES), 1)
    for i in range(SAMPLE_BLOCK):
        pick = jnp.where(lane == base + i, 1.0, 0.0)
        for hd in range(N_HEADS):
            vc = slice(hd * DV, (hd + 1) * DV)
            a_col = jnp.sum(at_ref[hd] * pick, axis=1, keepdims=True)
            k_col = jnp.sum(kt_ref[hd] * pick, axis=1, keepdims=True)
            q_col = jnp.sum(qt_ref[hd] * pick, axis=1, keepdims=True)
            s_new = a_col * s_ref[0, i, hd] + k_col * v_ref[i:i + 1, vc]
            snew_ref[0, i, hd] = s_new
            o_ref[i:i + 1, vc] = jnp.sum(q_col * s_new, axis=0, keepdims=True)


def _smix_post_kernel(x_ref, gt_ref, gpost_ref, o_ref, sr_ref, yconv_ref, sga_ref, sgb_ref,
                      ggla_ref, wbo_ref, wmo_ref, y_ref, ygla_scr):
    for hd in range(N_HEADS):
        vc = slice(hd * DV, (hd + 1) * DV)
        on = _rms(o_ref[:, vc], ggla_ref[:, vc])
        ygla_scr[:, vc] = (on * sr_ref[:, vc]).astype(BF16)
    mix = _merge_and_project(ygla_scr[...], yconv_ref[...].astype(BF16), sga_ref[...],
                             sgb_ref[...], wbo_ref, wmo_ref)
    y_ref[...] = x_ref[...] + gt_ref[...] * _rms(mix, gpost_ref[...])


def _sample_mixer(x, ada, state_gla, sconv_t, g_pre, g_post, wts):
    s = x.shape[0]
    wa, wz, wb, walpha, balpha, ggla, wconv, wbo, wmo = wts
    whole = lambda a: pl.BlockSpec(a.shape, lambda *_: (0,) * a.ndim)
    mod = lambda j: pl.BlockSpec((s, D_MODEL), lambda *_: (0, 3 + j))
    tok = jax.ShapeDtypeStruct((s, D_MODEL), F32)
    tr = jax.ShapeDtypeStruct((N_HEADS, DK, s), F32)

    pre_in = (x, ada, ada, g_pre, wa, wz, wb, walpha, balpha, wconv, sconv_t)
    pre_specs = [whole(x), mod(0), mod(1)] + [whole(a) for a in pre_in[3:]]
    pre_out = [tr, tr, tr, tok, tok, tok, tok, tok, jax.ShapeDtypeStruct(sconv_t.shape, F32)]
    qt, kt, at, v, sr, yconv, sga, sgb, sconv_new = pl.pallas_call(
        _smix_pre_kernel,
        grid=(1,),
        in_specs=pre_specs,
        out_specs=[whole(o) for o in pre_out],
        out_shape=pre_out,
        compiler_params=_params(("arbitrary",)),
        name="smix_pre",
    )(*pre_in)

    state_block = pl.BlockSpec((1, SAMPLE_BLOCK, N_HEADS, DK, DV), lambda j: (0, j, 0, 0, 0))
    snew, o = pl.pallas_call(
        _smix_state_kernel,
        grid=(s // SAMPLE_BLOCK,),
        in_specs=[state_block, whole(qt), whole(kt), whole(at),
                  pl.BlockSpec((SAMPLE_BLOCK, V_WIDTH), lambda j: (j, 0))],
        out_specs=[state_block, pl.BlockSpec((SAMPLE_BLOCK, V_WIDTH), lambda j: (j, 0))],
        out_shape=[jax.ShapeDtypeStruct(state_gla.shape, F32), tok],
        compiler_params=_params(("arbitrary",)),
        name="smix_state",
    )(state_gla, qt, kt, at, v)

    post_in = (x, ada, g_post, o, sr, yconv, sga, sgb, ggla, wbo, wmo)
    post_specs = [whole(x), mod(2)] + [whole(a) for a in post_in[2:]]
    y = pl.pallas_call(
        _smix_post_kernel,
        grid=(1,),
        in_specs=post_specs,
        out_specs=whole(tok),
        out_shape=tok,
        scratch_shapes=[pltpu.VMEM((s, V_WIDTH), BF16)],
        compiler_params=_params(("arbitrary",)),
        name="smix_post",
    )(*post_in)
    return y, snew, sconv_new


def _split_mix_in(w_mix_in):
    wa = w_mix_in[:, :_A_WIDTH].astype(BF16)
    wz = jnp.pad(w_mix_in[:, _A_WIDTH:_A_WIDTH + GATE_RANK], ((0, 0), (0, LANES - GATE_RANK)))
    wb = w_mix_in[:, _A_WIDTH + GATE_RANK:].astype(BF16)
    return wa, wz.astype(BF16), wb


def kernel(x_prompt, x_sample, state_gla, state_conv, c_prompt, c_sample, w_ada, b_ada, g_pre,
           g_post, w_ffn1_in, w_ffn1_out, w_ffn2_in, w_ffn2_out, w_mix_in, w_alpha, b_alpha,
           g_gla_norm, w_conv, w_branch_out, w_mix_out):
    depth = w_ada.shape[0]
    n_s = x_sample.shape[0]
    assert depth == 1 and x_sample.shape[1] == 1 and n_s == LANES
    assert c_prompt.shape[0] == SUBLANES and n_s % SUBLANES == 0
    prompt_row_block = n_s // SUBLANES

    yp = x_prompt
    ys = x_sample.reshape(1, n_s, D_MODEL)
    c_all = jnp.concatenate([c_sample, c_prompt], axis=0)
    outs = []
    for i in range(depth):
        ada = _ada(c_all, w_ada[i], b_ada[i])
        gpre = [g_pre[i, j].reshape(1, D_MODEL) for j in range(N_SUBLAYERS)]
        gpost = [g_post[i, j].reshape(1, D_MODEL) for j in range(N_SUBLAYERS)]
        ffn_w = [(w_ffn1_in[i].astype(BF16), w_ffn1_out[i].astype(BF16)),
                 (w_ffn2_in[i].astype(BF16), w_ffn2_out[i].astype(BF16))]
        wa, wz, wb = _split_mix_in(w_mix_in[i])
        walpha = jnp.pad(w_alpha[i], ((0, LANES - GATE_RANK), (0, 0))).astype(BF16)
        mix_w = (wa, wz, wb, walpha, b_alpha[i].reshape(1, QK_WIDTH),
                 g_gla_norm[i].reshape(1, V_WIDTH),
                 jnp.pad(w_conv[i], ((0, SUBLANES - CONV_WIDTH), (0, 0))),
                 w_branch_out[i].astype(BF16), w_mix_out[i].astype(BF16))

        ffn_p = functools.partial(_ffn, per_token=False, tm=FFN_TM,
                                  prompt_row_block=prompt_row_block)
        ffn_s = functools.partial(_ffn, per_token=True, tm=n_s,
                                  prompt_row_block=prompt_row_block)

        yp = ffn_p(yp, ada, 0, gpre[0], gpost[0], *ffn_w[0])
        yp, gla_p, conv_p = _mixer(yp, ada, gpre[1], gpost[1], mix_w,
                                   prompt_row_block=prompt_row_block)
        yp = ffn_p(yp, ada, 2, gpre[2], gpost[2], *ffn_w[1])

        ys = ffn_s(ys, ada, 0, gpre[0], gpost[0], *ffn_w[0])
        sconv_t = jnp.swapaxes(state_conv[i], 0, 1)
        ys2, gla_s, sconv_new = _sample_mixer(ys[0], ada, state_gla[i:i + 1], sconv_t,
                                              gpre[1], gpost[1], mix_w)
        ys = ffn_s(ys2.reshape(1, n_s, D_MODEL), ada, 2, gpre[2], gpost[2], *ffn_w[1])
        outs.append((gla_p, conv_p, gla_s, jnp.swapaxes(sconv_new, 0, 1)[None]))

    gla_p, conv_p, gla_s, conv_s = outs[0]
    return (yp, ys.reshape(n_s, 1, D_MODEL), gla_p, conv_p, gla_s, conv_s)
```

```python
import functools

import jax
import jax.numpy as jnp
from jax import lax
from jax.experimental import pallas as pl
from jax.experimental.pallas import tpu as pltpu

F32 = jnp.float32
BF16 = jnp.bfloat16

D_MODEL = 1024
D_FF = 2816
N_SUBLAYERS = 3
N_HEADS = 4
DK = 128
DV = 256
QK_WIDTH = N_HEADS * DK
V_WIDTH = N_HEADS * DV
GATE_RANK = 16
GATE_TAU = 16.0
D_CONV = D_MODEL
CONV_WIDTH = 3
RMS_EPS = 1e-6
MACARON_WEIGHT = 0.5

LANES = 128
SUBLANES = 8
GLA_CHUNK = 64
VMEM_LIMIT_BYTES = 56 * 1024 * 1024

FFN_TM = 512
MIX_TM = 256
ADA_TN = 1536
SAMPLE_BLOCK = 8

_A_WIDTH = 2 * QK_WIDTH + 2 * V_WIDTH
_B_WIDTH = 3 * D_CONV + 2 * D_MODEL


def _rms(x, g):
    return x * lax.rsqrt(jnp.mean(x * x, axis=-1, keepdims=True) + RMS_EPS) * g


def _silu(x):
    return x * jax.nn.sigmoid(x)


def _dot(a, b):
    return jnp.dot(a, b, preferred_element_type=F32)


def _dot_nt(a, b):
    return lax.dot_general(a, b, (((1,), (1,)), ((), ())), preferred_element_type=F32)


def _dot_tn(a, b):
    return lax.dot_general(a, b, (((0,), (0,)), ((), ())), preferred_element_type=F32)


def _resident(shape):
    zeros = (0,) * len(shape)
    return pl.BlockSpec(shape, lambda *_: zeros, pipeline_mode=pl.Buffered(1))


def _params(semantics):
    return pltpu.CompilerParams(dimension_semantics=semantics, vmem_limit_bytes=VMEM_LIMIT_BYTES)


def _ada_kernel(c_ref, w_ref, b_ref, o_ref):
    c = c_ref[...]
    o_ref[...] = _dot(_silu(c).astype(BF16), w_ref[...].astype(BF16)) + b_ref[...]


def _ada(c_all, w_ada, b_ada):
    rows = c_all.shape[0]
    width = w_ada.shape[1]
    return pl.pallas_call(
        _ada_kernel,
        grid=(width // ADA_TN,),
        in_specs=[
            pl.BlockSpec((rows, D_MODEL), lambda j: (0, 0)),
            pl.BlockSpec((D_MODEL, ADA_TN), lambda j: (0, j)),
            pl.BlockSpec((1, ADA_TN), lambda j: (0, j)),
        ],
        out_specs=pl.BlockSpec((rows, ADA_TN), lambda j: (0, j)),
        out_shape=jax.ShapeDtypeStruct((rows, width), F32),
        compiler_params=_params(("arbitrary",)),
        name="ada",
    )(c_all, w_ada, b_ada.reshape(1, width))


def _mod_specs(sub, per_token, prompt_row_block):
    specs = []
    for j in range(3):
        col = sub * 3 + j
        if per_token:
            specs.append(pl.BlockSpec((LANES, D_MODEL), lambda n, l, col=col: (0, col)))
        else:
            specs.append(pl.BlockSpec((SUBLANES, D_MODEL),
                                      lambda n, l, col=col: (prompt_row_block, col)))
    return specs


def _load_mod(ref, per_token):
    if per_token:
        return ref[...]
    return ref[pl.ds(pl.program_id(0), 1), :]


def _ffn_kernel(x_ref, sh_ref, sc_ref, gt_ref, gpre_ref, gpost_ref, win_ref, wout_ref, o_ref,
                *, per_token, weight):
    x = x_ref[0]
    shift = _load_mod(sh_ref, per_token)
    scale = _load_mod(sc_ref, per_token)
    gate = _load_mod(gt_ref, per_token)
    h = (_rms(x, gpre_ref[...]) * (1.0 + scale) + shift).astype(BF16)
    g = _dot(h, win_ref[:, :D_FF])
    u = _dot(h, win_ref[:, D_FF:])
    a = (_silu(g) * u).astype(BF16)
    out = _dot(a, wout_ref[...])
    o_ref[0] = x + weight * gate * _rms(out, gpost_ref[...])


def _ffn(x, ada, sub, g_pre, g_post, w_in, w_out, *, per_token, tm, prompt_row_block):
    n, l, _ = x.shape
    kern = functools.partial(_ffn_kernel, per_token=per_token, weight=MACARON_WEIGHT)
    return pl.pallas_call(
        kern,
        grid=(n, l // tm),
        in_specs=[pl.BlockSpec((1, tm, D_MODEL), lambda n, l: (n, l, 0))]
        + _mod_specs(sub, per_token, prompt_row_block)
        + [_resident((1, D_MODEL)), _resident((1, D_MODEL)),
           _resident((D_MODEL, 2 * D_FF)), _resident((D_FF, D_MODEL))],
        out_specs=pl.BlockSpec((1, tm, D_MODEL), lambda n, l: (n, l, 0)),
        out_shape=jax.ShapeDtypeStruct(x.shape, F32),
        compiler_params=_params(("arbitrary", "arbitrary")),
        name="ffn",
    )(x, ada, ada, ada, g_pre, g_post, w_in, w_out)


def _log_decay(hb, wz_ref, walpha_ref, balpha_ref):
    z = _dot(hb, wz_ref[...]).astype(BF16)
    xg = _dot(z, walpha_ref[...]) + balpha_ref[...]
    return jax.nn.log_sigmoid(xg) * (1.0 / GATE_TAU)


def _merge_and_project(y_gla, y_conv, sig_a, sig_b, wbo_ref, wmo_ref):
    pg = _dot(y_gla, wbo_ref[0])
    pc = _dot(y_conv, wbo_ref[1])
    merged = (sig_a * pg + sig_b * pc).astype(BF16)
    return _dot(merged, wmo_ref[...])


def _mixer_kernel(x_ref, sh_ref, sc_ref, gt_ref, gpre_ref, gpost_ref, wa_ref, wz_ref, wb_ref,
                  walpha_ref, balpha_ref, ggla_ref, wconv_ref, wbo_ref, wmo_ref,
                  y_ref, sgla_ref, sconv_ref, st_scr, u_scr, ygla_scr, *, tm):
    l = pl.program_id(1)
    n_l = pl.num_programs(1)

    @pl.when(l == 0)
    def _():
        st_scr[...] = jnp.zeros_like(st_scr)
        u_scr[0:SUBLANES, :] = jnp.zeros((SUBLANES, D_CONV), F32)

    x = x_ref[0]
    shift = _load_mod(sh_ref, False)
    scale = _load_mod(sc_ref, False)
    gate = _load_mod(gt_ref, False)
    hb = (_rms(x, gpre_ref[...]) * (1.0 + scale) + shift).astype(BF16)

    def proj_b(j):
        return _dot(hb, wb_ref[:, j * D_MODEL:(j + 1) * D_MODEL])

    nc = tm // GLA_CHUNK
    log_a = _log_decay(hb, wz_ref, walpha_ref, balpha_ref)
    pa = _dot(hb, wa_ref[...])
    q3 = pa[:, 0:QK_WIDTH].reshape(nc, GLA_CHUNK, QK_WIDTH)
    k3 = pa[:, QK_WIDTH:2 * QK_WIDTH].reshape(nc, GLA_CHUNK, QK_WIDTH)
    vb = pa[:, 2 * QK_WIDTH:2 * QK_WIDTH + V_WIDTH].astype(BF16)
    r = pa[:, 2 * QK_WIDTH + V_WIDTH:]

    row = lax.broadcasted_iota(jnp.int32, (tm, tm), 0)
    col = lax.broadcasted_iota(jnp.int32, (tm, tm), 1)
    tri = jnp.where((row >= col) & (row // GLA_CHUNK == col // GLA_CHUNK), 1.0, 0.0).astype(BF16)
    la_hi = log_a.astype(BF16)
    rem = log_a - la_hi.astype(F32)
    la_mid = rem.astype(BF16)
    la_lo = (rem - la_mid.astype(F32)).astype(BF16)
    b3 = (_dot(tri, la_hi) + _dot(tri, la_mid) + _dot(tri, la_lo)).reshape(
        nc, GLA_CHUNK, QK_WIDTH)

    u = proj_b(1) * proj_b(2)
    u_scr[SUBLANES:SUBLANES + tm, :] = u

    b_last = b3[:, GLA_CHUNK - 1:GLA_CHUNK, :]
    qd = (q3 * (jnp.exp(b3) * (DK ** -0.5))).astype(BF16)
    kd = (k3 * jnp.exp(-b3)).astype(BF16)
    kl = (k3 * jnp.exp(b_last - b3)).astype(BF16)
    b_last_rows = jnp.concatenate(
        [b_last[c, :, hd * DK:(hd + 1) * DK] for c in range(nc) for hd in range(N_HEADS)]
        + [jnp.zeros((LANES - nc * N_HEADS, DK), F32)], axis=0)
    decay_cols = jnp.exp(b_last_rows.T)

    crow = lax.broadcasted_iota(jnp.int32, (GLA_CHUNK, GLA_CHUNK), 0)
    ccol = lax.broadcasted_iota(jnp.int32, (GLA_CHUNK, GLA_CHUNK), 1)
    causal = crow >= ccol
    scores, upd = {}, {}
    for c in range(nc):
        rows = slice(c * GLA_CHUNK, (c + 1) * GLA_CHUNK)
        for hd in range(N_HEADS):
            kc = slice(hd * DK, (hd + 1) * DK)
            scores[c, hd] = _dot_nt(qd[c, :, kc], kd[c, :, kc])
            upd[c, hd] = _dot_tn(kl[c, :, kc], vb[rows, hd * DV:(hd + 1) * DV])

    conv = (wconv_ref[0:1, :] * u_scr[SUBLANES - 2:SUBLANES - 2 + tm, :]
            + wconv_ref[1:2, :] * u_scr[SUBLANES - 1:SUBLANES - 1 + tm, :]
            + wconv_ref[2:3, :] * u)
    u_scr[0:SUBLANES, :] = u_scr[tm:tm + SUBLANES, :]
    y_conv = (proj_b(0) * conv).astype(BF16)

    outs = {}
    for hd in range(N_HEADS):
        kc = slice(hd * DK, (hd + 1) * DK)
        vc = slice(hd * DV, (hd + 1) * DV)
        st = st_scr[hd]
        for c in range(nc):
            rows = slice(c * GLA_CHUNK, (c + 1) * GLA_CHUNK)
            s = jnp.where(causal, scores[c, hd], 0.0).astype(BF16)
            lhs = jnp.concatenate([qd[c, :, kc], s], axis=1)
            rhs = jnp.concatenate([st.astype(BF16), vb[rows, vc]], axis=0)
            outs[c, hd] = _dot(lhs, rhs)
            j = c * N_HEADS + hd
            st = st * decay_cols[:, j:j + 1] + upd[c, hd]
        st_scr[hd] = st

    sig_a = jax.nn.sigmoid(proj_b(3))
    sig_b = jax.nn.sigmoid(proj_b(4))

    for hd in range(N_HEADS):
        vc = slice(hd * DV, (hd + 1) * DV)
        for c in range(nc):
            rows = slice(c * GLA_CHUNK, (c + 1) * GLA_CHUNK)
            on = _rms(outs[c, hd], ggla_ref[:, vc])
            ygla_scr[rows, vc] = (on * _silu(r[rows, vc])).astype(BF16)

    mix = _merge_and_project(ygla_scr[...], y_conv, sig_a, sig_b, wbo_ref, wmo_ref)
    y_ref[0] = x + gate * _rms(mix, gpost_ref[...])

    @pl.when(l == n_l - 1)
    def _():
        sgla_ref[0, 0] = st_scr[...]
        sconv_ref[0, 0] = u_scr[SUBLANES - 2:SUBLANES, :]


def _mixer(x, ada, g_pre, g_post, wts, *, prompt_row_block):
    n, l, _ = x.shape
    tm = MIX_TM
    kern = functools.partial(_mixer_kernel, tm=tm)
    wa, wz, wb, walpha, balpha, ggla, wconv, wbo, wmo = wts
    return pl.pallas_call(
        kern,
        grid=(n, l // tm),
        in_specs=[pl.BlockSpec((1, tm, D_MODEL), lambda n, l: (n, l, 0))]
        + _mod_specs(1, False, prompt_row_block)
        + [_resident(a.shape) for a in (g_pre, g_post, wa, wz, wb, walpha, balpha, ggla, wconv,
                                        wbo, wmo)],
        out_specs=[
            pl.BlockSpec((1, tm, D_MODEL), lambda n, l: (n, l, 0)),
            pl.BlockSpec((1, 1, N_HEADS, DK, DV), lambda n, l: (0, n, 0, 0, 0)),
            pl.BlockSpec((1, 1, CONV_WIDTH - 1, D_CONV), lambda n, l: (0, n, 0, 0)),
        ],
        out_shape=[
            jax.ShapeDtypeStruct(x.shape, F32),
            jax.ShapeDtypeStruct((1, n, N_HEADS, DK, DV), F32),
            jax.ShapeDtypeStruct((1, n, CONV_WIDTH - 1, D_CONV), F32),
        ],
        scratch_shapes=[
            pltpu.VMEM((N_HEADS, DK, DV), F32),
            pltpu.VMEM((tm + SUBLANES, D_CONV), F32),
            pltpu.VMEM((tm, V_WIDTH), BF16),
        ],
        compiler_params=_params(("arbitrary", "arbitrary")),
        name="mixer",
    )(x, ada, ada, ada, g_pre, g_post, wa, wz, wb, walpha, balpha, ggla, wconv, wbo, wmo)


def _smix_pre_kernel(x_ref, sh_ref, sc_ref, gpre_ref, wa_ref, wz_ref, wb_ref, walpha_ref,
                     balpha_ref, wconv_ref, sconv_ref,
                     qt_ref, kt_ref, at_ref, v_ref, sr_ref, yconv_ref, sga_ref, sgb_ref,
                     sconv_new_ref):
    x = x_ref[...]
    hb = (_rms(x, gpre_ref[...]) * (1.0 + sc_ref[...]) + sh_ref[...]).astype(BF16)
    pa = _dot(hb, wa_ref[...])
    q = pa[:, 0:QK_WIDTH] * (DK ** -0.5)
    k = pa[:, QK_WIDTH:2 * QK_WIDTH]
    v_ref[...] = pa[:, 2 * QK_WIDTH:2 * QK_WIDTH + V_WIDTH]
    sr_ref[...] = _silu(pa[:, 2 * QK_WIDTH + V_WIDTH:])
    a = jnp.exp(_log_decay(hb, wz_ref, walpha_ref, balpha_ref))
    for hd in range(N_HEADS):
        kc = slice(hd * DK, (hd + 1) * DK)
        qt_ref[hd] = q[:, kc].T
        kt_ref[hd] = k[:, kc].T
        at_ref[hd] = a[:, kc].T

    pb = _dot(hb, wb_ref[...])
    u = pb[:, D_CONV:2 * D_CONV] * pb[:, 2 * D_CONV:3 * D_CONV]
    conv = (wconv_ref[0:1, :] * sconv_ref[0] + wconv_ref[1:2, :] * sconv_ref[1]
            + wconv_ref[2:3, :] * u)
    yconv_ref[...] = pb[:, 0:D_CONV] * conv
    sga_ref[...] = jax.nn.sigmoid(pb[:, 3 * D_CONV:3 * D_CONV + D_MODEL])
    sgb_ref[...] = jax.nn.sigmoid(pb[:, 3 * D_CONV + D_MODEL:])
    sconv_new_ref[0] = sconv_ref[1]
    sconv_new_ref[1] = u


def _smix_state_kernel(s_ref, qt_ref, kt_ref, at_ref, v_ref, snew_ref, o_ref):
    base = pl.program_id(0) * SAMPLE_BLOCK
    lane = lax.broadcasted_iota(jnp.int32, (1, LANES), 1)
    for i in range(SAMPLE_BLOCK):
        pick = jnp.where(lane == base + i, 1.0, 0.0)
        for hd in range(N_HEADS):
            vc = slice(hd * DV, (hd + 1) * DV)
            a_col = jnp.sum(at_ref[hd] * pick, axis=1, keepdims=True)
            k_col = jnp.sum(kt_ref[hd] * pick, axis=1, keepdims=True)
            q_col = jnp.sum(qt_ref[hd] * pick, axis=1, keepdims=True)
            s_new = a_col * s_ref[0, i, hd] + k_col * v_ref[i:i + 1, vc]
            snew_ref[0, i, hd] = s_new
            o_ref[i:i + 1, vc] = jnp.sum(q_col * s_new, axis=0, keepdims=True)


def _smix_post_kernel(x_ref, gt_ref, gpost_ref, o_ref, sr_ref, yconv_ref, sga_ref, sgb_ref,
                      ggla_ref, wbo_ref, wmo_ref, y_ref, ygla_scr):
    for hd in range(N_HEADS):
        vc = slice(hd * DV, (hd + 1) * DV)
        on = _rms(o_ref[:, vc], ggla_ref[:, vc])
        ygla_scr[:, vc] = (on * sr_ref[:, vc]).astype(BF16)
    mix = _merge_and_project(ygla_scr[...], yconv_ref[...].astype(BF16), sga_ref[...],
                             sgb_ref[...], wbo_ref, wmo_ref)
    y_ref[...] = x_ref[...] + gt_ref[...] * _rms(mix, gpost_ref[...])


def _sample_mixer(x, ada, state_gla, sconv_t, g_pre, g_post, wts):
    s = x.shape[0]
    wa, wz, wb, walpha, balpha, ggla, wconv, wbo, wmo = wts
    whole = lambda a: pl.BlockSpec(a.shape, lambda *_: (0,) * a.ndim)
    mod = lambda j: pl.BlockSpec((s, D_MODEL), lambda *_: (0, 3 + j))
    tok = jax.ShapeDtypeStruct((s, D_MODEL), F32)
    tr = jax.ShapeDtypeStruct((N_HEADS, DK, s), F32)

    pre_in = (x, ada, ada, g_pre, wa, wz, wb, walpha, balpha, wconv, sconv_t)
    pre_specs = [whole(x), mod(0), mod(1)] + [whole(a) for a in pre_in[3:]]
    pre_out = [tr, tr, tr, tok, tok, tok, tok, tok, jax.ShapeDtypeStruct(sconv_t.shape, F32)]
    qt, kt, at, v, sr, yconv, sga, sgb, sconv_new = pl.pallas_call(
        _smix_pre_kernel,
        grid=(1,),
        in_specs=pre_specs,
        out_specs=[whole(o) for o in pre_out],
        out_shape=pre_out,
        compiler_params=_params(("arbitrary",)),
        name="smix_pre",
    )(*pre_in)

    state_block = pl.BlockSpec((1, SAMPLE_BLOCK, N_HEADS, DK, DV), lambda j: (0, j, 0, 0, 0))
    snew, o = pl.pallas_call(
        _smix_state_kernel,
        grid=(s // SAMPLE_BLOCK,),
        in_specs=[state_block, whole(qt), whole(kt), whole(at),
                  pl.BlockSpec((SAMPLE_BLOCK, V_WIDTH), lambda j: (j, 0))],
        out_specs=[state_block, pl.BlockSpec((SAMPLE_BLOCK, V_WIDTH), lambda j: (j, 0))],
        out_shape=[jax.ShapeDtypeStruct(state_gla.shape, F32), tok],
        compiler_params=_params(("arbitrary",)),
        name="smix_state",
    )(state_gla, qt, kt, at, v)

    post_in = (x, ada, g_post, o, sr, yconv, sga, sgb, ggla, wbo, wmo)
    post_specs = [whole(x), mod(2)] + [whole(a) for a in post_in[2:]]
    y = pl.pallas_call(
        _smix_post_kernel,
        grid=(1,),
        in_specs=post_specs,
        out_specs=whole(tok),
        out_shape=tok,
        scratch_shapes=[pltpu.VMEM((s, V_WIDTH), BF16)],
        compiler_params=_params(("arbitrary",)),
        name="smix_post",
    )(*post_in)
    return y, snew, sconv_new


def _split_mix_in(w_mix_in):
    wa = w_mix_in[:, :_A_WIDTH].astype(BF16)
    wz = jnp.pad(w_mix_in[:, _A_WIDTH:_A_WIDTH + GATE_RANK], ((0, 0), (0, LANES - GATE_RANK)))
    wb = w_mix_in[:, _A_WIDTH + GATE_RANK:].astype(BF16)
    return wa, wz.astype(BF16), wb


def kernel(x_prompt, x_sample, state_gla, state_conv, c_prompt, c_sample, w_ada, b_ada, g_pre,
           g_post, w_ffn1_in, w_ffn1_out, w_ffn2_in, w_ffn2_out, w_mix_in, w_alpha, b_alpha,
           g_gla_norm, w_conv, w_branch_out, w_mix_out):
    depth = w_ada.shape[0]
    n_s = x_sample.shape[0]
    assert depth == 1 and x_sample.shape[1] == 1 and n_s == LANES
    assert c_prompt.shape[0] == SUBLANES and n_s % SUBLANES == 0
    prompt_row_block = n_s // SUBLANES

    yp = x_prompt
    ys = x_sample.reshape(1, n_s, D_MODEL)
    c_all = jnp.concatenate([c_sample, c_prompt], axis=0)
    outs = []
    for i in range(depth):
        ada = _ada(c_all, w_ada[i], b_ada[i])
        gpre = [g_pre[i, j].reshape(1, D_MODEL) for j in range(N_SUBLAYERS)]
        gpost = [g_post[i, j].reshape(1, D_MODEL) for j in range(N_SUBLAYERS)]
        ffn_w = [(w_ffn1_in[i].astype(BF16), w_ffn1_out[i].astype(BF16)),
                 (w_ffn2_in[i].astype(BF16), w_ffn2_out[i].astype(BF16))]
        wa, wz, wb = _split_mix_in(w_mix_in[i])
        walpha = jnp.pad(w_alpha[i], ((0, LANES - GATE_RANK), (0, 0))).astype(BF16)
        mix_w = (wa, wz, wb, walpha, b_alpha[i].reshape(1, QK_WIDTH),
                 g_gla_norm[i].reshape(1, V_WIDTH),
                 jnp.pad(w_conv[i], ((0, SUBLANES - CONV_WIDTH), (0, 0))),
                 w_branch_out[i].astype(BF16), w_mix_out[i].astype(BF16))

        ffn_p = functools.partial(_ffn, per_token=False, tm=FFN_TM,
                                  prompt_row_block=prompt_row_block)
        ffn_s = functools.partial(_ffn, per_token=True, tm=n_s,
                                  prompt_row_block=prompt_row_block)

        yp = ffn_p(yp, ada, 0, gpre[0], gpost[0], *ffn_w[0])
        yp, gla_p, conv_p = _mixer(yp, ada, gpre[1], gpost[1], mix_w,
                                   prompt_row_block=prompt_row_block)
        yp = ffn_p(yp, ada, 2, gpre[2], gpost[2], *ffn_w[1])

        ys = ffn_s(ys, ada, 0, gpre[0], gpost[0], *ffn_w[0])
        sconv_t = jnp.swapaxes(state_conv[i], 0, 1)
        ys2, gla_s, sconv_new = _sample_mixer(ys[0], ada, state_gla[i:i + 1], sconv_t,
                                              gpre[1], gpost[1], mix_w)
        ys = ffn_s(ys2.reshape(1, n_s, D_MODEL), ada, 2, gpre[2], gpost[2], *ffn_w[1])
        outs.append((gla_p, conv_p, gla_s, jnp.swapaxes(sconv_new, 0, 1)[None]))

    gla_p, conv_p, gla_s, conv_s = outs[0]
    return (yp, ys.reshape(n_s, 1, D_MODEL), gla_p, conv_p, gla_s, conv_s)
```

```python
import functools

import jax
import jax.numpy as jnp
from jax import lax
from jax.experimental import pallas as pl
from jax.experimental.pallas import tpu as pltpu

F32 = jnp.float32
BF16 = jnp.bfloat16

D_MODEL = 1024
D_FF = 2816
N_SUBLAYERS = 3
N_HEADS = 4
DK = 128
DV = 256
QK_WIDTH = N_HEADS * DK
V_WIDTH = N_HEADS * DV
GATE_RANK = 16
GATE_TAU = 16.0
D_CONV = D_MODEL
CONV_WIDTH = 3
RMS_EPS = 1e-6
MACARON_WEIGHT = 0.5

LANES = 128
SUBLANES = 8
GLA_CHUNK = 64
VMEM_LIMIT_BYTES = 56 * 1024 * 1024

FFN_TM = 512
MIX_TM = 256
ADA_TN = 1536
SAMPLE_BLOCK = 8

_A_WIDTH = 2 * QK_WIDTH + 2 * V_WIDTH
_B_WIDTH = 3 * D_CONV + 2 * D_MODEL


def _rms(x, g):
    return x * lax.rsqrt(jnp.mean(x * x, axis=-1, keepdims=True) + RMS_EPS) * g


def _silu(x):
    return x * jax.nn.sigmoid(x)


def _dot(a, b):
    return jnp.dot(a, b, preferred_element_type=F32)


def _dot_nt(a, b):
    return lax.dot_general(a, b, (((1,), (1,)), ((), ())), preferred_element_type=F32)


def _dot_tn(a, b):
    return lax.dot_general(a, b, (((0,), (0,)), ((), ())), preferred_element_type=F32)


def _resident(shape):
    zeros = (0,) * len(shape)
    return pl.BlockSpec(shape, lambda *_: zeros, pipeline_mode=pl.Buffered(1))


def _params(semantics):
    return pltpu.CompilerParams(dimension_semantics=semantics, vmem_limit_bytes=VMEM_LIMIT_BYTES)


def _ada_kernel(c_ref, w_ref, b_ref, o_ref):
    c = c_ref[...]
    o_ref[...] = _dot(_silu(c).astype(BF16), w_ref[...].astype(BF16)) + b_ref[...]


def _ada(c_all, w_ada, b_ada):
    rows = c_all.shape[0]
    width = w_ada.shape[1]
    return pl.pallas_call(
        _ada_kernel,
        grid=(width // ADA_TN,),
        in_specs=[
            pl.BlockSpec((rows, D_MODEL), lambda j: (0, 0)),
            pl.BlockSpec((D_MODEL, ADA_TN), lambda j: (0, j)),
            pl.BlockSpec((1, ADA_TN), lambda j: (0, j)),
        ],
        out_specs=pl.BlockSpec((rows, ADA_TN), lambda j: (0, j)),
        out_shape=jax.ShapeDtypeStruct((rows, width), F32),
        compiler_params=_params(("arbitrary",)),
        name="ada",
    )(c_all, w_ada, b_ada.reshape(1, width))


def _mod_specs(sub, per_token, prompt_row_block):
    specs = []
    for j in range(3):
        col = sub * 3 + j
        if per_token:
            specs.append(pl.BlockSpec((LANES, D_MODEL), lambda n, l, col=col: (0, col)))
        else:
            specs.append(pl.BlockSpec((SUBLANES, D_MODEL),
                                      lambda n, l, col=col: (prompt_row_block, col)))
    return specs


def _load_mod(ref, per_token):
    if per_token:
        return ref[...]
    return ref[pl.ds(pl.program_id(0), 1), :]


def _ffn_kernel(x_ref, sh_ref, sc_ref, gt_ref, gpre_ref, gpost_ref, win_ref, wout_ref, o_ref,
                *, per_token, weight):
    x = x_ref[0]
    shift = _load_mod(sh_ref, per_token)
    scale = _load_mod(sc_ref, per_token)
    gate = _load_mod(gt_ref, per_token)
    h = (_rms(x, gpre_ref[...]) * (1.0 + scale) + shift).astype(BF16)
    g = _dot(h, win_ref[:, :D_FF])
    u = _dot(h, win_ref[:, D_FF:])
    a = (_silu(g) * u).astype(BF16)
    out = _dot(a, wout_ref[...])
    o_ref[0] = x + weight * gate * _rms(out, gpost_ref[...])


def _ffn(x, ada, sub, g_pre, g_post, w_in, w_out, *, per_token, tm, prompt_row_block):
    n, l, _ = x.shape
    kern = functools.partial(_ffn_kernel, per_token=per_token, weight=MACARON_WEIGHT)
    return pl.pallas_call(
        kern,
        grid=(n, l // tm),
        in_specs=[pl.BlockSpec((1, tm, D_MODEL), lambda n, l: (n, l, 0))]
        + _mod_specs(sub, per_token, prompt_row_block)
        + [_resident((1, D_MODEL)), _resident((1, D_MODEL)),
           _resident((D_MODEL, 2 * D_FF)), _resident((D_FF, D_MODEL))],
        out_specs=pl.BlockSpec((1, tm, D_MODEL), lambda n, l: (n, l, 0)),
        out_shape=jax.ShapeDtypeStruct(x.shape, F32),
        compiler_params=_params(("arbitrary", "arbitrary")),
        name="ffn",
    )(x, ada, ada, ada, g_pre, g_post, w_in, w_out)


def _log_decay(hb, wz_ref, walpha_ref, balpha_ref):
    z = _dot(hb, wz_ref[...]).astype(BF16)
    xg = _dot(z, walpha_ref[...]) + balpha_ref[...]
    return jax.nn.log_sigmoid(xg) * (1.0 / GATE_TAU)


def _merge_and_project(y_gla, y_conv, sig_a, sig_b, wbo_ref, wmo_ref):
    pg = _dot(y_gla, wbo_ref[0])
    pc = _dot(y_conv, wbo_ref[1])
    merged = (sig_a * pg + sig_b * pc).astype(BF16)
    return _dot(merged, wmo_ref[...])


_PAIR_LEVELS = tuple(GLA_CHUNK >> i for i in range(1, GLA_CHUNK.bit_length()))


def _pair_operands(qs, k, b, log_a, m, tm):
    n = tm // (2 * m)
    if m >= SUBLANES:
        b4 = b.reshape(n, 2 * m, QK_WIDTH)
        w = jnp.exp(-jnp.abs(b4 - b4[:, m - 1:m, :]))
        halves = lambda a: a.reshape(n, 2, m, QK_WIDTH)
        src = jnp.concatenate([halves(k)[:, 0:1], halves(qs)[:, 1:2]], axis=1)
        return (src.reshape(n, 2 * m, QK_WIDTH) * w).astype(BF16).reshape(tm, QK_WIDTH)
    row = lax.broadcasted_iota(jnp.int32, (tm, QK_WIDTH), 0)
    second = (row & m) != 0
    if m == 1:
        w = jnp.exp(jnp.where(second, log_a, 0.0))
    else:
        b4 = b.reshape(tm // SUBLANES, SUBLANES, QK_WIDTH)
        sub = lax.broadcasted_iota(jnp.int32, b4.shape, 1)
        mid = b4[:, m - 1:m, :]
        for start in range(2 * m, SUBLANES, 2 * m):
            mid = jnp.where(sub >= start, b4[:, start + m - 1:start + m, :], mid)
        w = jnp.exp(-jnp.abs(b4 - mid)).reshape(tm, QK_WIDTH)
    return (jnp.where(second, qs, k) * w).astype(BF16)


def _pair_mask(m):
    t = lax.broadcasted_iota(jnp.int32, (GLA_CHUNK, GLA_CHUNK), 0)
    s = lax.broadcasted_iota(jnp.int32, (GLA_CHUNK, GLA_CHUNK), 1)
    return ((t & m) != 0) & ((s & m) == 0) & ((t // (2 * m)) == (s // (2 * m)))


def _mixer_kernel(x_ref, sh_ref, sc_ref, gt_ref, gpre_ref, gpost_ref, wa_ref, wz_ref, wb_ref,
                  walpha_ref, balpha_ref, ggla_ref, wconv_ref, wbo_ref, wmo_ref,
                  y_ref, sgla_ref, sconv_ref, st_scr, u_scr, ygla_scr, *, tm):
    l = pl.program_id(1)
    n_l = pl.num_programs(1)

    @pl.when(l == 0)
    def _():
        st_scr[...] = jnp.zeros_like(st_scr)
        u_scr[0:SUBLANES, :] = jnp.zeros((SUBLANES, D_CONV), F32)

    x = x_ref[0]
    shift = _load_mod(sh_ref, False)
    scale = _load_mod(sc_ref, False)
    gate = _load_mod(gt_ref, False)
    hb = (_rms(x, gpre_ref[...]) * (1.0 + scale) + shift).astype(BF16)

    def proj_b(j):
        return _dot(hb, wb_ref[:, j * D_MODEL:(j + 1) * D_MODEL])

    nc = tm // GLA_CHUNK
    log_a = _log_decay(hb, wz_ref, walpha_ref, balpha_ref)
    pa = _dot(hb, wa_ref[...])
    qs = pa[:, 0:QK_WIDTH] * (DK ** -0.5)
    k = pa[:, QK_WIDTH:2 * QK_WIDTH]
    vb = pa[:, 2 * QK_WIDTH:2 * QK_WIDTH + V_WIDTH].astype(BF16)
    r = pa[:, 2 * QK_WIDTH + V_WIDTH:]

    row = lax.broadcasted_iota(jnp.int32, (tm, tm), 0)
    col = lax.broadcasted_iota(jnp.int32, (tm, tm), 1)
    tri = jnp.where((row >= col) & (row // GLA_CHUNK == col // GLA_CHUNK), 1.0, 0.0).astype(BF16)
    la_hi = log_a.astype(BF16)
    rem = log_a - la_hi.astype(F32)
    la_mid = rem.astype(BF16)
    la_lo = (rem - la_mid.astype(F32)).astype(BF16)
    b3 = (_dot(tri, la_hi) + _dot(tri, la_mid) + _dot(tri, la_lo)).reshape(
        nc, GLA_CHUNK, QK_WIDTH)

    u = proj_b(1) * proj_b(2)
    u_scr[SUBLANES:SUBLANES + tm, :] = u

    b_last = b3[:, GLA_CHUNK - 1:GLA_CHUNK, :]
    b = b3.reshape(tm, QK_WIDTH)
    qd = (qs * jnp.exp(b)).astype(BF16).reshape(nc, GLA_CHUNK, QK_WIDTH)
    kl = (k.reshape(nc, GLA_CHUNK, QK_WIDTH) * jnp.exp(b_last - b3)).astype(BF16)
    b_last_rows = jnp.concatenate(
        [b_last[c, :, hd * DK:(hd + 1) * DK] for c in range(nc) for hd in range(N_HEADS)]
        + [jnp.zeros((LANES - nc * N_HEADS, DK), F32)], axis=0)
    decay_cols = jnp.exp(b_last_rows.T)

    pair_ops = [_pair_operands(qs, k, b, log_a, m, tm) for m in _PAIR_LEVELS]
    pair_masks = [_pair_mask(m) for m in _PAIR_LEVELS]
    qs_b, k_b = qs.astype(BF16), k.astype(BF16)
    crow = lax.broadcasted_iota(jnp.int32, (GLA_CHUNK, GLA_CHUNK), 0)
    ccol = lax.broadcasted_iota(jnp.int32, (GLA_CHUNK, GLA_CHUNK), 1)
    scores, upd = {}, {}
    for c in range(nc):
        rows = slice(c * GLA_CHUNK, (c + 1) * GLA_CHUNK)
        for hd in range(N_HEADS):
            kc = slice(hd * DK, (hd + 1) * DK)
            s = jnp.where(crow == ccol, _dot_nt(qs_b[rows, kc], k_b[rows, kc]), 0.0)
            for p, mask in zip(pair_ops, pair_masks):
                s = s + jnp.where(mask, _dot_nt(p[rows, kc], p[rows, kc]), 0.0)
            scores[c, hd] = s.astype(BF16)
            upd[c, hd] = _dot_tn(kl[c, :, kc], vb[rows, hd * DV:(hd + 1) * DV])

    conv = (wconv_ref[0:1, :] * u_scr[SUBLANES - 2:SUBLANES - 2 + tm, :]
            + wconv_ref[1:2, :] * u_scr[SUBLANES - 1:SUBLANES - 1 + tm, :]
            + wconv_ref[2:3, :] * u)
    u_scr[0:SUBLANES, :] = u_scr[tm:tm + SUBLANES, :]
    y_conv = (proj_b(0) * conv).astype(BF16)

    outs = {}
    for hd in range(N_HEADS):
        kc = slice(hd * DK, (hd + 1) * DK)
        vc = slice(hd * DV, (hd + 1) * DV)
        st = st_scr[hd]
        for c in range(nc):
            rows = slice(c * GLA_CHUNK, (c + 1) * GLA_CHUNK)
            lhs = jnp.concatenate([qd[c, :, kc], scores[c, hd]], axis=1)
            rhs = jnp.concatenate([st.astype(BF16), vb[rows, vc]], axis=0)
            outs[c, hd] = _dot(lhs, rhs)
            j = c * N_HEADS + hd
            st = st * decay_cols[:, j:j + 1] + upd[c, hd]
        st_scr[hd] = st

    sig_a = jax.nn.sigmoid(proj_b(3))
    sig_b = jax.nn.sigmoid(proj_b(4))

    for hd in range(N_HEADS):
        vc = slice(hd * DV, (hd + 1) * DV)
        for c in range(nc):
            rows = slice(c * GLA_CHUNK, (c + 1) * GLA_CHUNK)
            on = _rms(outs[c, hd], ggla_ref[:, vc])
            ygla_scr[rows, vc] = (on * _silu(r[rows, vc])).astype(BF16)

    mix = _merge_and_project(ygla_scr[...], y_conv, sig_a, sig_b, wbo_ref, wmo_ref)
    y_ref[0] = x + gate * _rms(mix, gpost_ref[...])

    @pl.when(l == n_l - 1)
    def _():
        sgla_ref[0, 0] = st_scr[...]
        sconv_ref[0, 0] = u_scr[SUBLANES - 2:SUBLANES, :]


def _mixer(x, ada, g_pre, g_post, wts, *, prompt_row_block):
    n, l, _ = x.shape
    tm = MIX_TM
    kern = functools.partial(_mixer_kernel, tm=tm)
    wa, wz, wb, walpha, balpha, ggla, wconv, wbo, wmo = wts
    return pl.pallas_call(
        kern,
        grid=(n, l // tm),
        in_specs=[pl.BlockSpec((1, tm, D_MODEL), lambda n, l: (n, l, 0))]
        + _mod_specs(1, False, prompt_row_block)
        + [_resident(a.shape) for a in (g_pre, g_post, wa, wz, wb, walpha, balpha, ggla, wconv,
                                        wbo, wmo)],
        out_specs=[
            pl.BlockSpec((1, tm, D_MODEL), lambda n, l: (n, l, 0)),
            pl.BlockSpec((1, 1, N_HEADS, DK, DV), lambda n, l: (0, n, 0, 0, 0)),
            pl.BlockSpec((1, 1, CONV_WIDTH - 1, D_CONV), lambda n, l: (0, n, 0, 0)),
        ],
        out_shape=[
            jax.ShapeDtypeStruct(x.shape, F32),
            jax.ShapeDtypeStruct((1, n, N_HEADS, DK, DV), F32),
            jax.ShapeDtypeStruct((1, n, CONV_WIDTH - 1, D_CONV), F32),
        ],
        scratch_shapes=[
            pltpu.VMEM((N_HEADS, DK, DV), F32),
            pltpu.VMEM((tm + SUBLANES, D_CONV), F32),
            pltpu.VMEM((tm, V_WIDTH), BF16),
        ],
        compiler_params=_params(("arbitrary", "arbitrary")),
        name="mixer",
    )(x, ada, ada, ada, g_pre, g_post, wa, wz, wb, walpha, balpha, ggla, wconv, wbo, wmo)


def _smix_pre_kernel(x_ref, sh_ref, sc_ref, gpre_ref, wa_ref, wz_ref, wb_ref, walpha_ref,
                     balpha_ref, wconv_ref, sconv_ref,
                     qt_ref, kt_ref, at_ref, v_ref, sr_ref, yconv_ref, sga_ref, sgb_ref,
                     sconv_new_ref):
    x = x_ref[...]
    hb = (_rms(x, gpre_ref[...]) * (1.0 + sc_ref[...]) + sh_ref[...]).astype(BF16)
    pa = _dot(hb, wa_ref[...])
    q = pa[:, 0:QK_WIDTH] * (DK ** -0.5)
    k = pa[:, QK_WIDTH:2 * QK_WIDTH]
    v_ref[...] = pa[:, 2 * QK_WIDTH:2 * QK_WIDTH + V_WIDTH]
    sr_ref[...] = _silu(pa[:, 2 * QK_WIDTH + V_WIDTH:])
    a = jnp.exp(_log_decay(hb, wz_ref, walpha_ref, balpha_ref))
    for hd in range(N_HEADS):
        kc = slice(hd * DK, (hd + 1) * DK)
        qt_ref[hd] = q[:, kc].T
        kt_ref[hd] = k[:, kc].T
        at_ref[hd] = a[:, kc].T

    pb = _dot(hb, wb_ref[...])
    u = pb[:, D_CONV:2 * D_CONV] * pb[:, 2 * D_CONV:3 * D_CONV]
    conv = (wconv_ref[0:1, :] * sconv_ref[0] + wconv_ref[1:2, :] * sconv_ref[1]
            + wconv_ref[2:3, :] * u)
    yconv_ref[...] = pb[:, 0:D_CONV] * conv
    sga_ref[...] = jax.nn.sigmoid(pb[:, 3 * D_CONV:3 * D_CONV + D_MODEL])
    sgb_ref[...] = jax.nn.sigmoid(pb[:, 3 * D_CONV + D_MODEL:])
    sconv_new_ref[0] = sconv_ref[1]
    sconv_new_ref[1] = u


def _smix_state_kernel(s_ref, qt_ref, kt_ref, at_ref, v_ref, snew_ref, o_ref):
    base = pl.program_id(0) * SAMPLE_BLOCK
    lane = lax.broadcasted_iota(jnp.int32, (1, LANES), 1)
    for i in range(SAMPLE_BLOCK):
        pick = jnp.where(lane == base + i, 1.0, 0.0)
        for hd in range(N_HEADS):
            vc = slice(hd * DV, (hd + 1) * DV)
            a_col = jnp.sum(at_ref[hd] * pick, axis=1, keepdims=True)
            k_col = jnp.sum(kt_ref[hd] * pick, axis=1, keepdims=True)
            q_col = jnp.sum(qt_ref[hd] * pick, axis=1, keepdims=True)
            s_new = a_col * s_ref[0, i, hd] + k_col * v_ref[i:i + 1, vc]
            snew_ref[0, i, hd] = s_new
            o_ref[i:i + 1, vc] = jnp.sum(q_col * s_new, axis=0, keepdims=True)


def _smix_post_kernel(x_ref, gt_ref, gpost_ref, o_ref, sr_ref, yconv_ref, sga_ref, sgb_ref,
                      ggla_ref, wbo_ref, wmo_ref, y_ref, ygla_scr):
    for hd in range(N_HEADS):
        vc = slice(hd * DV, (hd + 1) * DV)
        on = _rms(o_ref[:, vc], ggla_ref[:, vc])
        ygla_scr[:, vc] = (on * sr_ref[:, vc]).astype(BF16)
    mix = _merge_and_project(ygla_scr[...], yconv_ref[...].astype(BF16), sga_ref[...],
                             sgb_ref[...], wbo_ref, wmo_ref)
    y_ref[...] = x_ref[...] + gt_ref[...] * _rms(mix, gpost_ref[...])


def _sample_mixer(x, ada, state_gla, sconv_t, g_pre, g_post, wts):
    s = x.shape[0]
    wa, wz, wb, walpha, balpha, ggla, wconv, wbo, wmo = wts
    whole = lambda a: pl.BlockSpec(a.shape, lambda *_: (0,) * a.ndim)
    mod = lambda j: pl.BlockSpec((s, D_MODEL), lambda *_: (0, 3 + j))
    tok = jax.ShapeDtypeStruct((s, D_MODEL), F32)
    tr = jax.ShapeDtypeStruct((N_HEADS, DK, s), F32)

    pre_in = (x, ada, ada, g_pre, wa, wz, wb, walpha, balpha, wconv, sconv_t)
    pre_specs = [whole(x), mod(0), mod(1)] + [whole(a) for a in pre_in[3:]]
    pre_out = [tr, tr, tr, tok, tok, tok, tok, tok, jax.ShapeDtypeStruct(sconv_t.shape, F32)]
    qt, kt, at, v, sr, yconv, sga, sgb, sconv_new = pl.pallas_call(
        _smix_pre_kernel,
        grid=(1,),
        in_specs=pre_specs,
        out_specs=[whole(o) for o in pre_out],
        out_shape=pre_out,
        compiler_params=_params(("arbitrary",)),
        name="smix_pre",
    )(*pre_in)

    state_block = pl.BlockSpec((1, SAMPLE_BLOCK, N_HEADS, DK, DV), lambda j: (0, j, 0, 0, 0))
    snew, o = pl.pallas_call(
        _smix_state_kernel,
        grid=(s // SAMPLE_BLOCK,),
        in_specs=[state_block, whole(qt), whole(kt), whole(at),
                  pl.BlockSpec((SAMPLE_BLOCK, V_WIDTH), lambda j: (j, 0))],
        out_specs=[state_block, pl.BlockSpec((SAMPLE_BLOCK, V_WIDTH), lambda j: (j, 0))],
        out_shape=[jax.ShapeDtypeStruct(state_gla.shape, F32), tok],
        compiler_params=_params(("arbitrary",)),
        name="smix_state",
    )(state_gla, qt, kt, at, v)

    post_in = (x, ada, g_post, o, sr, yconv, sga, sgb, ggla, wbo, wmo)
    post_specs = [whole(x), mod(2)] + [whole(a) for a in post_in[2:]]
    y = pl.pallas_call(
        _smix_post_kernel,
        grid=(1,),
        in_specs=post_specs,
        out_specs=whole(tok),
        out_shape=tok,
        scratch_shapes=[pltpu.VMEM((s, V_WIDTH), BF16)],
        compiler_params=_params(("arbitrary",)),
        name="smix_post",
    )(*post_in)
    return y, snew, sconv_new


def _split_mix_in(w_mix_in):
    wa = w_mix_in[:, :_A_WIDTH].astype(BF16)
    wz = jnp.pad(w_mix_in[:, _A_WIDTH:_A_WIDTH + GATE_RANK], ((0, 0), (0, LANES - GATE_RANK)))
    wb = w_mix_in[:, _A_WIDTH + GATE_RANK:].astype(BF16)
    return wa, wz.astype(BF16), wb


def kernel(x_prompt, x_sample, state_gla, state_conv, c_prompt, c_sample, w_ada, b_ada, g_pre,
           g_post, w_ffn1_in, w_ffn1_out, w_ffn2_in, w_ffn2_out, w_mix_in, w_alpha, b_alpha,
           g_gla_norm, w_conv, w_branch_out, w_mix_out):
    depth = w_ada.shape[0]
    n_s = x_sample.shape[0]
    assert depth == 1 and x_sample.shape[1] == 1 and n_s == LANES
    assert c_prompt.shape[0] == SUBLANES and n_s % SUBLANES == 0
    prompt_row_block = n_s // SUBLANES

    yp = x_prompt
    ys = x_sample.reshape(1, n_s, D_MODEL)
    c_all = jnp.concatenate([c_sample, c_prompt], axis=0)
    outs = []
    for i in range(depth):
        ada = _ada(c_all, w_ada[i], b_ada[i])
        gpre = [g_pre[i, j].reshape(1, D_MODEL) for j in range(N_SUBLAYERS)]
        gpost = [g_post[i, j].reshape(1, D_MODEL) for j in range(N_SUBLAYERS)]
        ffn_w = [(w_ffn1_in[i].astype(BF16), w_ffn1_out[i].astype(BF16)),
                 (w_ffn2_in[i].astype(BF16), w_ffn2_out[i].astype(BF16))]
        wa, wz, wb = _split_mix_in(w_mix_in[i])
        walpha = jnp.pad(w_alpha[i], ((0, LANES - GATE_RANK), (0, 0))).astype(BF16)
        mix_w = (wa, wz, wb, walpha, b_alpha[i].reshape(1, QK_WIDTH),
                 g_gla_norm[i].reshape(1, V_WIDTH),
                 jnp.pad(w_conv[i], ((0, SUBLANES - CONV_WIDTH), (0, 0))),
                 w_branch_out[i].astype(BF16), w_mix_out[i].astype(BF16))

        ffn_p = functools.partial(_ffn, per_token=False, tm=FFN_TM,
                                  prompt_row_block=prompt_row_block)
        ffn_s = functools.partial(_ffn, per_token=True, tm=n_s,
                                  prompt_row_block=prompt_row_block)

        yp = ffn_p(yp, ada, 0, gpre[0], gpost[0], *ffn_w[0])
        yp, gla_p, conv_p = _mixer(yp, ada, gpre[1], gpost[1], mix_w,
                                   prompt_row_block=prompt_row_block)
        yp = ffn_p(yp, ada, 2, gpre[2], gpost[2], *ffn_w[1])

        ys = ffn_s(ys, ada, 0, gpre[0], gpost[0], *ffn_w[0])
        sconv_t = jnp.swapaxes(state_conv[i], 0, 1)
        ys2, gla_s, sconv_new = _sample_mixer(ys[0], ada, state_gla[i:i + 1], sconv_t,
                                              gpre[1], gpost[1], mix_w)
        ys = ffn_s(ys2.reshape(1, n_s, D_MODEL), ada, 2, gpre[2], gpost[2], *ffn_w[1])
        outs.append((gla_p, conv_p, gla_s, jnp.swapaxes(sconv_new, 0, 1)[None]))

    gla_p, conv_p, gla_s, conv_s = outs[0]
    return (yp, ys.reshape(n_s, 1, D_MODEL), gla_p, conv_p, gla_s, conv_s)
```

```python
import functools

import jax
import jax.numpy as jnp
from jax import lax
from jax.experimental import pallas as pl
from jax.experimental.pallas import tpu as pltpu

F32 = jnp.float32
BF16 = jnp.bfloat16

D_MODEL = 1024
D_FF = 2816
N_SUBLAYERS = 3
N_HEADS = 4
DK = 128
DV = 256
QK_WIDTH = N_HEADS * DK
V_WIDTH = N_HEADS * DV
GATE_RANK = 16
GATE_TAU = 16.0
D_CONV = D_MODEL
CONV_WIDTH = 3
RMS_EPS = 1e-6
MACARON_WEIGHT = 0.5

LANES = 128
SUBLANES = 8
GLA_CHUNK = 64
VMEM_LIMIT_BYTES = 56 * 1024 * 1024

FFN_TM = 1024
FFN_PART = 256
MIX_TM = 256
ADA_TN = 1536
SAMPLE_BLOCK = 8

_A_WIDTH = 2 * QK_WIDTH + 2 * V_WIDTH
_B_WIDTH = 3 * D_CONV + 2 * D_MODEL


def _rms(x, g):
    return x * lax.rsqrt(jnp.mean(x * x, axis=-1, keepdims=True) + RMS_EPS) * g


def _silu(x):
    return x * jax.nn.sigmoid(x)


def _dot(a, b):
    return jnp.dot(a, b, preferred_element_type=F32)


def _dot_nt(a, b):
    return lax.dot_general(a, b, (((1,), (1,)), ((), ())), preferred_element_type=F32)


def _dot_tn(a, b):
    return lax.dot_general(a, b, (((0,), (0,)), ((), ())), preferred_element_type=F32)


def _resident(shape):
    zeros = (0,) * len(shape)
    return pl.BlockSpec(shape, lambda *_: zeros, pipeline_mode=pl.Buffered(1))


def _params(semantics):
    return pltpu.CompilerParams(dimension_semantics=semantics, vmem_limit_bytes=VMEM_LIMIT_BYTES)


def _ada_kernel(c_ref, w_ref, b_ref, o_ref):
    c = c_ref[...]
    o_ref[...] = _dot(_silu(c).astype(BF16), w_ref[...].astype(BF16)) + b_ref[...]


def _ada(c_all, w_ada, b_ada):
    rows = c_all.shape[0]
    width = w_ada.shape[1]
    return pl.pallas_call(
        _ada_kernel,
        grid=(width // ADA_TN,),
        in_specs=[
            pl.BlockSpec((rows, D_MODEL), lambda j: (0, 0)),
            pl.BlockSpec((D_MODEL, ADA_TN), lambda j: (0, j)),
            pl.BlockSpec((1, ADA_TN), lambda j: (0, j)),
        ],
        out_specs=pl.BlockSpec((rows, ADA_TN), lambda j: (0, j)),
        out_shape=jax.ShapeDtypeStruct((rows, width), F32),
        compiler_params=_params(("arbitrary",)),
        name="ada",
    )(c_all, w_ada, b_ada.reshape(1, width))


def _mod_specs(sub, per_token, prompt_row_block):
    specs = []
    for j in range(3):
        col = sub * 3 + j
        if per_token:
            specs.append(pl.BlockSpec((LANES, D_MODEL), lambda n, l, col=col: (0, col)))
        else:
            specs.append(pl.BlockSpec((SUBLANES, D_MODEL),
                                      lambda n, l, col=col: (prompt_row_block, col)))
    return specs


def _load_mod(ref, per_token):
    if per_token:
        return ref[...]
    return ref[pl.ds(pl.program_id(0), 1), :]


def _ffn_kernel(x_ref, sh_ref, sc_ref, gt_ref, gpre_ref, gpost_ref, win_ref, wout_ref, o_ref,
                *, per_token, weight):
    shift = _load_mod(sh_ref, per_token)
    scale = _load_mod(sc_ref, per_token)
    gate = _load_mod(gt_ref, per_token)
    tm = x_ref.shape[1]
    parts = tm // FFN_PART if tm > FFN_PART else 1
    rows = [slice(i * (tm // parts), (i + 1) * (tm // parts)) for i in range(parts)]
    mod = lambda m, rs: m[rs] if per_token else m

    def pre(rs):
        return (_rms(x_ref[0, rs, :], gpre_ref[...]) * (1.0 + mod(scale, rs))
                + mod(shift, rs)).astype(BF16)

    def post(rs, out):
        o_ref[0, rs, :] = x_ref[0, rs, :] + weight * mod(gate, rs) * _rms(out, gpost_ref[...])

    h = pre(rows[0])
    for i in range(parts):
        g = _dot(h, win_ref[:, :D_FF])
        h_next = pre(rows[i + 1]) if i + 1 < parts else None
        u = _dot(h, win_ref[:, D_FF:])
        a = (_silu(g) * u).astype(BF16)
        if i > 0:
            post(rows[i - 1], out)
        out = _dot(a, wout_ref[...])
        h = h_next
    post(rows[parts - 1], out)


def _ffn(x, ada, sub, g_pre, g_post, w_in, w_out, *, per_token, tm, prompt_row_block):
    n, l, _ = x.shape
    kern = functools.partial(_ffn_kernel, per_token=per_token, weight=MACARON_WEIGHT)
    return pl.pallas_call(
        kern,
        grid=(n, l // tm),
        in_specs=[pl.BlockSpec((1, tm, D_MODEL), lambda n, l: (n, l, 0))]
        + _mod_specs(sub, per_token, prompt_row_block)
        + [_resident((1, D_MODEL)), _resident((1, D_MODEL)),
           _resident((D_MODEL, 2 * D_FF)), _resident((D_FF, D_MODEL))],
        out_specs=pl.BlockSpec((1, tm, D_MODEL), lambda n, l: (n, l, 0)),
        out_shape=jax.ShapeDtypeStruct(x.shape, F32),
        compiler_params=_params(("arbitrary", "arbitrary")),
        name="ffn",
    )(x, ada, ada, ada, g_pre, g_post, w_in, w_out)


def _log_decay(hb, wz_ref, walpha_ref, balpha_ref):
    z = _dot(hb, wz_ref[...]).astype(BF16)
    xg = _dot(z, walpha_ref[...]) + balpha_ref[...]
    return jax.nn.log_sigmoid(xg) * (1.0 / GATE_TAU)


def _merge_and_project(y_gla, y_conv, sig_a, sig_b, wbo_ref, wmo_ref):
    pg = _dot(y_gla, wbo_ref[0])
    pc = _dot(y_conv, wbo_ref[1])
    merged = (sig_a * pg + sig_b * pc).astype(BF16)
    return _dot(merged, wmo_ref[...])


_PAIR_LEVELS = tuple(GLA_CHUNK >> i for i in range(1, GLA_CHUNK.bit_length()))


def _pair_operands(qs, k, b, log_a, m, tm):
    n = tm // (2 * m)
    if m >= SUBLANES:
        b4 = b.reshape(n, 2 * m, QK_WIDTH)
        w = jnp.exp(-jnp.abs(b4 - b4[:, m - 1:m, :]))
        halves = lambda a: a.reshape(n, 2, m, QK_WIDTH)
        src = jnp.concatenate([halves(k)[:, 0:1], halves(qs)[:, 1:2]], axis=1)
        return (src.reshape(n, 2 * m, QK_WIDTH) * w).astype(BF16).reshape(tm, QK_WIDTH)
    row = lax.broadcasted_iota(jnp.int32, (tm, QK_WIDTH), 0)
    second = (row & m) != 0
    if m == 1:
        w = jnp.exp(jnp.where(second, log_a, 0.0))
    else:
        b4 = b.reshape(tm // SUBLANES, SUBLANES, QK_WIDTH)
        sub = lax.broadcasted_iota(jnp.int32, b4.shape, 1)
        mid = b4[:, m - 1:m, :]
        for start in range(2 * m, SUBLANES, 2 * m):
            mid = jnp.where(sub >= start, b4[:, start + m - 1:start + m, :], mid)
        w = jnp.exp(-jnp.abs(b4 - mid)).reshape(tm, QK_WIDTH)
    return (jnp.where(second, qs, k) * w).astype(BF16)


def _pair_mask(m):
    t = lax.broadcasted_iota(jnp.int32, (GLA_CHUNK, GLA_CHUNK), 0)
    s = lax.broadcasted_iota(jnp.int32, (GLA_CHUNK, GLA_CHUNK), 1)
    return ((t & m) != 0) & ((s & m) == 0) & ((t // (2 * m)) == (s // (2 * m)))


def _mixer_kernel(x_ref, sh_ref, sc_ref, gt_ref, gpre_ref, gpost_ref, wa_ref, wz_ref, wb_ref,
                  walpha_ref, balpha_ref, ggla_ref, wconv_ref, wbo_ref, wmo_ref,
                  y_ref, sgla_ref, sconv_ref, st_scr, u_scr, ygla_scr, *, tm):
    l = pl.program_id(1)
    n_l = pl.num_programs(1)

    @pl.when(l == 0)
    def _():
        st_scr[...] = jnp.zeros_like(st_scr)
        u_scr[0:SUBLANES, :] = jnp.zeros((SUBLANES, D_CONV), F32)

    x = x_ref[0]
    shift = _load_mod(sh_ref, False)
    scale = _load_mod(sc_ref, False)
    gate = _load_mod(gt_ref, False)
    hb = (_rms(x, gpre_ref[...]) * (1.0 + scale) + shift).astype(BF16)

    def proj_b(j):
        return _dot(hb, wb_ref[:, j * D_MODEL:(j + 1) * D_MODEL])

    nc = tm // GLA_CHUNK
    log_a = _log_decay(hb, wz_ref, walpha_ref, balpha_ref)
    pa = _dot(hb, wa_ref[...])
    qs = pa[:, 0:QK_WIDTH] * (DK ** -0.5)
    k = pa[:, QK_WIDTH:2 * QK_WIDTH]
    vb = pa[:, 2 * QK_WIDTH:2 * QK_WIDTH + V_WIDTH].astype(BF16)
    r = pa[:, 2 * QK_WIDTH + V_WIDTH:]

    row = lax.broadcasted_iota(jnp.int32, (tm, tm), 0)
    col = lax.broadcasted_iota(jnp.int32, (tm, tm), 1)
    tri = jnp.where((row >= col) & (row // GLA_CHUNK == col // GLA_CHUNK), 1.0, 0.0).astype(BF16)
    la_hi = log_a.astype(BF16)
    rem = log_a - la_hi.astype(F32)
    la_mid = rem.astype(BF16)
    la_lo = (rem - la_mid.astype(F32)).astype(BF16)
    b3 = (_dot(tri, la_hi) + _dot(tri, la_mid) + _dot(tri, la_lo)).reshape(
        nc, GLA_CHUNK, QK_WIDTH)

    u = proj_b(1) * proj_b(2)
    u_scr[SUBLANES:SUBLANES + tm, :] = u

    b_last = b3[:, GLA_CHUNK - 1:GLA_CHUNK, :]
    b = b3.reshape(tm, QK_WIDTH)
    qd = (qs * jnp.exp(b)).astype(BF16).reshape(nc, GLA_CHUNK, QK_WIDTH)
    kl = (k.reshape(nc, GLA_CHUNK, QK_WIDTH) * jnp.exp(b_last - b3)).astype(BF16)
    b_last_rows = jnp.concatenate(
        [b_last[c, :, hd * DK:(hd + 1) * DK] for c in range(nc) for hd in range(N_HEADS)]
        + [jnp.zeros((LANES - nc * N_HEADS, DK), F32)], axis=0)
    decay_cols = jnp.exp(b_last_rows.T)

    pair_ops = [_pair_operands(qs, k, b, log_a, m, tm) for m in _PAIR_LEVELS]
    pair_masks = [_pair_mask(m) for m in _PAIR_LEVELS]
    qs_b, k_b = qs.astype(BF16), k.astype(BF16)
    crow = lax.broadcasted_iota(jnp.int32, (GLA_CHUNK, GLA_CHUNK), 0)
    ccol = lax.broadcasted_iota(jnp.int32, (GLA_CHUNK, GLA_CHUNK), 1)
    scores, upd = {}, {}
    for c in range(nc):
        rows = slice(c * GLA_CHUNK, (c + 1) * GLA_CHUNK)
        for hd in range(N_HEADS):
            kc = slice(hd * DK, (hd + 1) * DK)
            s = jnp.where(crow == ccol, _dot_nt(qs_b[rows, kc], k_b[rows, kc]), 0.0)
            for p, mask in zip(pair_ops, pair_masks):
                s = s + jnp.where(mask, _dot_nt(p[rows, kc], p[rows, kc]), 0.0)
            scores[c, hd] = s.astype(BF16)
            upd[c, hd] = _dot_tn(kl[c, :, kc], vb[rows, hd * DV:(hd + 1) * DV])

    conv = (wconv_ref[0:1, :] * u_scr[SUBLANES - 2:SUBLANES - 2 + tm, :]
            + wconv_ref[1:2, :] * u_scr[SUBLANES - 1:SUBLANES - 1 + tm, :]
            + wconv_ref[2:3, :] * u)
    u_scr[0:SUBLANES, :] = u_scr[tm:tm + SUBLANES, :]
    y_conv = (proj_b(0) * conv).astype(BF16)

    outs = {}
    for hd in range(N_HEADS):
        kc = slice(hd * DK, (hd + 1) * DK)
        vc = slice(hd * DV, (hd + 1) * DV)
        st = st_scr[hd]
        for c in range(nc):
            rows = slice(c * GLA_CHUNK, (c + 1) * GLA_CHUNK)
            lhs = jnp.concatenate([qd[c, :, kc], scores[c, hd]], axis=1)
            rhs = jnp.concatenate([st.astype(BF16), vb[rows, vc]], axis=0)
            outs[c, hd] = _dot(lhs, rhs)
            j = c * N_HEADS + hd
            st = st * decay_cols[:, j:j + 1] + upd[c, hd]
        st_scr[hd] = st

    sig_a = jax.nn.sigmoid(proj_b(3))
    sig_b = jax.nn.sigmoid(proj_b(4))

    for hd in range(N_HEADS):
        vc = slice(hd * DV, (hd + 1) * DV)
        for c in range(nc):
            rows = slice(c * GLA_CHUNK, (c + 1) * GLA_CHUNK)
            on = _rms(outs[c, hd], ggla_ref[:, vc])
            ygla_scr[rows, vc] = (on * _silu(r[rows, vc])).astype(BF16)

    mix = _merge_and_project(ygla_scr[...], y_conv, sig_a, sig_b, wbo_ref, wmo_ref)
    y_ref[0] = x + gate * _rms(mix, gpost_ref[...])

    @pl.when(l == n_l - 1)
    def _():
        sgla_ref[0, 0] = st_scr[...]
        sconv_ref[0, 0] = u_scr[SUBLANES - 2:SUBLANES, :]


def _mixer(x, ada, g_pre, g_post, wts, *, prompt_row_block):
    n, l, _ = x.shape
    tm = MIX_TM
    kern = functools.partial(_mixer_kernel, tm=tm)
    wa, wz, wb, walpha, balpha, ggla, wconv, wbo, wmo = wts
    return pl.pallas_call(
        kern,
        grid=(n, l // tm),
        in_specs=[pl.BlockSpec((1, tm, D_MODEL), lambda n, l: (n, l, 0))]
        + _mod_specs(1, False, prompt_row_block)
        + [_resident(a.shape) for a in (g_pre, g_post, wa, wz, wb, walpha, balpha, ggla, wconv,
                                        wbo, wmo)],
        out_specs=[
            pl.BlockSpec((1, tm, D_MODEL), lambda n, l: (n, l, 0)),
            pl.BlockSpec((1, 1, N_HEADS, DK, DV), lambda n, l: (0, n, 0, 0, 0)),
            pl.BlockSpec((1, 1, CONV_WIDTH - 1, D_CONV), lambda n, l: (0, n, 0, 0)),
        ],
        out_shape=[
            jax.ShapeDtypeStruct(x.shape, F32),
            jax.ShapeDtypeStruct((1, n, N_HEADS, DK, DV), F32),
            jax.ShapeDtypeStruct((1, n, CONV_WIDTH - 1, D_CONV), F32),
        ],
        scratch_shapes=[
            pltpu.VMEM((N_HEADS, DK, DV), F32),
            pltpu.VMEM((tm + SUBLANES, D_CONV), F32),
            pltpu.VMEM((tm, V_WIDTH), BF16),
        ],
        compiler_params=_params(("arbitrary", "arbitrary")),
        name="mixer",
    )(x, ada, ada, ada, g_pre, g_post, wa, wz, wb, walpha, balpha, ggla, wconv, wbo, wmo)


def _smix_pre_kernel(x_ref, sh_ref, sc_ref, gpre_ref, wa_ref, wz_ref, wb_ref, walpha_ref,
                     balpha_ref, wconv_ref, sconv_ref,
                     qt_ref, kt_ref, at_ref, v_ref, sr_ref, yconv_ref, sga_ref, sgb_ref,
                     sconv_new_ref):
    x = x_ref[...]
    hb = (_rms(x, gpre_ref[...]) * (1.0 + sc_ref[...]) + sh_ref[...]).astype(BF16)
    pa = _dot(hb, wa_ref[...])
    q = pa[:, 0:QK_WIDTH] * (DK ** -0.5)
    k = pa[:, QK_WIDTH:2 * QK_WIDTH]
    v_ref[...] = pa[:, 2 * QK_WIDTH:2 * QK_WIDTH + V_WIDTH]
    sr_ref[...] = _silu(pa[:, 2 * QK_WIDTH + V_WIDTH:])
    a = jnp.exp(_log_decay(hb, wz_ref, walpha_ref, balpha_ref))
    for hd in range(N_HEADS):
        kc = slice(hd * DK, (hd + 1) * DK)
        qt_ref[hd] = q[:, kc].T
        kt_ref[hd] = k[:, kc].T
        at_ref[hd] = a[:, kc].T

    pb = _dot(hb, wb_ref[...])
    u = pb[:, D_CONV:2 * D_CONV] * pb[:, 2 * D_CONV:3 * D_CONV]
    conv = (wconv_ref[0:1, :] * sconv_ref[0] + wconv_ref[1:2, :] * sconv_ref[1]
            + wconv_ref[2:3, :] * u)
    yconv_ref[...] = pb[:, 0:D_CONV] * conv
    sga_ref[...] = jax.nn.sigmoid(pb[:, 3 * D_CONV:3 * D_CONV + D_MODEL])
    sgb_ref[...] = jax.nn.sigmoid(pb[:, 3 * D_CONV + D_MODEL:])
    sconv_new_ref[0] = sconv_ref[1]
    sconv_new_ref[1] = u


def _smix_state_kernel(s_ref, qt_ref, kt_ref, at_ref, v_ref, snew_ref, o_ref):
    base = pl.program_id(0) * SAMPLE_BLOCK
    lane = lax.broadcasted_iota(jnp.int32, (1, LANES), 1)
    for i in range(SAMPLE_BLOCK):
        pick = jnp.where(lane == base + i, 1.0, 0.0)
        for hd in range(N_HEADS):
            vc = slice(hd * DV, (hd + 1) * DV)
            a_col = jnp.sum(at_ref[hd] * pick, axis=1, keepdims=True)
            k_col = jnp.sum(kt_ref[hd] * pick, axis=1, keepdims=True)
            q_col = jnp.sum(qt_ref[hd] * pick, axis=1, keepdims=True)
            s_new = a_col * s_ref[0, i, hd] + k_col * v_ref[i:i + 1, vc]
            snew_ref[0, i, hd] = s_new
            o_ref[i:i + 1, vc] = jnp.sum(q_col * s_new, axis=0, keepdims=True)


def _smix_post_kernel(x_ref, gt_ref, gpost_ref, o_ref, sr_ref, yconv_ref, sga_ref, sgb_ref,
                      ggla_ref, wbo_ref, wmo_ref, y_ref, ygla_scr):
    for hd in range(N_HEADS):
        vc = slice(hd * DV, (hd + 1) * DV)
        on = _rms(o_ref[:, vc], ggla_ref[:, vc])
        ygla_scr[:, vc] = (on * sr_ref[:, vc]).astype(BF16)
    mix = _merge_and_project(ygla_scr[...], yconv_ref[...].astype(BF16), sga_ref[...],
                             sgb_ref[...], wbo_ref, wmo_ref)
    y_ref[...] = x_ref[...] + gt_ref[...] * _rms(mix, gpost_ref[...])


def _sample_mixer(x, ada, state_gla, sconv_t, g_pre, g_post, wts):
    s = x.shape[0]
    wa, wz, wb, walpha, balpha, ggla, wconv, wbo, wmo = wts
    whole = lambda a: pl.BlockSpec(a.shape, lambda *_: (0,) * a.ndim)
    mod = lambda j: pl.BlockSpec((s, D_MODEL), lambda *_: (0, 3 + j))
    tok = jax.ShapeDtypeStruct((s, D_MODEL), F32)
    tr = jax.ShapeDtypeStruct((N_HEADS, DK, s), F32)

    pre_in = (x, ada, ada, g_pre, wa, wz, wb, walpha, balpha, wconv, sconv_t)
    pre_specs = [whole(x), mod(0), mod(1)] + [whole(a) for a in pre_in[3:]]
    pre_out = [tr, tr, tr, tok, tok, tok, tok, tok, jax.ShapeDtypeStruct(sconv_t.shape, F32)]
    qt, kt, at, v, sr, yconv, sga, sgb, sconv_new = pl.pallas_call(
        _smix_pre_kernel,
        grid=(1,),
        in_specs=pre_specs,
        out_specs=[whole(o) for o in pre_out],
        out_shape=pre_out,
        compiler_params=_params(("arbitrary",)),
        name="smix_pre",
    )(*pre_in)

    state_block = pl.BlockSpec((1, SAMPLE_BLOCK, N_HEADS, DK, DV), lambda j: (0, j, 0, 0, 0))
    snew, o = pl.pallas_call(
        _smix_state_kernel,
        grid=(s // SAMPLE_BLOCK,),
        in_specs=[state_block, whole(qt), whole(kt), whole(at),
                  pl.BlockSpec((SAMPLE_BLOCK, V_WIDTH), lambda j: (j, 0))],
        out_specs=[state_block, pl.BlockSpec((SAMPLE_BLOCK, V_WIDTH), lambda j: (j, 0))],
        out_shape=[jax.ShapeDtypeStruct(state_gla.shape, F32), tok],
        compiler_params=_params(("arbitrary",)),
        name="smix_state",
    )(state_gla, qt, kt, at, v)

    post_in = (x, ada, g_post, o, sr, yconv, sga, sgb, ggla, wbo, wmo)
    post_specs = [whole(x), mod(2)] + [whole(a) for a in post_in[2:]]
    y = pl.pallas_call(
        _smix_post_kernel,
        grid=(1,),
        in_specs=post_specs,
        out_specs=whole(tok),
        out_shape=tok,
        scratch_shapes=[pltpu.VMEM((s, V_WIDTH), BF16)],
        compiler_params=_params(("arbitrary",)),
        name="smix_post",
    )(*post_in)
    return y, snew, sconv_new


def _split_mix_in(w_mix_in):
    wa = w_mix_in[:, :_A_WIDTH].astype(BF16)
    wz = jnp.pad(w_mix_in[:, _A_WIDTH:_A_WIDTH + GATE_RANK], ((0, 0), (0, LANES - GATE_RANK)))
    wb = w_mix_in[:, _A_WIDTH + GATE_RANK:].astype(BF16)
    return wa, wz.astype(BF16), wb


def kernel(x_prompt, x_sample, state_gla, state_conv, c_prompt, c_sample, w_ada, b_ada, g_pre,
           g_post, w_ffn1_in, w_ffn1_out, w_ffn2_in, w_ffn2_out, w_mix_in, w_alpha, b_alpha,
           g_gla_norm, w_conv, w_branch_out, w_mix_out):
    depth = w_ada.shape[0]
    n_s = x_sample.shape[0]
    assert depth == 1 and x_sample.shape[1] == 1 and n_s == LANES
    assert c_prompt.shape[0] == SUBLANES and n_s % SUBLANES == 0
    prompt_row_block = n_s // SUBLANES

    yp = x_prompt
    ys = x_sample.reshape(1, n_s, D_MODEL)
    c_all = jnp.concatenate([c_sample, c_prompt], axis=0)
    outs = []
    for i in range(depth):
        ada = _ada(c_all, w_ada[i], b_ada[i])
        gpre = [g_pre[i, j].reshape(1, D_MODEL) for j in range(N_SUBLAYERS)]
        gpost = [g_post[i, j].reshape(1, D_MODEL) for j in range(N_SUBLAYERS)]
        ffn_w = [(w_ffn1_in[i].astype(BF16), w_ffn1_out[i].astype(BF16)),
                 (w_ffn2_in[i].astype(BF16), w_ffn2_out[i].astype(BF16))]
        wa, wz, wb = _split_mix_in(w_mix_in[i])
        walpha = jnp.pad(w_alpha[i], ((0, LANES - GATE_RANK), (0, 0))).astype(BF16)
        mix_w = (wa, wz, wb, walpha, b_alpha[i].reshape(1, QK_WIDTH),
                 g_gla_norm[i].reshape(1, V_WIDTH),
                 jnp.pad(w_conv[i], ((0, SUBLANES - CONV_WIDTH), (0, 0))),
                 w_branch_out[i].astype(BF16), w_mix_out[i].astype(BF16))

        ffn_p = functools.partial(_ffn, per_token=False, tm=FFN_TM,
                                  prompt_row_block=prompt_row_block)
        ffn_s = functools.partial(_ffn, per_token=True, tm=n_s,
                                  prompt_row_block=prompt_row_block)

        yp = ffn_p(yp, ada, 0, gpre[0], gpost[0], *ffn_w[0])
        yp, gla_p, conv_p = _mixer(yp, ada, gpre[1], gpost[1], mix_w,
                                   prompt_row_block=prompt_row_block)
        yp = ffn_p(yp, ada, 2, gpre[2], gpost[2], *ffn_w[1])

        ys = ffn_s(ys, ada, 0, gpre[0], gpost[0], *ffn_w[0])
        sconv_t = jnp.swapaxes(state_conv[i], 0, 1)
        ys2, gla_s, sconv_new = _sample_mixer(ys[0], ada, state_gla[i:i + 1], sconv_t,
                                              gpre[1], gpost[1], mix_w)
        ys = ffn_s(ys2.reshape(1, n_s, D_MODEL), ada, 2, gpre[2], gpost[2], *ffn_w[1])
        outs.append((gla_p, conv_p, gla_s, jnp.swapaxes(sconv_new, 0, 1)[None]))

    gla_p, conv_p, gla_s, conv_s = outs[0]
    return (yp, ys.reshape(n_s, 1, D_MODEL), gla_p, conv_p, gla_s, conv_s)
```

```python
import functools
from typing import NamedTuple

import jax
import jax.numpy as jnp
from jax import lax
from jax.experimental import pallas as pl
from jax.experimental.pallas import tpu as pltpu

F32 = jnp.float32
BF16 = jnp.bfloat16

D_MODEL = 1024
D_FF = 2816
N_SUBLAYERS = 3
N_HEADS = 4
DK = 128
DV = 256
QK_WIDTH = N_HEADS * DK
V_WIDTH = N_HEADS * DV
GATE_RANK = 16
GATE_TAU = 16.0
D_CONV = D_MODEL
CONV_WIDTH = 3
RMS_EPS = 1e-6
MACARON_WEIGHT = 0.5

LANES = 128
SUBLANES = 8
GLA_CHUNK = 64
VMEM_LIMIT_BYTES = 56 * 1024 * 1024

FFN_TM = 1024
FFN_PART = 256
MIX_TM = 256
ADA_TN = 1536
SAMPLE_BLOCK = 8

_A_WIDTH = 2 * QK_WIDTH + 2 * V_WIDTH
_B_WIDTH = 3 * D_CONV + 2 * D_MODEL


def _rms(x, g):
    return x * lax.rsqrt(jnp.mean(x * x, axis=-1, keepdims=True) + RMS_EPS) * g


def _silu(x):
    return x * jax.nn.sigmoid(x)


def _dot(a, b):
    return jnp.dot(a, b, preferred_element_type=F32)


def _dot_nt(a, b):
    return lax.dot_general(a, b, (((1,), (1,)), ((), ())), preferred_element_type=F32)


def _dot_tn(a, b):
    return lax.dot_general(a, b, (((0,), (0,)), ((), ())), preferred_element_type=F32)


def _resident(shape):
    zeros = (0,) * len(shape)
    return pl.BlockSpec(shape, lambda *_: zeros, pipeline_mode=pl.Buffered(1))


def _params(semantics):
    return pltpu.CompilerParams(dimension_semantics=semantics, vmem_limit_bytes=VMEM_LIMIT_BYTES)


def _ada_kernel(c_ref, w_ref, b_ref, o_ref):
    c = c_ref[...]
    o_ref[...] = _dot(_silu(c).astype(BF16), w_ref[...].astype(BF16)) + b_ref[...]


def _ada(c_all, w_ada, b_ada):
    rows = c_all.shape[0]
    width = w_ada.shape[1]
    return pl.pallas_call(
        _ada_kernel,
        grid=(width // ADA_TN,),
        in_specs=[
            pl.BlockSpec((rows, D_MODEL), lambda j: (0, 0)),
            pl.BlockSpec((D_MODEL, ADA_TN), lambda j: (0, j)),
            pl.BlockSpec((1, ADA_TN), lambda j: (0, j)),
        ],
        out_specs=pl.BlockSpec((rows, ADA_TN), lambda j: (0, j)),
        out_shape=jax.ShapeDtypeStruct((rows, width), F32),
        compiler_params=_params(("arbitrary",)),
        name="ada",
    )(c_all, w_ada, b_ada.reshape(1, width))


def _mod_specs(sub, per_token, prompt_row_block):
    specs = []
    for j in range(3):
        col = sub * 3 + j
        if per_token:
            specs.append(pl.BlockSpec((LANES, D_MODEL), lambda n, l, col=col: (0, col)))
        else:
            specs.append(pl.BlockSpec((SUBLANES, D_MODEL),
                                      lambda n, l, col=col: (prompt_row_block, col)))
    return specs


def _load_mod(ref, per_token):
    if per_token:
        return ref[...]
    return ref[pl.ds(pl.program_id(0), 1), :]


BF16_SUBLANES = 16


class _CastJob(NamedTuple):
    src: jax.Array
    groups: tuple

    def block_rows(self, n_steps):
        rows = self.src.shape[0]
        return next(r for r in range(BF16_SUBLANES, rows + 1, BF16_SUBLANES)
                    if rows % r == 0 and rows // r <= n_steps)

    def specs(self, n_steps, step_of):
        rows, cols = self.src.shape
        r = self.block_rows(n_steps)
        index = lambda *g: (jnp.minimum(step_of(*g), rows // r - 1), 0)
        return (pl.BlockSpec((r, cols), index),
                [pl.BlockSpec((r, w), index) for _, _, w in self.groups],
                [jax.ShapeDtypeStruct((rows, w), BF16) for _, _, w in self.groups])


def _whole_cast(w):
    return _CastJob(w, ((0, w.shape[1], w.shape[1]),))


def _run_cast(src_ref, dst_refs, groups):
    for dst, (start, width, out_width) in zip(dst_refs, groups):
        blk = src_ref[:, start:start + width]
        if out_width > width:
            blk = jnp.concatenate([blk, jnp.zeros((blk.shape[0], out_width - width), F32)], axis=1)
        dst[...] = blk.astype(BF16)


def _cast_plumbing(jobs, n_steps, step_of):
    in_specs, out_specs, out_shapes = [], [], []
    for job in jobs:
        i, o, s = job.specs(n_steps, step_of)
        in_specs.append(i)
        out_specs += o
        out_shapes += s
    return in_specs, out_specs, out_shapes


def _run_casts(job_groups, src_refs, dst_refs):
    dst_refs = list(dst_refs)
    for groups, src in zip(job_groups, src_refs):
        _run_cast(src, dst_refs[:len(groups)], groups)
        dst_refs = dst_refs[len(groups):]


def _ffn_kernel(x_ref, sh_ref, sc_ref, gt_ref, gpre_ref, gpost_ref, win_ref, wout_ref, *rest,
                per_token, weight, job_groups):
    n_jobs = len(job_groups)
    o_ref = rest[n_jobs]
    _run_casts(job_groups, rest[:n_jobs], rest[n_jobs + 1:])
    shift = _load_mod(sh_ref, per_token)
    scale = _load_mod(sc_ref, per_token)
    gate = _load_mod(gt_ref, per_token)
    tm = x_ref.shape[1]
    parts = tm // FFN_PART if tm > FFN_PART else 1
    rows = [slice(i * (tm // parts), (i + 1) * (tm // parts)) for i in range(parts)]
    mod = lambda m, rs: m[rs] if per_token else m

    def pre(rs):
        return (_rms(x_ref[0, rs, :], gpre_ref[...]) * (1.0 + mod(scale, rs))
                + mod(shift, rs)).astype(BF16)

    def post(rs, out):
        o_ref[0, rs, :] = x_ref[0, rs, :] + weight * mod(gate, rs) * _rms(out, gpost_ref[...])

    h = pre(rows[0])
    for i in range(parts):
        g = _dot(h, win_ref[:, :D_FF])
        h_next = pre(rows[i + 1]) if i + 1 < parts else None
        u = _dot(h, win_ref[:, D_FF:])
        a = (_silu(g) * u).astype(BF16)
        if i > 0:
            post(rows[i - 1], out)
        out = _dot(a, wout_ref[...])
        h = h_next
    post(rows[parts - 1], out)


def _ffn(x, ada, sub, g_pre, g_post, w_in, w_out, *, per_token, tm, prompt_row_block, jobs=()):
    n, l, _ = x.shape
    n_l = l // tm
    kern = functools.partial(_ffn_kernel, per_token=per_token, weight=MACARON_WEIGHT,
                             job_groups=tuple(j.groups for j in jobs))
    cast_in, cast_out, cast_shapes = _cast_plumbing(jobs, n * n_l, lambda n, l: n * n_l + l)
    return pl.pallas_call(
        kern,
        grid=(n, n_l),
        in_specs=[pl.BlockSpec((1, tm, D_MODEL), lambda n, l: (n, l, 0))]
        + _mod_specs(sub, per_token, prompt_row_block)
        + [_resident((1, D_MODEL)), _resident((1, D_MODEL)),
           _resident((D_MODEL, 2 * D_FF)), _resident((D_FF, D_MODEL))]
        + cast_in,
        out_specs=[pl.BlockSpec((1, tm, D_MODEL), lambda n, l: (n, l, 0))] + cast_out,
        out_shape=[jax.ShapeDtypeStruct(x.shape, F32)] + cast_shapes,
        compiler_params=_params(("arbitrary", "arbitrary")),
        name="ffn",
    )(x, ada, ada, ada, g_pre, g_post, w_in, w_out, *[j.src for j in jobs])


def _log_decay(hb, wz_ref, walpha_ref, balpha_ref):
    z = _dot(hb, wz_ref[...]).astype(BF16)
    xg = _dot(z, walpha_ref[...]) + balpha_ref[...]
    return jax.nn.log_sigmoid(xg) * (1.0 / GATE_TAU)


def _merge_and_project(y_gla, y_conv, sig_a, sig_b, wbo_ref, wmo_ref):
    pg = _dot(y_gla, wbo_ref[0])
    pc = _dot(y_conv, wbo_ref[1])
    merged = (sig_a * pg + sig_b * pc).astype(BF16)
    return _dot(merged, wmo_ref[...])


_PAIR_LEVELS = tuple(GLA_CHUNK >> i for i in range(1, GLA_CHUNK.bit_length()))


def _pair_operands(qs, k, b, log_a, m, tm):
    n = tm // (2 * m)
    if m >= SUBLANES:
        b4 = b.reshape(n, 2 * m, QK_WIDTH)
        w = jnp.exp(-jnp.abs(b4 - b4[:, m - 1:m, :]))
        halves = lambda a: a.reshape(n, 2, m, QK_WIDTH)
        src = jnp.concatenate([halves(k)[:, 0:1], halves(qs)[:, 1:2]], axis=1)
        return (src.reshape(n, 2 * m, QK_WIDTH) * w).astype(BF16).reshape(tm, QK_WIDTH)
    row = lax.broadcasted_iota(jnp.int32, (tm, QK_WIDTH), 0)
    second = (row & m) != 0
    if m == 1:
        w = jnp.exp(jnp.where(second, log_a, 0.0))
    else:
        b4 = b.reshape(tm // SUBLANES, SUBLANES, QK_WIDTH)
        sub = lax.broadcasted_iota(jnp.int32, b4.shape, 1)
        mid = b4[:, m - 1:m, :]
        for start in range(2 * m, SUBLANES, 2 * m):
            mid = jnp.where(sub >= start, b4[:, start + m - 1:start + m, :], mid)
        w = jnp.exp(-jnp.abs(b4 - mid)).reshape(tm, QK_WIDTH)
    return (jnp.where(second, qs, k) * w).astype(BF16)


def _pair_mask(m):
    t = lax.broadcasted_iota(jnp.int32, (GLA_CHUNK, GLA_CHUNK), 0)
    s = lax.broadcasted_iota(jnp.int32, (GLA_CHUNK, GLA_CHUNK), 1)
    return ((t & m) != 0) & ((s & m) == 0) & ((t // (2 * m)) == (s // (2 * m)))


def _mixer_kernel(x_ref, sh_ref, sc_ref, gt_ref, gpre_ref, gpost_ref, wa_ref, wz_ref, wb_ref,
                  walpha_ref, balpha_ref, ggla_ref, wconv_ref, wbo_ref, wmo_ref, *rest,
                  tm, job_groups):
    n_jobs = len(job_groups)
    y_ref, sgla_ref, sconv_ref = rest[n_jobs:n_jobs + 3]
    st_scr, u_scr, ygla_scr = rest[-3:]
    _run_casts(job_groups, rest[:n_jobs], rest[n_jobs + 3:-3])
    l = pl.program_id(1)
    n_l = pl.num_programs(1)

    @pl.when(l == 0)
    def _():
        st_scr[...] = jnp.zeros_like(st_scr)
        u_scr[0:SUBLANES, :] = jnp.zeros((SUBLANES, D_CONV), F32)

    x = x_ref[0]
    shift = _load_mod(sh_ref, False)
    scale = _load_mod(sc_ref, False)
    gate = _load_mod(gt_ref, False)
    hb = (_rms(x, gpre_ref[...]) * (1.0 + scale) + shift).astype(BF16)

    def proj_b(j):
        return _dot(hb, wb_ref[:, j * D_MODEL:(j + 1) * D_MODEL])

    nc = tm // GLA_CHUNK
    log_a = _log_decay(hb, wz_ref, walpha_ref, balpha_ref)
    pa = _dot(hb, wa_ref[...])
    qs = pa[:, 0:QK_WIDTH] * (DK ** -0.5)
    k = pa[:, QK_WIDTH:2 * QK_WIDTH]
    vb = pa[:, 2 * QK_WIDTH:2 * QK_WIDTH + V_WIDTH].astype(BF16)
    r = pa[:, 2 * QK_WIDTH + V_WIDTH:]

    row = lax.broadcasted_iota(jnp.int32, (tm, tm), 0)
    col = lax.broadcasted_iota(jnp.int32, (tm, tm), 1)
    tri = jnp.where((row >= col) & (row // GLA_CHUNK == col // GLA_CHUNK), 1.0, 0.0).astype(BF16)
    la_hi = log_a.astype(BF16)
    rem = log_a - la_hi.astype(F32)
    la_mid = rem.astype(BF16)
    la_lo = (rem - la_mid.astype(F32)).astype(BF16)
    b3 = (_dot(tri, la_hi) + _dot(tri, la_mid) + _dot(tri, la_lo)).reshape(
        nc, GLA_CHUNK, QK_WIDTH)

    u = proj_b(1) * proj_b(2)
    u_scr[SUBLANES:SUBLANES + tm, :] = u

    b_last = b3[:, GLA_CHUNK - 1:GLA_CHUNK, :]
    b = b3.reshape(tm, QK_WIDTH)
    qd = (qs * jnp.exp(b)).astype(BF16).reshape(nc, GLA_CHUNK, QK_WIDTH)
    kl = (k.reshape(nc, GLA_CHUNK, QK_WIDTH) * jnp.exp(b_last - b3)).astype(BF16)
    b_last_rows = jnp.concatenate(
        [b_last[c, :, hd * DK:(hd + 1) * DK] for c in range(nc) for hd in range(N_HEADS)]
        + [jnp.zeros((LANES - nc * N_HEADS, DK), F32)], axis=0)
    decay_cols = jnp.exp(b_last_rows.T)

    pair_ops = [_pair_operands(qs, k, b, log_a, m, tm) for m in _PAIR_LEVELS]
    pair_masks = [_pair_mask(m) for m in _PAIR_LEVELS]
    qs_b, k_b = qs.astype(BF16), k.astype(BF16)
    crow = lax.broadcasted_iota(jnp.int32, (GLA_CHUNK, GLA_CHUNK), 0)
    ccol = lax.broadcasted_iota(jnp.int32, (GLA_CHUNK, GLA_CHUNK), 1)
    scores, upd = {}, {}
    for c in range(nc):
        rows = slice(c * GLA_CHUNK, (c + 1) * GLA_CHUNK)
        for hd in range(N_HEADS):
            kc = slice(hd * DK, (hd + 1) * DK)
            s = jnp.where(crow == ccol, _dot_nt(qs_b[rows, kc], k_b[rows, kc]), 0.0)
            for p, mask in zip(pair_ops, pair_masks):
                s = s + jnp.where(mask, _dot_nt(p[rows, kc], p[rows, kc]), 0.0)
            scores[c, hd] = s.astype(BF16)
            upd[c, hd] = _dot_tn(kl[c, :, kc], vb[rows, hd * DV:(hd + 1) * DV])

    conv = (wconv_ref[0:1, :] * u_scr[SUBLANES - 2:SUBLANES - 2 + tm, :]
            + wconv_ref[1:2, :] * u_scr[SUBLANES - 1:SUBLANES - 1 + tm, :]
            + wconv_ref[2:3, :] * u)
    u_scr[0:SUBLANES, :] = u_scr[tm:tm + SUBLANES, :]
    y_conv = (proj_b(0) * conv).astype(BF16)

    outs = {}
    for hd in range(N_HEADS):
        kc = slice(hd * DK, (hd + 1) * DK)
        vc = slice(hd * DV, (hd + 1) * DV)
        st = st_scr[hd]
        for c in range(nc):
            rows = slice(c * GLA_CHUNK, (c + 1) * GLA_CHUNK)
            lhs = jnp.concatenate([qd[c, :, kc], scores[c, hd]], axis=1)
            rhs = jnp.concatenate([st.astype(BF16), vb[rows, vc]], axis=0)
            outs[c, hd] = _dot(lhs, rhs)
            j = c * N_HEADS + hd
            st = st * decay_cols[:, j:j + 1] + upd[c, hd]
        st_scr[hd] = st

    sig_a = jax.nn.sigmoid(proj_b(3))
    sig_b = jax.nn.sigmoid(proj_b(4))

    for hd in range(N_HEADS):
        vc = slice(hd * DV, (hd + 1) * DV)
        for c in range(nc):
            rows = slice(c * GLA_CHUNK, (c + 1) * GLA_CHUNK)
            on = _rms(outs[c, hd], ggla_ref[:, vc])
            ygla_scr[rows, vc] = (on * _silu(r[rows, vc])).astype(BF16)

    mix = _merge_and_project(ygla_scr[...], y_conv, sig_a, sig_b, wbo_ref, wmo_ref)
    y_ref[0] = x + gate * _rms(mix, gpost_ref[...])

    @pl.when(l == n_l - 1)
    def _():
        sgla_ref[0, 0] = st_scr[...]
        sconv_ref[0, 0] = u_scr[SUBLANES - 2:SUBLANES, :]


def _mixer(x, ada, g_pre, g_post, wts, *, prompt_row_block, jobs=()):
    n, l, _ = x.shape
    tm = MIX_TM
    n_l = l // tm
    kern = functools.partial(_mixer_kernel, tm=tm, job_groups=tuple(j.groups for j in jobs))
    cast_in, cast_out, cast_shapes = _cast_plumbing(jobs, n * n_l, lambda n, l: n * n_l + l)
    wa, wz, wb, walpha, balpha, ggla, wconv, wbo, wmo = wts
    return pl.pallas_call(
        kern,
        grid=(n, n_l),
        in_specs=[pl.BlockSpec((1, tm, D_MODEL), lambda n, l: (n, l, 0))]
        + _mod_specs(1, False, prompt_row_block)
        + [_resident(a.shape) for a in (g_pre, g_post, wa, wz, wb, walpha, balpha, ggla, wconv,
                                        wbo, wmo)]
        + cast_in,
        out_specs=[
            pl.BlockSpec((1, tm, D_MODEL), lambda n, l: (n, l, 0)),
            pl.BlockSpec((1, 1, N_HEADS, DK, DV), lambda n, l: (0, n, 0, 0, 0)),
            pl.BlockSpec((1, 1, CONV_WIDTH - 1, D_CONV), lambda n, l: (0, n, 0, 0)),
        ] + cast_out,
        out_shape=[
            jax.ShapeDtypeStruct(x.shape, F32),
            jax.ShapeDtypeStruct((1, n, N_HEADS, DK, DV), F32),
            jax.ShapeDtypeStruct((1, n, CONV_WIDTH - 1, D_CONV), F32),
        ] + cast_shapes,
        scratch_shapes=[
            pltpu.VMEM((N_HEADS, DK, DV), F32),
            pltpu.VMEM((tm + SUBLANES, D_CONV), F32),
            pltpu.VMEM((tm, V_WIDTH), BF16),
        ],
        compiler_params=_params(("arbitrary", "arbitrary")),
        name="mixer",
    )(x, ada, ada, ada, g_pre, g_post, wa, wz, wb, walpha, balpha, ggla, wconv, wbo, wmo,
      *[j.src for j in jobs])


def _smix_pre_kernel(x_ref, sh_ref, sc_ref, gpre_ref, wa_ref, wz_ref, wb_ref, walpha_ref,
                     balpha_ref, wconv_ref, sconv_ref,
                     qt_ref, kt_ref, at_ref, v_ref, sr_ref, yconv_ref, sga_ref, sgb_ref,
                     sconv_new_ref):
    x = x_ref[...]
    hb = (_rms(x, gpre_ref[...]) * (1.0 + sc_ref[...]) + sh_ref[...]).astype(BF16)
    pa = _dot(hb, wa_ref[...])
    q = pa[:, 0:QK_WIDTH] * (DK ** -0.5)
    k = pa[:, QK_WIDTH:2 * QK_WIDTH]
    v_ref[...] = pa[:, 2 * QK_WIDTH:2 * QK_WIDTH + V_WIDTH]
    sr_ref[...] = _silu(pa[:, 2 * QK_WIDTH + V_WIDTH:])
    a = jnp.exp(_log_decay(hb, wz_ref, walpha_ref, balpha_ref))
    for hd in range(N_HEADS):
        kc = slice(hd * DK, (hd + 1) * DK)
        qt_ref[hd] = q[:, kc].T
        kt_ref[hd] = k[:, kc].T
        at_ref[hd] = a[:, kc].T

    pb = _dot(hb, wb_ref[...])
    u = pb[:, D_CONV:2 * D_CONV] * pb[:, 2 * D_CONV:3 * D_CONV]
    conv = (wconv_ref[0:1, :] * sconv_ref[0] + wconv_ref[1:2, :] * sconv_ref[1]
            + wconv_ref[2:3, :] * u)
    yconv_ref[...] = pb[:, 0:D_CONV] * conv
    sga_ref[...] = jax.nn.sigmoid(pb[:, 3 * D_CONV:3 * D_CONV + D_MODEL])
    sgb_ref[...] = jax.nn.sigmoid(pb[:, 3 * D_CONV + D_MODEL:])
    sconv_new_ref[0] = sconv_ref[1]
    sconv_new_ref[1] = u


def _smix_state_kernel(s_ref, qt_ref, kt_ref, at_ref, v_ref, snew_ref, o_ref):
    base = pl.program_id(0) * SAMPLE_BLOCK
    lane = lax.broadcasted_iota(jnp.int32, (1, LANES), 1)
    for i in range(SAMPLE_BLOCK):
        pick = jnp.where(lane == base + i, 1.0, 0.0)
        for hd in range(N_HEADS):
            vc = slice(hd * DV, (hd + 1) * DV)
            a_col = jnp.sum(at_ref[hd] * pick, axis=1, keepdims=True)
            k_col = jnp.sum(kt_ref[hd] * pick, axis=1, keepdims=True)
            q_col = jnp.sum(qt_ref[hd] * pick, axis=1, keepdims=True)
            s_new = a_col * s_ref[0, i, hd] + k_col * v_ref[i:i + 1, vc]
            snew_ref[0, i, hd] = s_new
            o_ref[i:i + 1, vc] = jnp.sum(q_col * s_new, axis=0, keepdims=True)


def _smix_post_kernel(x_ref, gt_ref, gpost_ref, o_ref, sr_ref, yconv_ref, sga_ref, sgb_ref,
                      ggla_ref, wbo_ref, wmo_ref, y_ref, ygla_scr):
    for hd in range(N_HEADS):
        vc = slice(hd * DV, (hd + 1) * DV)
        on = _rms(o_ref[:, vc], ggla_ref[:, vc])
        ygla_scr[:, vc] = (on * sr_ref[:, vc]).astype(BF16)
    mix = _merge_and_project(ygla_scr[...], yconv_ref[...].astype(BF16), sga_ref[...],
                             sgb_ref[...], wbo_ref, wmo_ref)
    y_ref[...] = x_ref[...] + gt_ref[...] * _rms(mix, gpost_ref[...])


def _sample_mixer(x, ada, state_gla, sconv_t, g_pre, g_post, wts):
    s = x.shape[0]
    wa, wz, wb, walpha, balpha, ggla, wconv, wbo, wmo = wts
    whole = lambda a: pl.BlockSpec(a.shape, lambda *_: (0,) * a.ndim)
    mod = lambda j: pl.BlockSpec((s, D_MODEL), lambda *_: (0, 3 + j))
    tok = jax.ShapeDtypeStruct((s, D_MODEL), F32)
    tr = jax.ShapeDtypeStruct((N_HEADS, DK, s), F32)

    pre_in = (x, ada, ada, g_pre, wa, wz, wb, walpha, balpha, wconv, sconv_t)
    pre_specs = [whole(x), mod(0), mod(1)] + [whole(a) for a in pre_in[3:]]
    pre_out = [tr, tr, tr, tok, tok, tok, tok, tok, jax.ShapeDtypeStruct(sconv_t.shape, F32)]
    qt, kt, at, v, sr, yconv, sga, sgb, sconv_new = pl.pallas_call(
        _smix_pre_kernel,
        grid=(1,),
        in_specs=pre_specs,
        out_specs=[whole(o) for o in pre_out],
        out_shape=pre_out,
        compiler_params=_params(("arbitrary",)),
        name="smix_pre",
    )(*pre_in)

    state_block = pl.BlockSpec((1, SAMPLE_BLOCK, N_HEADS, DK, DV), lambda j: (0, j, 0, 0, 0))
    snew, o = pl.pallas_call(
        _smix_state_kernel,
        grid=(s // SAMPLE_BLOCK,),
        in_specs=[state_block, whole(qt), whole(kt), whole(at),
                  pl.BlockSpec((SAMPLE_BLOCK, V_WIDTH), lambda j: (j, 0))],
        out_specs=[state_block, pl.BlockSpec((SAMPLE_BLOCK, V_WIDTH), lambda j: (j, 0))],
        out_shape=[jax.ShapeDtypeStruct(state_gla.shape, F32), tok],
        compiler_params=_params(("arbitrary",)),
        name="smix_state",
    )(state_gla, qt, kt, at, v)

    post_in = (x, ada, g_post, o, sr, yconv, sga, sgb, ggla, wbo, wmo)
    post_specs = [whole(x), mod(2)] + [whole(a) for a in post_in[2:]]
    y = pl.pallas_call(
        _smix_post_kernel,
        grid=(1,),
        in_specs=post_specs,
        out_specs=whole(tok),
        out_shape=tok,
        scratch_shapes=[pltpu.VMEM((s, V_WIDTH), BF16)],
        compiler_params=_params(("arbitrary",)),
        name="smix_post",
    )(*post_in)
    return y, snew, sconv_new


def kernel(x_prompt, x_sample, state_gla, state_conv, c_prompt, c_sample, w_ada, b_ada, g_pre,
           g_post, w_ffn1_in, w_ffn1_out, w_ffn2_in, w_ffn2_out, w_mix_in, w_alpha, b_alpha,
           g_gla_norm, w_conv, w_branch_out, w_mix_out):
    depth = w_ada.shape[0]
    n_s = x_sample.shape[0]
    assert depth == 1 and x_sample.shape[1] == 1 and n_s == LANES
    assert c_prompt.shape[0] == SUBLANES and n_s % SUBLANES == 0
    prompt_row_block = n_s // SUBLANES

    yp = x_prompt
    ys = x_sample.reshape(1, n_s, D_MODEL)
    c_all = jnp.concatenate([c_sample, c_prompt], axis=0)
    outs = []
    for i in range(depth):
        ada = _ada(c_all, w_ada[i], b_ada[i])
        gpre = [g_pre[i, j].reshape(1, D_MODEL) for j in range(N_SUBLAYERS)]
        gpost = [g_post[i, j].reshape(1, D_MODEL) for j in range(N_SUBLAYERS)]
        ffn_p = functools.partial(_ffn, per_token=False, tm=FFN_TM,
                                  prompt_row_block=prompt_row_block)
        ffn_s = functools.partial(_ffn, per_token=True, tm=n_s,
                                  prompt_row_block=prompt_row_block)

        ffn1_w = (w_ffn1_in[i].astype(BF16), w_ffn1_out[i].astype(BF16))
        mixer_jobs = (
            _CastJob(w_mix_in[i], ((0, _A_WIDTH, _A_WIDTH), (_A_WIDTH, GATE_RANK, LANES),
                                   (_A_WIDTH + GATE_RANK, _B_WIDTH, _B_WIDTH))),
            _whole_cast(w_branch_out[i].reshape(2 * D_MODEL, D_MODEL)),
            _whole_cast(w_mix_out[i]))
        yp, wa, wz, wb, wbo, wmo = ffn_p(yp, ada, 0, gpre[0], gpost[0], *ffn1_w, jobs=mixer_jobs)
        walpha = jnp.pad(w_alpha[i], ((0, LANES - GATE_RANK), (0, 0))).astype(BF16)
        mix_w = (wa, wz, wb, walpha, b_alpha[i].reshape(1, QK_WIDTH),
                 g_gla_norm[i].reshape(1, V_WIDTH),
                 jnp.pad(w_conv[i], ((0, SUBLANES - CONV_WIDTH), (0, 0))),
                 wbo.reshape(2, D_MODEL, D_MODEL), wmo)
        yp, gla_p, conv_p, *ffn2_w = _mixer(
            yp, ada, gpre[1], gpost[1], mix_w, prompt_row_block=prompt_row_block,
            jobs=(_whole_cast(w_ffn2_in[i]), _whole_cast(w_ffn2_out[i])))
        yp, = ffn_p(yp, ada, 2, gpre[2], gpost[2], *ffn2_w)

        ys, = ffn_s(ys, ada, 0, gpre[0], gpost[0], *ffn1_w)
        sconv_t = jnp.swapaxes(state_conv[i], 0, 1)
        ys2, gla_s, sconv_new = _sample_mixer(ys[0], ada, state_gla[i:i + 1], sconv_t,
                                              gpre[1], gpost[1], mix_w)
        ys, = ffn_s(ys2.reshape(1, n_s, D_MODEL), ada, 2, gpre[2], gpost[2], *ffn2_w)
        outs.append((gla_p, conv_p, gla_s, jnp.swapaxes(sconv_new, 0, 1)[None]))

    gla_p, conv_p, gla_s, conv_s = outs[0]
    return (yp, ys.reshape(n_s, 1, D_MODEL), gla_p, conv_p, gla_s, conv_s)
```

```python
import functools
from typing import NamedTuple

import jax
import jax.numpy as jnp
from jax import lax
from jax.experimental import pallas as pl
from jax.experimental.pallas import tpu as pltpu

F32 = jnp.float32
BF16 = jnp.bfloat16

D_MODEL = 1024
D_FF = 2816
N_SUBLAYERS = 3
N_HEADS = 4
DK = 128
DV = 256
QK_WIDTH = N_HEADS * DK
V_WIDTH = N_HEADS * DV
GATE_RANK = 16
GATE_TAU = 16.0
D_CONV = D_MODEL
CONV_WIDTH = 3
RMS_EPS = 1e-6
MACARON_WEIGHT = 0.5

LANES = 128
SUBLANES = 8
GLA_CHUNK = 64
VMEM_LIMIT_BYTES = 56 * 1024 * 1024

FFN_TM = 1024
FFN_PART = 256
MIX_TM = 256
ADA_TN = 1536
SAMPLE_BLOCK = 8

_A_WIDTH = 2 * QK_WIDTH + 2 * V_WIDTH
_B_WIDTH = 3 * D_CONV + 2 * D_MODEL


def _rms(x, g):
    return x * lax.rsqrt(jnp.mean(x * x, axis=-1, keepdims=True) + RMS_EPS) * g


def _silu(x):
    return x * jax.nn.sigmoid(x)


def _dot(a, b):
    return jnp.dot(a, b, preferred_element_type=F32)


def _dot_nt(a, b):
    return lax.dot_general(a, b, (((1,), (1,)), ((), ())), preferred_element_type=F32)


def _dot_tn(a, b):
    return lax.dot_general(a, b, (((0,), (0,)), ((), ())), preferred_element_type=F32)


def _resident(shape):
    zeros = (0,) * len(shape)
    return pl.BlockSpec(shape, lambda *_: zeros, pipeline_mode=pl.Buffered(1))


def _params(semantics):
    return pltpu.CompilerParams(dimension_semantics=semantics, vmem_limit_bytes=VMEM_LIMIT_BYTES)


def _ada_kernel(c_ref, w_ref, b_ref, o_ref):
    c = c_ref[...]
    o_ref[...] = _dot(_silu(c).astype(BF16), w_ref[...].astype(BF16)) + b_ref[...]


def _ada(c_all, w_ada, b_ada):
    rows = c_all.shape[0]
    width = w_ada.shape[1]
    return pl.pallas_call(
        _ada_kernel,
        grid=(width // ADA_TN,),
        in_specs=[
            pl.BlockSpec((rows, D_MODEL), lambda j: (0, 0)),
            pl.BlockSpec((D_MODEL, ADA_TN), lambda j: (0, j)),
            pl.BlockSpec((1, ADA_TN), lambda j: (0, j)),
        ],
        out_specs=pl.BlockSpec((rows, ADA_TN), lambda j: (0, j)),
        out_shape=jax.ShapeDtypeStruct((rows, width), F32),
        compiler_params=_params(("arbitrary",)),
        name="ada",
    )(c_all, w_ada, b_ada.reshape(1, width))


def _mod_specs(sub, per_token, prompt_row_block):
    specs = []
    for j in range(3):
        col = sub * 3 + j
        if per_token:
            specs.append(pl.BlockSpec((LANES, D_MODEL), lambda n, l, col=col: (0, col)))
        else:
            specs.append(pl.BlockSpec((SUBLANES, D_MODEL),
                                      lambda n, l, col=col: (prompt_row_block, col)))
    return specs


def _load_mod(ref, per_token):
    if per_token:
        return ref[...]
    return ref[pl.ds(pl.program_id(0), 1), :]


BF16_SUBLANES = 16


class _CastJob(NamedTuple):
    src: jax.Array
    block_rows: int
    n_blocks: int
    transpose: bool = False
    row0: int = 0
    valid: int = 0

    def specs(self, step_of):
        cols = self.src.shape[1]
        r, nb = self.block_rows, self.n_blocks
        block = lambda *g: jnp.minimum(step_of(*g), nb - 1)
        if not self.transpose:
            spec = pl.BlockSpec((r, cols), lambda *g: (block(*g), 0))
            return spec, spec, jax.ShapeDtypeStruct((nb * r, cols), BF16)
        assert self.row0 % SUBLANES == 0 and r % SUBLANES == 0
        return (pl.BlockSpec((pl.Element(r), pl.Element(cols)),
                             lambda *g: (pl.multiple_of(self.row0 + r * block(*g), SUBLANES), 0)),
                pl.BlockSpec((cols, r), lambda *g: (0, block(*g))),
                jax.ShapeDtypeStruct((cols, nb * r), BF16))

    @property
    def static(self):
        return (self.transpose, self.valid or self.block_rows)


def _whole_cast(w, n_steps):
    rows = w.shape[0]
    r = next(r for r in range(BF16_SUBLANES, rows + 1, BF16_SUBLANES)
             if rows % r == 0 and rows // r <= n_steps)
    return _CastJob(w, r, rows // r)


def _cast_plumbing(jobs, step_of):
    specs = [job.specs(step_of) for job in jobs]
    return ([i for i, _, _ in specs], [o for _, o, _ in specs], [s for _, _, s in specs])


def _run_casts(statics, src_refs, dst_refs):
    for (transpose, valid), src, dst in zip(statics, src_refs, dst_refs, strict=True):
        blk = src[...]
        if transpose:
            blk = blk.T
            if valid < blk.shape[1]:
                lane = lax.broadcasted_iota(jnp.int32, blk.shape, 1)
                blk = jnp.where(lane < valid, blk, 0.0)
        dst[...] = blk.astype(BF16)


def _ffn_kernel(x_ref, sh_ref, sc_ref, gt_ref, gpre_ref, gpost_ref, win_ref, wout_ref, *rest,
                per_token, weight, job_groups):
    n_jobs = len(job_groups)
    o_ref = rest[n_jobs]
    _run_casts(job_groups, rest[:n_jobs], rest[n_jobs + 1:])
    shift = _load_mod(sh_ref, per_token)
    scale = _load_mod(sc_ref, per_token)
    gate = _load_mod(gt_ref, per_token)
    tm = x_ref.shape[1]
    parts = tm // FFN_PART if tm > FFN_PART else 1
    rows = [slice(i * (tm // parts), (i + 1) * (tm // parts)) for i in range(parts)]
    mod = lambda m, rs: m[rs] if per_token else m

    def pre(rs):
        return (_rms(x_ref[0, rs, :], gpre_ref[...]) * (1.0 + mod(scale, rs))
                + mod(shift, rs)).astype(BF16)

    def post(rs, out):
        o_ref[0, rs, :] = x_ref[0, rs, :] + weight * mod(gate, rs) * _rms(out, gpost_ref[...])

    h = pre(rows[0])
    for i in range(parts):
        g = _dot(h, win_ref[:, :D_FF])
        h_next = pre(rows[i + 1]) if i + 1 < parts else None
        u = _dot(h, win_ref[:, D_FF:])
        a = (_silu(g) * u).astype(BF16)
        if i > 0:
            post(rows[i - 1], out)
        out = _dot(a, wout_ref[...])
        h = h_next
    post(rows[parts - 1], out)


def _ffn(x, ada, sub, g_pre, g_post, w_in, w_out, *, per_token, tm, prompt_row_block, jobs=()):
    n, l, _ = x.shape
    n_l = l // tm
    kern = functools.partial(_ffn_kernel, per_token=per_token, weight=MACARON_WEIGHT,
                             job_groups=tuple(j.static for j in jobs))
    cast_in, cast_out, cast_shapes = _cast_plumbing(jobs, lambda n, l: n * n_l + l)
    return pl.pallas_call(
        kern,
        grid=(n, n_l),
        in_specs=[pl.BlockSpec((1, tm, D_MODEL), lambda n, l: (n, l, 0))]
        + _mod_specs(sub, per_token, prompt_row_block)
        + [_resident((1, D_MODEL)), _resident((1, D_MODEL)),
           _resident((D_MODEL, 2 * D_FF)), _resident((D_FF, D_MODEL))]
        + cast_in,
        out_specs=[pl.BlockSpec((1, tm, D_MODEL), lambda n, l: (n, l, 0))] + cast_out,
        out_shape=[jax.ShapeDtypeStruct(x.shape, F32)] + cast_shapes,
        compiler_params=_params(("arbitrary", "arbitrary")),
        name="ffn",
    )(x, ada, ada, ada, g_pre, g_post, w_in, w_out, *[j.src for j in jobs])


def _log_decay(hb, wz_ref, walpha_ref, balpha_ref):
    z = _dot(hb, wz_ref[...]).astype(BF16)
    xg = _dot(z, walpha_ref[...]) + balpha_ref[...]
    return jax.nn.log_sigmoid(xg) * (1.0 / GATE_TAU)


def _merge_and_project(y_gla, y_conv, sig_a, sig_b, wbo_ref, wmo_ref):
    pg = _dot(y_gla, wbo_ref[0])
    pc = _dot(y_conv, wbo_ref[1])
    merged = (sig_a * pg + sig_b * pc).astype(BF16)
    return _dot(merged, wmo_ref[...])


_PAIR_LEVELS = tuple(GLA_CHUNK >> i for i in range(1, GLA_CHUNK.bit_length()))


def _pair_operands(qs, k, b, log_a, m, tm):
    n = tm // (2 * m)
    if m >= SUBLANES:
        b4 = b.reshape(n, 2 * m, QK_WIDTH)
        w = jnp.exp(-jnp.abs(b4 - b4[:, m - 1:m, :]))
        halves = lambda a: a.reshape(n, 2, m, QK_WIDTH)
        src = jnp.concatenate([halves(k)[:, 0:1], halves(qs)[:, 1:2]], axis=1)
        return (src.reshape(n, 2 * m, QK_WIDTH) * w).astype(BF16).reshape(tm, QK_WIDTH)
    row = lax.broadcasted_iota(jnp.int32, (tm, QK_WIDTH), 0)
    second = (row & m) != 0
    if m == 1:
        w = jnp.exp(jnp.where(second, log_a, 0.0))
    else:
        b4 = b.reshape(tm // SUBLANES, SUBLANES, QK_WIDTH)
        sub = lax.broadcasted_iota(jnp.int32, b4.shape, 1)
        mid = b4[:, m - 1:m, :]
        for start in range(2 * m, SUBLANES, 2 * m):
            mid = jnp.where(sub >= start, b4[:, start + m - 1:start + m, :], mid)
        w = jnp.exp(-jnp.abs(b4 - mid)).reshape(tm, QK_WIDTH)
    return (jnp.where(second, qs, k) * w).astype(BF16)


def _pair_mask(m):
    t = lax.broadcasted_iota(jnp.int32, (GLA_CHUNK, GLA_CHUNK), 0)
    s = lax.broadcasted_iota(jnp.int32, (GLA_CHUNK, GLA_CHUNK), 1)
    return ((t & m) != 0) & ((s & m) == 0) & ((t // (2 * m)) == (s // (2 * m)))


def _mixer_kernel(x_ref, sh_ref, sc_ref, gt_ref, gpre_ref, gpost_ref, wa_ref, wz_ref, wb_ref,
                  walpha_ref, balpha_ref, ggla_ref, wconv_ref, wbo_ref, wmo_ref, *rest,
                  tm, job_groups):
    n_jobs = len(job_groups)
    y_ref, sgla_ref, sconv_ref = rest[n_jobs:n_jobs + 3]
    st_scr, u_scr, ygla_scr = rest[-3:]
    _run_casts(job_groups, rest[:n_jobs], rest[n_jobs + 3:-3])
    l = pl.program_id(1)
    n_l = pl.num_programs(1)

    @pl.when(l == 0)
    def _():
        st_scr[...] = jnp.zeros_like(st_scr)
        u_scr[0:SUBLANES, :] = jnp.zeros((SUBLANES, D_CONV), F32)

    x = x_ref[0]
    shift = _load_mod(sh_ref, False)
    scale = _load_mod(sc_ref, False)
    gate = _load_mod(gt_ref, False)
    hb = (_rms(x, gpre_ref[...]) * (1.0 + scale) + shift).astype(BF16)

    def proj_b(j):
        return _dot(hb, wb_ref[:, j * D_MODEL:(j + 1) * D_MODEL])

    nc = tm // GLA_CHUNK
    log_a = _log_decay(hb, wz_ref, walpha_ref, balpha_ref)
    pa = _dot(hb, wa_ref[...])
    qs = pa[:, 0:QK_WIDTH] * (DK ** -0.5)
    k = pa[:, QK_WIDTH:2 * QK_WIDTH]
    vb = pa[:, 2 * QK_WIDTH:2 * QK_WIDTH + V_WIDTH].astype(BF16)
    r = pa[:, 2 * QK_WIDTH + V_WIDTH:]

    row = lax.broadcasted_iota(jnp.int32, (tm, tm), 0)
    col = lax.broadcasted_iota(jnp.int32, (tm, tm), 1)
    tri = jnp.where((row >= col) & (row // GLA_CHUNK == col // GLA_CHUNK), 1.0, 0.0).astype(BF16)
    la_hi = log_a.astype(BF16)
    rem = log_a - la_hi.astype(F32)
    la_mid = rem.astype(BF16)
    la_lo = (rem - la_mid.astype(F32)).astype(BF16)
    b3 = (_dot(tri, la_hi) + _dot(tri, la_mid) + _dot(tri, la_lo)).reshape(
        nc, GLA_CHUNK, QK_WIDTH)

    u = proj_b(1) * proj_b(2)
    u_scr[SUBLANES:SUBLANES + tm, :] = u

    b_last = b3[:, GLA_CHUNK - 1:GLA_CHUNK, :]
    b = b3.reshape(tm, QK_WIDTH)
    qd = (qs * jnp.exp(b)).astype(BF16).reshape(nc, GLA_CHUNK, QK_WIDTH)
    kl = (k.reshape(nc, GLA_CHUNK, QK_WIDTH) * jnp.exp(b_last - b3)).astype(BF16)
    b_last_rows = jnp.concatenate(
        [b_last[c, :, hd * DK:(hd + 1) * DK] for c in range(nc) for hd in range(N_HEADS)]
        + [jnp.zeros((LANES - nc * N_HEADS, DK), F32)], axis=0)
    decay_cols = jnp.exp(b_last_rows.T)

    pair_ops = [_pair_operands(qs, k, b, log_a, m, tm) for m in _PAIR_LEVELS]
    pair_masks = [_pair_mask(m) for m in _PAIR_LEVELS]
    qs_b, k_b = qs.astype(BF16), k.astype(BF16)
    crow = lax.broadcasted_iota(jnp.int32, (GLA_CHUNK, GLA_CHUNK), 0)
    ccol = lax.broadcasted_iota(jnp.int32, (GLA_CHUNK, GLA_CHUNK), 1)
    scores, upd = {}, {}
    for c in range(nc):
        rows = slice(c * GLA_CHUNK, (c + 1) * GLA_CHUNK)
        for hd in range(N_HEADS):
            kc = slice(hd * DK, (hd + 1) * DK)
            s = jnp.where(crow == ccol, _dot_nt(qs_b[rows, kc], k_b[rows, kc]), 0.0)
            for p, mask in zip(pair_ops, pair_masks):
                s = s + jnp.where(mask, _dot_nt(p[rows, kc], p[rows, kc]), 0.0)
            scores[c, hd] = s.astype(BF16)
            upd[c, hd] = _dot_tn(kl[c, :, kc], vb[rows, hd * DV:(hd + 1) * DV])

    conv = (wconv_ref[0:1, :] * u_scr[SUBLANES - 2:SUBLANES - 2 + tm, :]
            + wconv_ref[1:2, :] * u_scr[SUBLANES - 1:SUBLANES - 1 + tm, :]
            + wconv_ref[2:3, :] * u)
    u_scr[0:SUBLANES, :] = u_scr[tm:tm + SUBLANES, :]
    y_conv = (proj_b(0) * conv).astype(BF16)

    outs = {}
    for hd in range(N_HEADS):
        kc = slice(hd * DK, (hd + 1) * DK)
        vc = slice(hd * DV, (hd + 1) * DV)
        st = st_scr[hd]
        for c in range(nc):
            rows = slice(c * GLA_CHUNK, (c + 1) * GLA_CHUNK)
            lhs = jnp.concatenate([qd[c, :, kc], scores[c, hd]], axis=1)
            rhs = jnp.concatenate([st.astype(BF16), vb[rows, vc]], axis=0)
            outs[c, hd] = _dot(lhs, rhs)
            j = c * N_HEADS + hd
            st = st * decay_cols[:, j:j + 1] + upd[c, hd]
        st_scr[hd] = st

    sig_a = jax.nn.sigmoid(proj_b(3))
    sig_b = jax.nn.sigmoid(proj_b(4))

    for hd in range(N_HEADS):
        vc = slice(hd * DV, (hd + 1) * DV)
        for c in range(nc):
            rows = slice(c * GLA_CHUNK, (c + 1) * GLA_CHUNK)
            on = _rms(outs[c, hd], ggla_ref[:, vc])
            ygla_scr[rows, vc] = (on * _silu(r[rows, vc])).astype(BF16)

    mix = _merge_and_project(ygla_scr[...], y_conv, sig_a, sig_b, wbo_ref, wmo_ref)
    y_ref[0] = x + gate * _rms(mix, gpost_ref[...])

    @pl.when(l == n_l - 1)
    def _():
        sgla_ref[0, 0] = st_scr[...]
        sconv_ref[0, 0] = u_scr[SUBLANES - 2:SUBLANES, :]


def _mixer(x, ada, g_pre, g_post, wts, *, prompt_row_block, jobs=()):
    n, l, _ = x.shape
    tm = MIX_TM
    n_l = l // tm
    kern = functools.partial(_mixer_kernel, tm=tm, job_groups=tuple(j.static for j in jobs))
    cast_in, cast_out, cast_shapes = _cast_plumbing(jobs, lambda n, l: n * n_l + l)
    wa, wz, wb, walpha, balpha, ggla, wconv, wbo, wmo = wts
    return pl.pallas_call(
        kern,
        grid=(n, n_l),
        in_specs=[pl.BlockSpec((1, tm, D_MODEL), lambda n, l: (n, l, 0))]
        + _mod_specs(1, False, prompt_row_block)
        + [_resident(a.shape) for a in (g_pre, g_post, wa, wz, wb, walpha, balpha, ggla, wconv,
                                        wbo, wmo)]
        + cast_in,
        out_specs=[
            pl.BlockSpec((1, tm, D_MODEL), lambda n, l: (n, l, 0)),
            pl.BlockSpec((1, 1, N_HEADS, DK, DV), lambda n, l: (0, n, 0, 0, 0)),
            pl.BlockSpec((1, 1, CONV_WIDTH - 1, D_CONV), lambda n, l: (0, n, 0, 0)),
        ] + cast_out,
        out_shape=[
            jax.ShapeDtypeStruct(x.shape, F32),
            jax.ShapeDtypeStruct((1, n, N_HEADS, DK, DV), F32),
            jax.ShapeDtypeStruct((1, n, CONV_WIDTH - 1, D_CONV), F32),
        ] + cast_shapes,
        scratch_shapes=[
            pltpu.VMEM((N_HEADS, DK, DV), F32),
            pltpu.VMEM((tm + SUBLANES, D_CONV), F32),
            pltpu.VMEM((tm, V_WIDTH), BF16),
        ],
        compiler_params=_params(("arbitrary", "arbitrary")),
        name="mixer",
    )(x, ada, ada, ada, g_pre, g_post, wa, wz, wb, walpha, balpha, ggla, wconv, wbo, wmo,
      *[j.src for j in jobs])


def _smix_pre_kernel(x_ref, sh_ref, sc_ref, gpre_ref, wa_ref, wz_ref, wb_ref, walpha_ref,
                     balpha_ref, wconv_ref, sconv_ref,
                     qt_ref, kt_ref, at_ref, v_ref, sr_ref, yconv_ref, sga_ref, sgb_ref,
                     sconv_new_ref):
    x = x_ref[...]
    hb = (_rms(x, gpre_ref[...]) * (1.0 + sc_ref[...]) + sh_ref[...]).astype(BF16)
    pa = _dot(hb, wa_ref[...])
    q = pa[:, 0:QK_WIDTH] * (DK ** -0.5)
    k = pa[:, QK_WIDTH:2 * QK_WIDTH]
    v_ref[...] = pa[:, 2 * QK_WIDTH:2 * QK_WIDTH + V_WIDTH]
    sr_ref[...] = _silu(pa[:, 2 * QK_WIDTH + V_WIDTH:])
    a = jnp.exp(_log_decay(hb, wz_ref, walpha_ref, balpha_ref))
    for hd in range(N_HEADS):
        kc = slice(hd * DK, (hd + 1) * DK)
        qt_ref[hd] = q[:, kc].T
        kt_ref[hd] = k[:, kc].T
        at_ref[hd] = a[:, kc].T

    pb = _dot(hb, wb_ref[...])
    u = pb[:, D_CONV:2 * D_CONV] * pb[:, 2 * D_CONV:3 * D_CONV]
    conv = (wconv_ref[0:1, :] * sconv_ref[0] + wconv_ref[1:2, :] * sconv_ref[1]
            + wconv_ref[2:3, :] * u)
    yconv_ref[...] = pb[:, 0:D_CONV] * conv
    sga_ref[...] = jax.nn.sigmoid(pb[:, 3 * D_CONV:3 * D_CONV + D_MODEL])
    sgb_ref[...] = jax.nn.sigmoid(pb[:, 3 * D_CONV + D_MODEL:])
    sconv_new_ref[0] = sconv_ref[1]
    sconv_new_ref[1] = u


def _smix_state_kernel(s_ref, qt_ref, kt_ref, at_ref, v_ref, snew_ref, o_ref):
    base = pl.program_id(0) * SAMPLE_BLOCK
    lane = lax.broadcasted_iota(jnp.int32, (1, LANES), 1)
    for i in range(SAMPLE_BLOCK):
        pick = jnp.where(lane == base + i, 1.0, 0.0)
        for hd in range(N_HEADS):
            vc = slice(hd * DV, (hd + 1) * DV)
            a_col = jnp.sum(at_ref[hd] * pick, axis=1, keepdims=True)
            k_col = jnp.sum(kt_ref[hd] * pick, axis=1, keepdims=True)
            q_col = jnp.sum(qt_ref[hd] * pick, axis=1, keepdims=True)
            s_new = a_col * s_ref[0, i, hd] + k_col * v_ref[i:i + 1, vc]
            snew_ref[0, i, hd] = s_new
            o_ref[i:i + 1, vc] = jnp.sum(q_col * s_new, axis=0, keepdims=True)


def _smix_post_kernel(x_ref, gt_ref, gpost_ref, o_ref, sr_ref, yconv_ref, sga_ref, sgb_ref,
                      ggla_ref, wbo_ref, wmo_ref, y_ref, ygla_scr):
    for hd in range(N_HEADS):
        vc = slice(hd * DV, (hd + 1) * DV)
        on = _rms(o_ref[:, vc], ggla_ref[:, vc])
        ygla_scr[:, vc] = (on * sr_ref[:, vc]).astype(BF16)
    mix = _merge_and_project(ygla_scr[...], yconv_ref[...].astype(BF16), sga_ref[...],
                             sgb_ref[...], wbo_ref, wmo_ref)
    y_ref[...] = x_ref[...] + gt_ref[...] * _rms(mix, gpost_ref[...])


def _sample_mixer(x, ada, state_gla, sconv_t, g_pre, g_post, wts):
    s = x.shape[0]
    wa, wz, wb, walpha, balpha, ggla, wconv, wbo, wmo = wts
    whole = lambda a: pl.BlockSpec(a.shape, lambda *_: (0,) * a.ndim)
    mod = lambda j: pl.BlockSpec((s, D_MODEL), lambda *_: (0, 3 + j))
    tok = jax.ShapeDtypeStruct((s, D_MODEL), F32)
    tr = jax.ShapeDtypeStruct((N_HEADS, DK, s), F32)

    pre_in = (x, ada, ada, g_pre, wa, wz, wb, walpha, balpha, wconv, sconv_t)
    pre_specs = [whole(x), mod(0), mod(1)] + [whole(a) for a in pre_in[3:]]
    pre_out = [tr, tr, tr, tok, tok, tok, tok, tok, jax.ShapeDtypeStruct(sconv_t.shape, F32)]
    qt, kt, at, v, sr, yconv, sga, sgb, sconv_new = pl.pallas_call(
        _smix_pre_kernel,
        grid=(1,),
        in_specs=pre_specs,
        out_specs=[whole(o) for o in pre_out],
        out_shape=pre_out,
        compiler_params=_params(("arbitrary",)),
        name="smix_pre",
    )(*pre_in)

    state_block = pl.BlockSpec((1, SAMPLE_BLOCK, N_HEADS, DK, DV), lambda j: (0, j, 0, 0, 0))
    snew, o = pl.pallas_call(
        _smix_state_kernel,
        grid=(s // SAMPLE_BLOCK,),
        in_specs=[state_block, whole(qt), whole(kt), whole(at),
                  pl.BlockSpec((SAMPLE_BLOCK, V_WIDTH), lambda j: (j, 0))],
        out_specs=[state_block, pl.BlockSpec((SAMPLE_BLOCK, V_WIDTH), lambda j: (j, 0))],
        out_shape=[jax.ShapeDtypeStruct(state_gla.shape, F32), tok],
        compiler_params=_params(("arbitrary",)),
        name="smix_state",
    )(state_gla, qt, kt, at, v)

    post_in = (x, ada, g_post, o, sr, yconv, sga, sgb, ggla, wbo, wmo)
    post_specs = [whole(x), mod(2)] + [whole(a) for a in post_in[2:]]
    y = pl.pallas_call(
        _smix_post_kernel,
        grid=(1,),
        in_specs=post_specs,
        out_specs=whole(tok),
        out_shape=tok,
        scratch_shapes=[pltpu.VMEM((s, V_WIDTH), BF16)],
        compiler_params=_params(("arbitrary",)),
        name="smix_post",
    )(*post_in)
    return y, snew, sconv_new


def kernel(x_prompt, x_sample, state_gla, state_conv, c_prompt, c_sample, w_ada, b_ada, g_pre,
           g_post, w_ffn1_in, w_ffn1_out, w_ffn2_in, w_ffn2_out, w_mix_in, w_alpha, b_alpha,
           g_gla_norm, w_conv, w_branch_out, w_mix_out):
    depth = w_ada.shape[0]
    n_s = x_sample.shape[0]
    assert depth == 1 and x_sample.shape[1] == 1 and n_s == LANES
    assert c_prompt.shape[0] == SUBLANES and n_s % SUBLANES == 0
    prompt_row_block = n_s // SUBLANES

    yp = x_prompt
    ys = x_sample.reshape(1, n_s, D_MODEL)
    c_all = jnp.concatenate([c_sample, c_prompt], axis=0)
    outs = []
    for i in range(depth):
        ada = _ada(c_all, w_ada[i], b_ada[i])
        gpre = [g_pre[i, j].reshape(1, D_MODEL) for j in range(N_SUBLAYERS)]
        gpost = [g_post[i, j].reshape(1, D_MODEL) for j in range(N_SUBLAYERS)]
        ffn_p = functools.partial(_ffn, per_token=False, tm=FFN_TM,
                                  prompt_row_block=prompt_row_block)
        ffn_s = functools.partial(_ffn, per_token=True, tm=n_s,
                                  prompt_row_block=prompt_row_block)

        ffn1_w = (w_ffn1_in[i].astype(BF16), w_ffn1_out[i].astype(BF16))
        n_ffn_steps = yp.shape[0] * (yp.shape[1] // FFN_TM)
        n_mix_steps = yp.shape[0] * (yp.shape[1] // MIX_TM)
        w_mix_t = jnp.swapaxes(w_mix_in[i], 0, 1)
        mixer_jobs = (
            _CastJob(w_mix_t, 2 * LANES, _A_WIDTH // (2 * LANES), transpose=True),
            _CastJob(w_mix_t, LANES, 1, transpose=True, row0=_A_WIDTH, valid=GATE_RANK),
            _CastJob(w_mix_t, 4 * LANES, _B_WIDTH // (4 * LANES), transpose=True,
                     row0=_A_WIDTH + GATE_RANK),
            _whole_cast(w_branch_out[i].reshape(2 * D_MODEL, D_MODEL), n_ffn_steps),
            _whole_cast(w_mix_out[i], n_ffn_steps))
        assert all(j.n_blocks <= n_ffn_steps for j in mixer_jobs)
        yp, wa, wz, wb, wbo, wmo = ffn_p(yp, ada, 0, gpre[0], gpost[0], *ffn1_w, jobs=mixer_jobs)
        walpha = jnp.pad(w_alpha[i], ((0, LANES - GATE_RANK), (0, 0))).astype(BF16)
        mix_w = (wa, wz, wb, walpha, b_alpha[i].reshape(1, QK_WIDTH),
                 g_gla_norm[i].reshape(1, V_WIDTH),
                 jnp.pad(w_conv[i], ((0, SUBLANES - CONV_WIDTH), (0, 0))),
                 wbo.reshape(2, D_MODEL, D_MODEL), wmo)
        yp, gla_p, conv_p, *ffn2_w = _mixer(
            yp, ada, gpre[1], gpost[1], mix_w, prompt_row_block=prompt_row_block,
            jobs=(_whole_cast(w_ffn2_in[i], n_mix_steps), _whole_cast(w_ffn2_out[i], n_mix_steps)))
        yp, = ffn_p(yp, ada, 2, gpre[2], gpost[2], *ffn2_w)

        ys, = ffn_s(ys, ada, 0, gpre[0], gpost[0], *ffn1_w)
        sconv_t = jnp.swapaxes(state_conv[i], 0, 1)
        ys2, gla_s, sconv_new = _sample_mixer(ys[0], ada, state_gla[i:i + 1], sconv_t,
                                              gpre[1], gpost[1], mix_w)
        ys, = ffn_s(ys2.reshape(1, n_s, D_MODEL), ada, 2, gpre[2], gpost[2], *ffn2_w)
        outs.append((gla_p, conv_p, gla_s, jnp.swapaxes(sconv_new, 0, 1)[None]))

    gla_p, conv_p, gla_s, conv_s = outs[0]
    return (yp, ys.reshape(n_s, 1, D_MODEL), gla_p, conv_p, gla_s, conv_s)
```

```python
import functools
from typing import NamedTuple

import jax
import jax.numpy as jnp
from jax import lax
from jax.experimental import pallas as pl
from jax.experimental.pallas import tpu as pltpu

F32 = jnp.float32
BF16 = jnp.bfloat16

D_MODEL = 1024
D_FF = 2816
N_SUBLAYERS = 3
N_HEADS = 4
DK = 128
DV = 256
QK_WIDTH = N_HEADS * DK
V_WIDTH = N_HEADS * DV
GATE_RANK = 16
GATE_TAU = 16.0
D_CONV = D_MODEL
CONV_WIDTH = 3
RMS_EPS = 1e-6
MACARON_WEIGHT = 0.5

LANES = 128
SUBLANES = 8
GLA_CHUNK = 64
VMEM_LIMIT_BYTES = 56 * 1024 * 1024

FFN_TM = 1024
FFN_PART = 256
MIX_TM = 512
MIX_PART = 256
ADA_TN = 1536
SAMPLE_BLOCK = 8

_A_WIDTH = 2 * QK_WIDTH + 2 * V_WIDTH
_B_WIDTH = 3 * D_CONV + 2 * D_MODEL


def _rms(x, g):
    return x * lax.rsqrt(jnp.mean(x * x, axis=-1, keepdims=True) + RMS_EPS) * g


def _silu(x):
    return x * jax.nn.sigmoid(x)


def _dot(a, b):
    return jnp.dot(a, b, preferred_element_type=F32)


def _dot_nt(a, b):
    return lax.dot_general(a, b, (((1,), (1,)), ((), ())), preferred_element_type=F32)


def _dot_tn(a, b):
    return lax.dot_general(a, b, (((0,), (0,)), ((), ())), preferred_element_type=F32)


def _resident(shape):
    zeros = (0,) * len(shape)
    return pl.BlockSpec(shape, lambda *_: zeros, pipeline_mode=pl.Buffered(1))


def _params(semantics):
    return pltpu.CompilerParams(dimension_semantics=semantics, vmem_limit_bytes=VMEM_LIMIT_BYTES)


def _ada_kernel(c_ref, w_ref, b_ref, o_ref):
    c = c_ref[...]
    o_ref[...] = _dot(_silu(c).astype(BF16), w_ref[...].astype(BF16)) + b_ref[...]


def _ada(c_all, w_ada, b_ada):
    rows = c_all.shape[0]
    width = w_ada.shape[1]
    return pl.pallas_call(
        _ada_kernel,
        grid=(width // ADA_TN,),
        in_specs=[
            pl.BlockSpec((rows, D_MODEL), lambda j: (0, 0)),
            pl.BlockSpec((D_MODEL, ADA_TN), lambda j: (0, j)),
            pl.BlockSpec((1, ADA_TN), lambda j: (0, j)),
        ],
        out_specs=pl.BlockSpec((rows, ADA_TN), lambda j: (0, j)),
        out_shape=jax.ShapeDtypeStruct((rows, width), F32),
        compiler_params=_params(("arbitrary",)),
        name="ada",
    )(c_all, w_ada, b_ada.reshape(1, width))


def _mod_specs(sub, per_token, prompt_row_block):
    specs = []
    for j in range(3):
        col = sub * 3 + j
        if per_token:
            specs.append(pl.BlockSpec((LANES, D_MODEL), lambda n, l, col=col: (0, col)))
        else:
            specs.append(pl.BlockSpec((SUBLANES, D_MODEL),
                                      lambda n, l, col=col: (prompt_row_block, col)))
    return specs


def _load_mod(ref, per_token):
    if per_token:
        return ref[...]
    return ref[pl.ds(pl.program_id(0), 1), :]


BF16_SUBLANES = 16


class _CastJob(NamedTuple):
    src: jax.Array
    block_rows: int
    n_blocks: int
    transpose: bool = False
    row0: int = 0
    valid: int = 0

    def specs(self, step_of):
        cols = self.src.shape[1]
        r, nb = self.block_rows, self.n_blocks
        block = lambda *g: jnp.minimum(step_of(*g), nb - 1)
        if not self.transpose:
            spec = pl.BlockSpec((r, cols), lambda *g: (block(*g), 0))
            return spec, spec, jax.ShapeDtypeStruct((nb * r, cols), BF16)
        assert self.row0 % SUBLANES == 0 and r % SUBLANES == 0
        return (pl.BlockSpec((pl.Element(r), pl.Element(cols)),
                             lambda *g: (pl.multiple_of(self.row0 + r * block(*g), SUBLANES), 0)),
                pl.BlockSpec((cols, r), lambda *g: (0, block(*g))),
                jax.ShapeDtypeStruct((cols, nb * r), BF16))

    @property
    def static(self):
        return (self.transpose, self.valid or self.block_rows)


def _whole_cast(w, n_steps):
    rows = w.shape[0]
    r = next(r for r in range(BF16_SUBLANES, rows + 1, BF16_SUBLANES)
             if rows % r == 0 and rows // r <= n_steps)
    return _CastJob(w, r, rows // r)


def _cast_plumbing(jobs, step_of):
    specs = [job.specs(step_of) for job in jobs]
    return ([i for i, _, _ in specs], [o for _, o, _ in specs], [s for _, _, s in specs])


def _run_casts(statics, src_refs, dst_refs):
    for (transpose, valid), src, dst in zip(statics, src_refs, dst_refs, strict=True):
        blk = src[...]
        if transpose:
            blk = blk.T
            if valid < blk.shape[1]:
                lane = lax.broadcasted_iota(jnp.int32, blk.shape, 1)
                blk = jnp.where(lane < valid, blk, 0.0)
        dst[...] = blk.astype(BF16)


def _ffn_kernel(x_ref, sh_ref, sc_ref, gt_ref, gpre_ref, gpost_ref, win_ref, wout_ref, *rest,
                per_token, weight, job_groups):
    n_jobs = len(job_groups)
    o_ref = rest[n_jobs]
    _run_casts(job_groups, rest[:n_jobs], rest[n_jobs + 1:])
    shift = _load_mod(sh_ref, per_token)
    scale = _load_mod(sc_ref, per_token)
    gate = _load_mod(gt_ref, per_token)
    tm = x_ref.shape[1]
    parts = tm // FFN_PART if tm > FFN_PART else 1
    rows = [slice(i * (tm // parts), (i + 1) * (tm // parts)) for i in range(parts)]
    mod = lambda m, rs: m[rs] if per_token else m

    def pre(rs):
        return (_rms(x_ref[0, rs, :], gpre_ref[...]) * (1.0 + mod(scale, rs))
                + mod(shift, rs)).astype(BF16)

    def post(rs, out):
        o_ref[0, rs, :] = x_ref[0, rs, :] + weight * mod(gate, rs) * _rms(out, gpost_ref[...])

    h = pre(rows[0])
    for i in range(parts):
        g = _dot(h, win_ref[:, :D_FF])
        h_next = pre(rows[i + 1]) if i + 1 < parts else None
        u = _dot(h, win_ref[:, D_FF:])
        a = (_silu(g) * u).astype(BF16)
        if i > 0:
            post(rows[i - 1], out)
        out = _dot(a, wout_ref[...])
        h = h_next
    post(rows[parts - 1], out)


def _ffn(x, ada, sub, g_pre, g_post, w_in, w_out, *, per_token, tm, prompt_row_block, jobs=()):
    n, l, _ = x.shape
    n_l = l // tm
    kern = functools.partial(_ffn_kernel, per_token=per_token, weight=MACARON_WEIGHT,
                             job_groups=tuple(j.static for j in jobs))
    cast_in, cast_out, cast_shapes = _cast_plumbing(jobs, lambda n, l: n * n_l + l)
    return pl.pallas_call(
        kern,
        grid=(n, n_l),
        in_specs=[pl.BlockSpec((1, tm, D_MODEL), lambda n, l: (n, l, 0))]
        + _mod_specs(sub, per_token, prompt_row_block)
        + [_resident((1, D_MODEL)), _resident((1, D_MODEL)),
           _resident((D_MODEL, 2 * D_FF)), _resident((D_FF, D_MODEL))]
        + cast_in,
        out_specs=[pl.BlockSpec((1, tm, D_MODEL), lambda n, l: (n, l, 0))] + cast_out,
        out_shape=[jax.ShapeDtypeStruct(x.shape, F32)] + cast_shapes,
        compiler_params=_params(("arbitrary", "arbitrary")),
        name="ffn",
    )(x, ada, ada, ada, g_pre, g_post, w_in, w_out, *[j.src for j in jobs])


def _log_decay(hb, wz_ref, walpha_ref, balpha_ref):
    z = _dot(hb, wz_ref[...]).astype(BF16)
    xg = _dot(z, walpha_ref[...]) + balpha_ref[...]
    return jax.nn.log_sigmoid(xg) * (1.0 / GATE_TAU)


def _merge_and_project(y_gla, y_conv, sig_a, sig_b, wbo_ref, wmo_ref):
    pg = _dot(y_gla, wbo_ref[0])
    pc = _dot(y_conv, wbo_ref[1])
    merged = (sig_a * pg + sig_b * pc).astype(BF16)
    return _dot(merged, wmo_ref[...])


_PAIR_LEVELS = tuple(GLA_CHUNK >> i for i in range(1, GLA_CHUNK.bit_length()))


def _pair_operands(qs, k, b, log_a, m, tm):
    if m >= SUBLANES:
        halves = lambda a: a.reshape(tm // (2 * m), 2, m, QK_WIDTH)
        b4, k4, q4 = halves(b), halves(k), halves(qs)
        mid = b4[:, 0:1, m - 1:m, :]
        w = jnp.exp(jnp.concatenate([mid - b4[:, 0:1], b4[:, 1:2] - mid], axis=1))
        src = jnp.concatenate([k4[:, 0:1], q4[:, 1:2]], axis=1)
        return (src * w).astype(BF16).reshape(tm, QK_WIDTH)
    tiles = lambda a: a.reshape(tm // SUBLANES, SUBLANES, QK_WIDTH)
    sub = lax.broadcasted_iota(jnp.int32, (1, SUBLANES, QK_WIDTH), 1)
    second = (sub & m) != 0
    if m == 1:
        w = jnp.exp(jnp.where(second, tiles(log_a), 0.0))
    else:
        b4 = tiles(b)
        mid = b4[:, m - 1:m, :]
        for start in range(2 * m, SUBLANES, 2 * m):
            mid = jnp.where(sub >= start, b4[:, start + m - 1:start + m, :], mid)
        w = jnp.exp((b4 - mid) * jnp.where(second, 1.0, -1.0))
    return (jnp.where(second, tiles(qs), tiles(k)) * w).astype(BF16).reshape(tm, QK_WIDTH)


def _pair_mask(m):
    t = lax.broadcasted_iota(jnp.int32, (GLA_CHUNK, GLA_CHUNK), 0)
    s = lax.broadcasted_iota(jnp.int32, (GLA_CHUNK, GLA_CHUNK), 1)
    return ((t & m) != 0) & ((s & m) == 0) & ((t // (2 * m)) == (s // (2 * m)))


def _mixer_kernel(x_ref, sh_ref, sc_ref, gt_ref, gpre_ref, gpost_ref, wa_ref, wz_ref, wb_ref,
                  walpha_ref, balpha_ref, ggla_ref, wconv_ref, wbo_ref, wmo_ref, *rest,
                  job_groups):
    n_jobs = len(job_groups)
    y_ref, sgla_ref, sconv_ref = rest[n_jobs:n_jobs + 3]
    st_scr, u_scr, ygla_scr = rest[-3:]
    _run_casts(job_groups, rest[:n_jobs], rest[n_jobs + 3:-3])
    l = pl.program_id(1)
    n_l = pl.num_programs(1)

    @pl.when(l == 0)
    def _():
        st_scr[...] = jnp.zeros_like(st_scr)
        u_scr[0:SUBLANES, :] = jnp.zeros((SUBLANES, D_CONV), F32)

    mods = (_load_mod(sh_ref, False), _load_mod(sc_ref, False), _load_mod(gt_ref, False))
    weights = (gpre_ref, gpost_ref, wa_ref, wz_ref, wb_ref, walpha_ref, balpha_ref, ggla_ref,
               wconv_ref, wbo_ref, wmo_ref)
    n_parts = x_ref.shape[1] // MIX_PART
    parts = [_mixer_part(x_ref, y_ref, mods, weights, (st_scr, u_scr, ygla_scr),
                         slice(i * MIX_PART, (i + 1) * MIX_PART)) for i in range(n_parts)]
    next(parts[0])
    for i, p in enumerate(parts):
        next(p)
        if i + 1 < n_parts:
            next(parts[i + 1])
        if i > 0:
            next(parts[i - 1], None)
        next(p)
    next(parts[-1], None)

    @pl.when(l == n_l - 1)
    def _():
        sgla_ref[0, 0] = st_scr[...]
        sconv_ref[0, 0] = u_scr[SUBLANES - 2:SUBLANES, :]


def _mixer_part(x_ref, y_ref, mods, weights, scratch, prow):
    shift, scale, gate = mods
    (gpre_ref, gpost_ref, wa_ref, wz_ref, wb_ref, walpha_ref, balpha_ref, ggla_ref, wconv_ref,
     wbo_ref, wmo_ref) = weights
    st_scr, u_scr, ygla_scr = scratch
    ygla_scr = ygla_scr.at[prow]
    tm = MIX_PART
    x = x_ref[0, prow, :]
    hb = (_rms(x, gpre_ref[...]) * (1.0 + scale) + shift).astype(BF16)
    yield

    def proj_b(j):
        return _dot(hb, wb_ref[:, j * D_MODEL:(j + 1) * D_MODEL])

    nc = tm // GLA_CHUNK
    log_a = _log_decay(hb, wz_ref, walpha_ref, balpha_ref)
    pa = _dot(hb, wa_ref[...])
    qs = pa[:, 0:QK_WIDTH] * (DK ** -0.5)
    k = pa[:, QK_WIDTH:2 * QK_WIDTH]
    vb = pa[:, 2 * QK_WIDTH:2 * QK_WIDTH + V_WIDTH].astype(BF16)
    r = pa[:, 2 * QK_WIDTH + V_WIDTH:]

    row = lax.broadcasted_iota(jnp.int32, (tm, tm), 0)
    col = lax.broadcasted_iota(jnp.int32, (tm, tm), 1)
    tri = jnp.where((row >= col) & (row // GLA_CHUNK == col // GLA_CHUNK), 1.0, 0.0).astype(BF16)
    la_hi = log_a.astype(BF16)
    rem = log_a - la_hi.astype(F32)
    la_mid = rem.astype(BF16)
    la_lo = (rem - la_mid.astype(F32)).astype(BF16)
    b3 = (_dot(tri, la_hi) + _dot(tri, la_mid) + _dot(tri, la_lo)).reshape(
        nc, GLA_CHUNK, QK_WIDTH)

    u = proj_b(1) * proj_b(2)
    u_scr[SUBLANES:SUBLANES + tm, :] = u
    yield

    b_last = b3[:, GLA_CHUNK - 1:GLA_CHUNK, :]
    b = b3.reshape(tm, QK_WIDTH)
    qd = (qs * jnp.exp(b)).astype(BF16).reshape(nc, GLA_CHUNK, QK_WIDTH)
    kl = (k.reshape(nc, GLA_CHUNK, QK_WIDTH) * jnp.exp(b_last - b3)).astype(BF16)
    b_last_rows = jnp.concatenate(
        [b_last[c, :, hd * DK:(hd + 1) * DK] for c in range(nc) for hd in range(N_HEADS)]
        + [jnp.zeros((LANES - nc * N_HEADS, DK), F32)], axis=0)
    decay_cols = jnp.exp(b_last_rows.T)

    pair_ops = [_pair_operands(qs, k, b, log_a, m, tm) for m in _PAIR_LEVELS]
    pair_masks = [_pair_mask(m) for m in _PAIR_LEVELS]
    qs_b, k_b = qs.astype(BF16), k.astype(BF16)
    crow = lax.broadcasted_iota(jnp.int32, (GLA_CHUNK, GLA_CHUNK), 0)
    ccol = lax.broadcasted_iota(jnp.int32, (GLA_CHUNK, GLA_CHUNK), 1)
    scores, upd = {}, {}
    for c in range(nc):
        rows = slice(c * GLA_CHUNK, (c + 1) * GLA_CHUNK)
        for hd in range(N_HEADS):
            kc = slice(hd * DK, (hd + 1) * DK)
            s = jnp.where(crow == ccol, _dot_nt(qs_b[rows, kc], k_b[rows, kc]), 0.0)
            for p, mask in zip(pair_ops, pair_masks):
                s = jnp.where(mask, _dot_nt(p[rows, kc], p[rows, kc]), s)
            scores[c, hd] = s.astype(BF16)
            upd[c, hd] = _dot_tn(kl[c, :, kc], vb[rows, hd * DV:(hd + 1) * DV])

    conv = (wconv_ref[0:1, :] * u_scr[SUBLANES - 2:SUBLANES - 2 + tm, :]
            + wconv_ref[1:2, :] * u_scr[SUBLANES - 1:SUBLANES - 1 + tm, :]
            + wconv_ref[2:3, :] * u)
    u_scr[0:SUBLANES, :] = u_scr[tm:tm + SUBLANES, :]
    y_conv = (proj_b(0) * conv).astype(BF16)

    outs = {}
    for hd in range(N_HEADS):
        kc = slice(hd * DK, (hd + 1) * DK)
        vc = slice(hd * DV, (hd + 1) * DV)
        st = st_scr[hd]
        for c in range(nc):
            rows = slice(c * GLA_CHUNK, (c + 1) * GLA_CHUNK)
            lhs = jnp.concatenate([qd[c, :, kc], scores[c, hd]], axis=1)
            rhs = jnp.concatenate([st.astype(BF16), vb[rows, vc]], axis=0)
            outs[c, hd] = _dot(lhs, rhs)
            j = c * N_HEADS + hd
            st = st * decay_cols[:, j:j + 1] + upd[c, hd]
        st_scr[hd] = st

    sig_a = jax.nn.sigmoid(proj_b(3))
    sig_b = jax.nn.sigmoid(proj_b(4))

    for hd in range(N_HEADS):
        vc = slice(hd * DV, (hd + 1) * DV)
        for c in range(nc):
            rows = slice(c * GLA_CHUNK, (c + 1) * GLA_CHUNK)
            on = _rms(outs[c, hd], ggla_ref[:, vc])
            ygla_scr[rows, vc] = (on * _silu(r[rows, vc])).astype(BF16)

    yield
    mix = _merge_and_project(ygla_scr[...], y_conv, sig_a, sig_b, wbo_ref, wmo_ref)
    y_ref[0, prow, :] = x + gate * _rms(mix, gpost_ref[...])


def _mixer(x, ada, g_pre, g_post, wts, *, prompt_row_block, jobs=()):
    n, l, _ = x.shape
    tm = MIX_TM
    n_l = l // tm
    kern = functools.partial(_mixer_kernel, job_groups=tuple(j.static for j in jobs))
    cast_in, cast_out, cast_shapes = _cast_plumbing(jobs, lambda n, l: n * n_l + l)
    wa, wz, wb, walpha, balpha, ggla, wconv, wbo, wmo = wts
    return pl.pallas_call(
        kern,
        grid=(n, n_l),
        in_specs=[pl.BlockSpec((1, tm, D_MODEL), lambda n, l: (n, l, 0))]
        + _mod_specs(1, False, prompt_row_block)
        + [_resident(a.shape) for a in (g_pre, g_post, wa, wz, wb, walpha, balpha, ggla, wconv,
                                        wbo, wmo)]
        + cast_in,
        out_specs=[
            pl.BlockSpec((1, tm, D_MODEL), lambda n, l: (n, l, 0)),
            pl.BlockSpec((1, 1, N_HEADS, DK, DV), lambda n, l: (0, n, 0, 0, 0)),
            pl.BlockSpec((1, 1, CONV_WIDTH - 1, D_CONV), lambda n, l: (0, n, 0, 0)),
        ] + cast_out,
        out_shape=[
            jax.ShapeDtypeStruct(x.shape, F32),
            jax.ShapeDtypeStruct((1, n, N_HEADS, DK, DV), F32),
            jax.ShapeDtypeStruct((1, n, CONV_WIDTH - 1, D_CONV), F32),
        ] + cast_shapes,
        scratch_shapes=[
            pltpu.VMEM((N_HEADS, DK, DV), F32),
            pltpu.VMEM((MIX_PART + SUBLANES, D_CONV), F32),
            pltpu.VMEM((tm, V_WIDTH), BF16),
        ],
        compiler_params=_params(("arbitrary", "arbitrary")),
        name="mixer",
    )(x, ada, ada, ada, g_pre, g_post, wa, wz, wb, walpha, balpha, ggla, wconv, wbo, wmo,
      *[j.src for j in jobs])


def _smix_pre_kernel(x_ref, sh_ref, sc_ref, gpre_ref, wa_ref, wz_ref, wb_ref, walpha_ref,
                     balpha_ref, wconv_ref, sconv_ref,
                     qt_ref, kt_ref, at_ref, v_ref, sr_ref, yconv_ref, sga_ref, sgb_ref,
                     sconv_new_ref):
    x = x_ref[...]
    hb = (_rms(x, gpre_ref[...]) * (1.0 + sc_ref[...]) + sh_ref[...]).astype(BF16)
    pa = _dot(hb, wa_ref[...])
    q = pa[:, 0:QK_WIDTH] * (DK ** -0.5)
    k = pa[:, QK_WIDTH:2 * QK_WIDTH]
    v_ref[...] = pa[:, 2 * QK_WIDTH:2 * QK_WIDTH + V_WIDTH]
    sr_ref[...] = _silu(pa[:, 2 * QK_WIDTH + V_WIDTH:])
    a = jnp.exp(_log_decay(hb, wz_ref, walpha_ref, balpha_ref))
    for hd in range(N_HEADS):
        kc = slice(hd * DK, (hd + 1) * DK)
        qt_ref[hd] = q[:, kc].T
        kt_ref[hd] = k[:, kc].T
        at_ref[hd] = a[:, kc].T

    pb = _dot(hb, wb_ref[...])
    u = pb[:, D_CONV:2 * D_CONV] * pb[:, 2 * D_CONV:3 * D_CONV]
    conv = (wconv_ref[0:1, :] * sconv_ref[0] + wconv_ref[1:2, :] * sconv_ref[1]
            + wconv_ref[2:3, :] * u)
    yconv_ref[...] = pb[:, 0:D_CONV] * conv
    sga_ref[...] = jax.nn.sigmoid(pb[:, 3 * D_CONV:3 * D_CONV + D_MODEL])
    sgb_ref[...] = jax.nn.sigmoid(pb[:, 3 * D_CONV + D_MODEL:])
    sconv_new_ref[0] = sconv_ref[1]
    sconv_new_ref[1] = u


def _smix_state_kernel(s_ref, qt_ref, kt_ref, at_ref, v_ref, snew_ref, o_ref):
    base = pl.program_id(0) * SAMPLE_BLOCK
    lane = lax.broadcasted_iota(jnp.int32, (1, LANES), 1)
    for i in range(SAMPLE_BLOCK):
        pick = jnp.where(lane == base + i, 1.0, 0.0)
        for hd in range(N_HEADS):
            vc = slice(hd * DV, (hd + 1) * DV)
            a_col = jnp.sum(at_ref[hd] * pick, axis=1, keepdims=True)
            k_col = jnp.sum(kt_ref[hd] * pick, axis=1, keepdims=True)
            q_col = jnp.sum(qt_ref[hd] * pick, axis=1, keepdims=True)
            s_new = a_col * s_ref[0, i, hd] + k_col * v_ref[i:i + 1, vc]
            snew_ref[0, i, hd] = s_new
            o_ref[i:i + 1, vc] = jnp.sum(q_col * s_new, axis=0, keepdims=True)


def _smix_post_kernel(x_ref, gt_ref, gpost_ref, o_ref, sr_ref, yconv_ref, sga_ref, sgb_ref,
                      ggla_ref, wbo_ref, wmo_ref, y_ref, ygla_scr):
    for hd in range(N_HEADS):
        vc = slice(hd * DV, (hd + 1) * DV)
        on = _rms(o_ref[:, vc], ggla_ref[:, vc])
        ygla_scr[:, vc] = (on * sr_ref[:, vc]).astype(BF16)
    mix = _merge_and_project(ygla_scr[...], yconv_ref[...].astype(BF16), sga_ref[...],
                             sgb_ref[...], wbo_ref, wmo_ref)
    y_ref[...] = x_ref[...] + gt_ref[...] * _rms(mix, gpost_ref[...])


def _sample_mixer(x, ada, state_gla, sconv_t, g_pre, g_post, wts):
    s = x.shape[0]
    wa, wz, wb, walpha, balpha, ggla, wconv, wbo, wmo = wts
    whole = lambda a: pl.BlockSpec(a.shape, lambda *_: (0,) * a.ndim)
    mod = lambda j: pl.BlockSpec((s, D_MODEL), lambda *_: (0, 3 + j))
    tok = jax.ShapeDtypeStruct((s, D_MODEL), F32)
    tr = jax.ShapeDtypeStruct((N_HEADS, DK, s), F32)

    pre_in = (x, ada, ada, g_pre, wa, wz, wb, walpha, balpha, wconv, sconv_t)
    pre_specs = [whole(x), mod(0), mod(1)] + [whole(a) for a in pre_in[3:]]
    pre_out = [tr, tr, tr, tok, tok, tok, tok, tok, jax.ShapeDtypeStruct(sconv_t.shape, F32)]
    qt, kt, at, v, sr, yconv, sga, sgb, sconv_new = pl.pallas_call(
        _smix_pre_kernel,
        grid=(1,),
        in_specs=pre_specs,
        out_specs=[whole(o) for o in pre_out],
        out_shape=pre_out,
        compiler_params=_params(("arbitrary",)),
        name="smix_pre",
    )(*pre_in)

    state_block = pl.BlockSpec((1, SAMPLE_BLOCK, N_HEADS, DK, DV), lambda j: (0, j, 0, 0, 0))
    snew, o = pl.pallas_call(
        _smix_state_kernel,
        grid=(s // SAMPLE_BLOCK,),
        in_specs=[state_block, whole(qt), whole(kt), whole(at),
                  pl.BlockSpec((SAMPLE_BLOCK, V_WIDTH), lambda j: (j, 0))],
        out_specs=[state_block, pl.BlockSpec((SAMPLE_BLOCK, V_WIDTH), lambda j: (j, 0))],
        out_shape=[jax.ShapeDtypeStruct(state_gla.shape, F32), tok],
        compiler_params=_params(("arbitrary",)),
        name="smix_state",
    )(state_gla, qt, kt, at, v)

    post_in = (x, ada, g_post, o, sr, yconv, sga, sgb, ggla, wbo, wmo)
    post_specs = [whole(x), mod(2)] + [whole(a) for a in post_in[2:]]
    y = pl.pallas_call(
        _smix_post_kernel,
        grid=(1,),
        in_specs=post_specs,
        out_specs=whole(tok),
        out_shape=tok,
        scratch_shapes=[pltpu.VMEM((s, V_WIDTH), BF16)],
        compiler_params=_params(("arbitrary",)),
        name="smix_post",
    )(*post_in)
    return y, snew, sconv_new


def kernel(x_prompt, x_sample, state_gla, state_conv, c_prompt, c_sample, w_ada, b_ada, g_pre,
           g_post, w_ffn1_in, w_ffn1_out, w_ffn2_in, w_ffn2_out, w_mix_in, w_alpha, b_alpha,
           g_gla_norm, w_conv, w_branch_out, w_mix_out):
    depth = w_ada.shape[0]
    n_s = x_sample.shape[0]
    assert depth == 1 and x_sample.shape[1] == 1 and n_s == LANES
    assert c_prompt.shape[0] == SUBLANES and n_s % SUBLANES == 0
    prompt_row_block = n_s // SUBLANES

    yp = x_prompt
    ys = x_sample.reshape(1, n_s, D_MODEL)
    c_all = jnp.concatenate([c_sample, c_prompt], axis=0)
    outs = []
    for i in range(depth):
        ada = _ada(c_all, w_ada[i], b_ada[i])
        gpre = [g_pre[i, j].reshape(1, D_MODEL) for j in range(N_SUBLAYERS)]
        gpost = [g_post[i, j].reshape(1, D_MODEL) for j in range(N_SUBLAYERS)]
        ffn_p = functools.partial(_ffn, per_token=False, tm=FFN_TM,
                                  prompt_row_block=prompt_row_block)
        ffn_s = functools.partial(_ffn, per_token=True, tm=n_s,
                                  prompt_row_block=prompt_row_block)

        ffn1_w = (w_ffn1_in[i].astype(BF16), w_ffn1_out[i].astype(BF16))
        n_ffn_steps = yp.shape[0] * (yp.shape[1] // FFN_TM)
        n_mix_steps = yp.shape[0] * (yp.shape[1] // MIX_TM)
        w_mix_t = jnp.swapaxes(w_mix_in[i], 0, 1)
        mixer_jobs = (
            _CastJob(w_mix_t, 2 * LANES, _A_WIDTH // (2 * LANES), transpose=True),
            _CastJob(w_mix_t, LANES, 1, transpose=True, row0=_A_WIDTH, valid=GATE_RANK),
            _CastJob(w_mix_t, 4 * LANES, _B_WIDTH // (4 * LANES), transpose=True,
                     row0=_A_WIDTH + GATE_RANK),
            _whole_cast(w_branch_out[i].reshape(2 * D_MODEL, D_MODEL), n_ffn_steps),
            _whole_cast(w_mix_out[i], n_ffn_steps))
        assert all(j.n_blocks <= n_ffn_steps for j in mixer_jobs)
        yp, wa, wz, wb, wbo, wmo = ffn_p(yp, ada, 0, gpre[0], gpost[0], *ffn1_w, jobs=mixer_jobs)
        walpha = jnp.pad(w_alpha[i], ((0, LANES - GATE_RANK), (0, 0))).astype(BF16)
        mix_w = (wa, wz, wb, walpha, b_alpha[i].reshape(1, QK_WIDTH),
                 g_gla_norm[i].reshape(1, V_WIDTH),
                 jnp.pad(w_conv[i], ((0, SUBLANES - CONV_WIDTH), (0, 0))),
                 wbo.reshape(2, D_MODEL, D_MODEL), wmo)
        yp, gla_p, conv_p, *ffn2_w = _mixer(
            yp, ada, gpre[1], gpost[1], mix_w, prompt_row_block=prompt_row_block,
            jobs=(_whole_cast(w_ffn2_in[i], n_mix_steps), _whole_cast(w_ffn2_out[i], n_mix_steps)))
        yp, = ffn_p(yp, ada, 2, gpre[2], gpost[2], *ffn2_w)

        ys, = ffn_s(ys, ada, 0, gpre[0], gpost[0], *ffn1_w)
        sconv_t = jnp.swapaxes(state_conv[i], 0, 1)
        ys2, gla_s, sconv_new = _sample_mixer(ys[0], ada, state_gla[i:i + 1], sconv_t,
                                              gpre[1], gpost[1], mix_w)
        ys, = ffn_s(ys2.reshape(1, n_s, D_MODEL), ada, 2, gpre[2], gpost[2], *ffn2_w)
        outs.append((gla_p, conv_p, gla_s, jnp.swapaxes(sconv_new, 0, 1)[None]))

    gla_p, conv_p, gla_s, conv_s = outs[0]
    return (yp, ys.reshape(n_s, 1, D_MODEL), gla_p, conv_p, gla_s, conv_s)
```

```python
import functools
from typing import NamedTuple

import jax
import jax.numpy as jnp
from jax import lax
from jax.experimental import pallas as pl
from jax.experimental.pallas import tpu as pltpu

F32 = jnp.float32
BF16 = jnp.bfloat16

D_MODEL = 1024
D_FF = 2816
N_SUBLAYERS = 3
N_HEADS = 4
DK = 128
DV = 256
QK_WIDTH = N_HEADS * DK
V_WIDTH = N_HEADS * DV
GATE_RANK = 16
GATE_TAU = 16.0
D_CONV = D_MODEL
CONV_WIDTH = 3
RMS_EPS = 1e-6
MACARON_WEIGHT = 0.5

LANES = 128
SUBLANES = 8
GLA_CHUNK = 64
VMEM_LIMIT_BYTES = 56 * 1024 * 1024

FFN_TM = 1024
FFN_PART = 256
MIX_TM = 512
MIX_PART = 256
ADA_TN = 3072
SAMPLE_BLOCK = 16

_A_WIDTH = 2 * QK_WIDTH + 2 * V_WIDTH
_B_WIDTH = 3 * D_CONV + 2 * D_MODEL


def _rms(x, g):
    return x * lax.rsqrt(jnp.mean(x * x, axis=-1, keepdims=True) + RMS_EPS) * g


def _silu(x):
    return x * jax.nn.sigmoid(x)


def _dot(a, b):
    return jnp.dot(a, b, preferred_element_type=F32)


def _dot_nt(a, b):
    return lax.dot_general(a, b, (((1,), (1,)), ((), ())), preferred_element_type=F32)


def _dot_tn(a, b):
    return lax.dot_general(a, b, (((0,), (0,)), ((), ())), preferred_element_type=F32)


def _resident(shape):
    zeros = (0,) * len(shape)
    return pl.BlockSpec(shape, lambda *_: zeros, pipeline_mode=pl.Buffered(1))


def _params(semantics):
    return pltpu.CompilerParams(dimension_semantics=semantics, vmem_limit_bytes=VMEM_LIMIT_BYTES)


def _ada_kernel(c_ref, w_ref, b_ref, o_ref):
    c = c_ref[...]
    o_ref[...] = _dot(_silu(c).astype(BF16), w_ref[...].astype(BF16)) + b_ref[...]


def _ada(c_all, w_ada, b_ada):
    rows = c_all.shape[0]
    width = w_ada.shape[1]
    return pl.pallas_call(
        _ada_kernel,
        grid=(width // ADA_TN,),
        in_specs=[
            pl.BlockSpec((rows, D_MODEL), lambda j: (0, 0)),
            pl.BlockSpec((D_MODEL, ADA_TN), lambda j: (0, j)),
            pl.BlockSpec((1, ADA_TN), lambda j: (0, j)),
        ],
        out_specs=pl.BlockSpec((rows, ADA_TN), lambda j: (0, j)),
        out_shape=jax.ShapeDtypeStruct((rows, width), F32),
        compiler_params=_params(("arbitrary",)),
        name="ada",
    )(c_all, w_ada, b_ada.reshape(1, width))


def _mod_specs(sub, per_token, prompt_row_block):
    specs = []
    for j in range(3):
        col = sub * 3 + j
        if per_token:
            specs.append(pl.BlockSpec((LANES, D_MODEL), lambda *_, col=col: (0, col)))
        else:
            specs.append(pl.BlockSpec((SUBLANES, D_MODEL),
                                      lambda *_, col=col: (prompt_row_block, col)))
    return specs


def _load_mod(ref, row):
    if row is None:
        return ref[...]
    return ref[pl.ds(row, 1), :]


BF16_SUBLANES = 16


class _CastJob(NamedTuple):
    src: jax.Array
    block_rows: int
    n_blocks: int
    transpose: bool = False
    row0: int = 0
    valid: int = 0

    def specs(self, step_of):
        cols = self.src.shape[1]
        r, nb = self.block_rows, self.n_blocks
        block = lambda *g: jnp.minimum(step_of(*g), nb - 1)
        if not self.transpose:
            spec = pl.BlockSpec((r, cols), lambda *g: (block(*g), 0))
            return spec, spec, jax.ShapeDtypeStruct((nb * r, cols), BF16)
        assert self.row0 % SUBLANES == 0 and r % SUBLANES == 0
        return (pl.BlockSpec((pl.Element(r), pl.Element(cols)),
                             lambda *g: (pl.multiple_of(self.row0 + r * block(*g), SUBLANES), 0)),
                pl.BlockSpec((cols, r), lambda *g: (0, block(*g))),
                jax.ShapeDtypeStruct((cols, nb * r), BF16))

    @property
    def static(self):
        return (self.transpose, self.valid or self.block_rows)


def _whole_cast(w, n_steps):
    rows = w.shape[0]
    r = next(r for r in range(BF16_SUBLANES, rows + 1, BF16_SUBLANES)
             if rows % r == 0 and rows // r <= n_steps)
    return _CastJob(w, r, rows // r)


def _cast_plumbing(jobs, step_of):
    specs = [job.specs(step_of) for job in jobs]
    return ([i for i, _, _ in specs], [o for _, o, _ in specs], [s for _, _, s in specs])


def _run_casts(statics, src_refs, dst_refs):
    for (transpose, valid), src, dst in zip(statics, src_refs, dst_refs, strict=True):
        blk = src[...]
        if transpose:
            blk = blk.T
            if valid < blk.shape[1]:
                lane = lax.broadcasted_iota(jnp.int32, blk.shape, 1)
                blk = jnp.where(lane < valid, blk, 0.0)
        dst[...] = blk.astype(BF16)


def _ffn_rows(x_ref, o_ref, mods, per_token, gpre_ref, gpost_ref, win_ref, wout_ref):
    shift, scale, gate = mods
    tm = x_ref.shape[1]
    parts = max(tm // FFN_PART, 1)
    rows = [slice(i * (tm // parts), (i + 1) * (tm // parts)) for i in range(parts)]
    mod = lambda m, rs: m[rs] if per_token else m

    def pre(rs):
        return (_rms(x_ref[0, rs, :], gpre_ref[...]) * (1.0 + mod(scale, rs))
                + mod(shift, rs)).astype(BF16)

    def post(rs, out):
        o_ref[0, rs, :] = (x_ref[0, rs, :]
                           + MACARON_WEIGHT * mod(gate, rs) * _rms(out, gpost_ref[...]))

    h = pre(rows[0])
    for i in range(parts):
        g = _dot(h, win_ref[:, :D_FF])
        h_next = pre(rows[i + 1]) if i + 1 < parts else None
        u = _dot(h, win_ref[:, D_FF:])
        a = (_silu(g) * u).astype(BF16)
        if i > 0:
            post(rows[i - 1], out)
        out = _dot(a, wout_ref[...])
        h = h_next
    post(rows[parts - 1], out)


def _ffn_kernel(x_ref, sh_ref, sc_ref, gt_ref, xs_ref, shs_ref, scs_ref, gts_ref, gpre_ref,
                gpost_ref, win_ref, wout_ref, *rest, n_l, job_groups):
    n_jobs = len(job_groups)
    o_ref, os_ref = rest[n_jobs:n_jobs + 2]
    _run_casts(job_groups, rest[:n_jobs], rest[n_jobs + 2:])
    step = pl.program_id(0)
    last = pl.num_programs(0) - 1
    weights = (gpre_ref, gpost_ref, win_ref, wout_ref)

    @pl.when(step < last)
    def _():
        mods = [_load_mod(r, step // n_l) for r in (sh_ref, sc_ref, gt_ref)]
        _ffn_rows(x_ref, o_ref, mods, False, *weights)

    @pl.when(step == last)
    def _():
        mods = [_load_mod(r, None) for r in (shs_ref, scs_ref, gts_ref)]
        _ffn_rows(xs_ref, os_ref, mods, True, *weights)


def _ffn(x, xs, ada, sub, g_pre, g_post, w_in, w_out, *, prompt_row_block, jobs=()):
    n, l, _ = x.shape
    n_l = l // FFN_TM
    n_tiles = n * n_l
    tile = lambda i: (jnp.minimum(i, n_tiles - 1) // n_l, jnp.minimum(i, n_tiles - 1) % n_l, 0)
    whole = lambda i: (0, 0, 0)
    kern = functools.partial(_ffn_kernel, n_l=n_l, job_groups=tuple(j.static for j in jobs))
    cast_in, cast_out, cast_shapes = _cast_plumbing(jobs, lambda i: i)
    return pl.pallas_call(
        kern,
        grid=(n_tiles + 1,),
        in_specs=[pl.BlockSpec((1, FFN_TM, D_MODEL), tile)]
        + _mod_specs(sub, False, prompt_row_block)
        + [pl.BlockSpec(xs.shape, whole)]
        + _mod_specs(sub, True, prompt_row_block)
        + [_resident((1, D_MODEL)), _resident((1, D_MODEL)),
           _resident((D_MODEL, 2 * D_FF)), _resident((D_FF, D_MODEL))]
        + cast_in,
        out_specs=[pl.BlockSpec((1, FFN_TM, D_MODEL), tile), pl.BlockSpec(xs.shape, whole)]
        + cast_out,
        out_shape=[jax.ShapeDtypeStruct(x.shape, F32), jax.ShapeDtypeStruct(xs.shape, F32)]
        + cast_shapes,
        compiler_params=_params(("arbitrary",)),
        name="ffn",
    )(x, ada, ada, ada, xs, ada, ada, ada, g_pre, g_post, w_in, w_out, *[j.src for j in jobs])


def _log_decay(hb, wz_ref, walpha_ref, balpha_ref):
    z = _dot(hb, wz_ref[...]).astype(BF16)
    xg = _dot(z, walpha_ref[...]) + balpha_ref[...]
    return jax.nn.log_sigmoid(xg) * (1.0 / GATE_TAU)


def _merge_and_project(y_gla, y_conv, sig_a, sig_b, wbo_ref, wmo_ref):
    pg = _dot(y_gla, wbo_ref[0])
    pc = _dot(y_conv, wbo_ref[1])
    merged = (sig_a * pg + sig_b * pc).astype(BF16)
    return _dot(merged, wmo_ref[...])


_PAIR_LEVELS = tuple(GLA_CHUNK >> i for i in range(1, GLA_CHUNK.bit_length()))


def _pair_operands(qs, k, b, log_a, m, tm):
    if m >= SUBLANES:
        halves = lambda a: a.reshape(tm // (2 * m), 2, m, QK_WIDTH)
        b4, k4, q4 = halves(b), halves(k), halves(qs)
        mid = b4[:, 0:1, m - 1:m, :]
        w = jnp.exp(jnp.concatenate([mid - b4[:, 0:1], b4[:, 1:2] - mid], axis=1))
        src = jnp.concatenate([k4[:, 0:1], q4[:, 1:2]], axis=1)
        return (src * w).astype(BF16).reshape(tm, QK_WIDTH)
    tiles = lambda a: a.reshape(tm // SUBLANES, SUBLANES, QK_WIDTH)
    sub = lax.broadcasted_iota(jnp.int32, (1, SUBLANES, QK_WIDTH), 1)
    second = (sub & m) != 0
    if m == 1:
        w = jnp.exp(jnp.where(second, tiles(log_a), 0.0))
    else:
        b4 = tiles(b)
        mid = b4[:, m - 1:m, :]
        for start in range(2 * m, SUBLANES, 2 * m):
            mid = jnp.where(sub >= start, b4[:, start + m - 1:start + m, :], mid)
        w = jnp.exp((b4 - mid) * jnp.where(second, 1.0, -1.0))
    return (jnp.where(second, tiles(qs), tiles(k)) * w).astype(BF16).reshape(tm, QK_WIDTH)


def _pair_mask(m):
    t = lax.broadcasted_iota(jnp.int32, (GLA_CHUNK, GLA_CHUNK), 0)
    s = lax.broadcasted_iota(jnp.int32, (GLA_CHUNK, GLA_CHUNK), 1)
    return ((t & m) != 0) & ((s & m) == 0) & ((t // (2 * m)) == (s // (2 * m)))


def _mixer_kernel(x_ref, sh_ref, sc_ref, gt_ref, gpre_ref, gpost_ref, wa_ref, wz_ref, wb_ref,
                  walpha_ref, balpha_ref, ggla_ref, wconv_ref, wbo_ref, wmo_ref, *rest,
                  job_groups):
    n_jobs = len(job_groups)
    y_ref, sgla_ref, sconv_ref = rest[n_jobs:n_jobs + 3]
    st_scr, u_scr, ygla_scr = rest[-3:]
    _run_casts(job_groups, rest[:n_jobs], rest[n_jobs + 3:-3])
    l = pl.program_id(1)
    n_l = pl.num_programs(1)

    @pl.when(l == 0)
    def _():
        st_scr[...] = jnp.zeros_like(st_scr)
        u_scr[0:SUBLANES, :] = jnp.zeros((SUBLANES, D_CONV), F32)

    mods = [_load_mod(r, pl.program_id(0)) for r in (sh_ref, sc_ref, gt_ref)]
    weights = (gpre_ref, gpost_ref, wa_ref, wz_ref, wb_ref, walpha_ref, balpha_ref, ggla_ref,
               wconv_ref, wbo_ref, wmo_ref)
    n_parts = x_ref.shape[1] // MIX_PART
    parts = [_mixer_part(x_ref, y_ref, mods, weights, (st_scr, u_scr, ygla_scr),
                         slice(i * MIX_PART, (i + 1) * MIX_PART)) for i in range(n_parts)]
    next(parts[0])
    for i, p in enumerate(parts):
        next(p)
        if i + 1 < n_parts:
            next(parts[i + 1])
        if i > 0:
            next(parts[i - 1], None)
        next(p)
    next(parts[-1], None)

    @pl.when(l == n_l - 1)
    def _():
        sgla_ref[0, 0] = st_scr[...]
        sconv_ref[0, 0] = u_scr[SUBLANES - 2:SUBLANES, :]


def _mixer_part(x_ref, y_ref, mods, weights, scratch, prow):
    shift, scale, gate = mods
    (gpre_ref, gpost_ref, wa_ref, wz_ref, wb_ref, walpha_ref, balpha_ref, ggla_ref, wconv_ref,
     wbo_ref, wmo_ref) = weights
    st_scr, u_scr, ygla_scr = scratch
    ygla_scr = ygla_scr.at[prow]
    tm = MIX_PART
    x = x_ref[0, prow, :]
    hb = (_rms(x, gpre_ref[...]) * (1.0 + scale) + shift).astype(BF16)
    yield

    def proj_b(j):
        return _dot(hb, wb_ref[:, j * D_MODEL:(j + 1) * D_MODEL])

    nc = tm // GLA_CHUNK
    log_a = _log_decay(hb, wz_ref, walpha_ref, balpha_ref)
    pa = _dot(hb, wa_ref[...])
    qs = pa[:, 0:QK_WIDTH] * (DK ** -0.5)
    k = pa[:, QK_WIDTH:2 * QK_WIDTH]
    vb = pa[:, 2 * QK_WIDTH:2 * QK_WIDTH + V_WIDTH].astype(BF16)
    r = pa[:, 2 * QK_WIDTH + V_WIDTH:]

    row = lax.broadcasted_iota(jnp.int32, (tm, tm), 0)
    col = lax.broadcasted_iota(jnp.int32, (tm, tm), 1)
    tri = jnp.where((row >= col) & (row // GLA_CHUNK == col // GLA_CHUNK), 1.0, 0.0).astype(BF16)
    la_hi = log_a.astype(BF16)
    rem = log_a - la_hi.astype(F32)
    la_mid = rem.astype(BF16)
    la_lo = (rem - la_mid.astype(F32)).astype(BF16)
    b3 = (_dot(tri, la_hi) + _dot(tri, la_mid) + _dot(tri, la_lo)).reshape(
        nc, GLA_CHUNK, QK_WIDTH)

    u = proj_b(1) * proj_b(2)
    u_scr[SUBLANES:SUBLANES + tm, :] = u
    yield

    b_last = b3[:, GLA_CHUNK - 1:GLA_CHUNK, :]
    b = b3.reshape(tm, QK_WIDTH)
    qd = (qs * jnp.exp(b)).astype(BF16).reshape(nc, GLA_CHUNK, QK_WIDTH)
    kl = (k.reshape(nc, GLA_CHUNK, QK_WIDTH) * jnp.exp(b_last - b3)).astype(BF16)
    b_last_rows = jnp.concatenate(
        [b_last[c, :, hd * DK:(hd + 1) * DK] for c in range(nc) for hd in range(N_HEADS)]
        + [jnp.zeros((LANES - nc * N_HEADS, DK), F32)], axis=0)
    decay_cols = jnp.exp(b_last_rows.T)

    pair_ops = [_pair_operands(qs, k, b, log_a, m, tm) for m in _PAIR_LEVELS]
    pair_masks = [_pair_mask(m) for m in _PAIR_LEVELS]
    qs_b, k_b = qs.astype(BF16), k.astype(BF16)
    crow = lax.broadcasted_iota(jnp.int32, (GLA_CHUNK, GLA_CHUNK), 0)
    ccol = lax.broadcasted_iota(jnp.int32, (GLA_CHUNK, GLA_CHUNK), 1)
    scores, upd = {}, {}
    for c in range(nc):
        rows = slice(c * GLA_CHUNK, (c + 1) * GLA_CHUNK)
        for hd in range(N_HEADS):
            kc = slice(hd * DK, (hd + 1) * DK)
            s = jnp.where(crow == ccol, _dot_nt(qs_b[rows, kc], k_b[rows, kc]), 0.0)
            for p, mask in zip(pair_ops, pair_masks):
                s = jnp.where(mask, _dot_nt(p[rows, kc], p[rows, kc]), s)
            scores[c, hd] = s.astype(BF16)
            upd[c, hd] = _dot_tn(kl[c, :, kc], vb[rows, hd * DV:(hd + 1) * DV])

    conv = (wconv_ref[0:1, :] * u_scr[SUBLANES - 2:SUBLANES - 2 + tm, :]
            + wconv_ref[1:2, :] * u_scr[SUBLANES - 1:SUBLANES - 1 + tm, :]
            + wconv_ref[2:3, :] * u)
    u_scr[0:SUBLANES, :] = u_scr[tm:tm + SUBLANES, :]
    y_conv = (proj_b(0) * conv).astype(BF16)

    outs = {}
    for hd in range(N_HEADS):
        kc = slice(hd * DK, (hd + 1) * DK)
        vc = slice(hd * DV, (hd + 1) * DV)
        st = st_scr[hd]
        for c in range(nc):
            rows = slice(c * GLA_CHUNK, (c + 1) * GLA_CHUNK)
            lhs = jnp.concatenate([qd[c, :, kc], scores[c, hd]], axis=1)
            rhs = jnp.concatenate([st.astype(BF16), vb[rows, vc]], axis=0)
            outs[c, hd] = _dot(lhs, rhs)
            j = c * N_HEADS + hd
            st = st * decay_cols[:, j:j + 1] + upd[c, hd]
        st_scr[hd] = st

    sig_a = jax.nn.sigmoid(proj_b(3))
    sig_b = jax.nn.sigmoid(proj_b(4))

    for hd in range(N_HEADS):
        vc = slice(hd * DV, (hd + 1) * DV)
        for c in range(nc):
            rows = slice(c * GLA_CHUNK, (c + 1) * GLA_CHUNK)
            on = _rms(outs[c, hd], ggla_ref[:, vc])
            ygla_scr[rows, vc] = (on * _silu(r[rows, vc])).astype(BF16)

    yield
    mix = _merge_and_project(ygla_scr[...], y_conv, sig_a, sig_b, wbo_ref, wmo_ref)
    y_ref[0, prow, :] = x + gate * _rms(mix, gpost_ref[...])


def _mixer(x, ada, g_pre, g_post, wts, *, prompt_row_block, jobs=()):
    n, l, _ = x.shape
    tm = MIX_TM
    n_l = l // tm
    kern = functools.partial(_mixer_kernel, job_groups=tuple(j.static for j in jobs))
    cast_in, cast_out, cast_shapes = _cast_plumbing(jobs, lambda n, l: n * n_l + l)
    wa, wz, wb, walpha, balpha, ggla, wconv, wbo, wmo = wts
    return pl.pallas_call(
        kern,
        grid=(n, n_l),
        in_specs=[pl.BlockSpec((1, tm, D_MODEL), lambda n, l: (n, l, 0))]
        + _mod_specs(1, False, prompt_row_block)
        + [_resident(a.shape) for a in (g_pre, g_post, wa, wz, wb, walpha, balpha, ggla, wconv,
                                        wbo, wmo)]
        + cast_in,
        out_specs=[
            pl.BlockSpec((1, tm, D_MODEL), lambda n, l: (n, l, 0)),
            pl.BlockSpec((1, 1, N_HEADS, DK, DV), lambda n, l: (0, n, 0, 0, 0)),
            pl.BlockSpec((1, 1, CONV_WIDTH - 1, D_CONV), lambda n, l: (0, n, 0, 0)),
        ] + cast_out,
        out_shape=[
            jax.ShapeDtypeStruct(x.shape, F32),
            jax.ShapeDtypeStruct((1, n, N_HEADS, DK, DV), F32),
            jax.ShapeDtypeStruct((1, n, CONV_WIDTH - 1, D_CONV), F32),
        ] + cast_shapes,
        scratch_shapes=[
            pltpu.VMEM((N_HEADS, DK, DV), F32),
            pltpu.VMEM((MIX_PART + SUBLANES, D_CONV), F32),
            pltpu.VMEM((tm, V_WIDTH), BF16),
        ],
        compiler_params=_params(("arbitrary", "arbitrary")),
        name="mixer",
    )(x, ada, ada, ada, g_pre, g_post, wa, wz, wb, walpha, balpha, ggla, wconv, wbo, wmo,
      *[j.src for j in jobs])


def _smix_pre_kernel(x_ref, sh_ref, sc_ref, gpre_ref, wa_ref, wz_ref, wb_ref, walpha_ref,
                     balpha_ref, wconv_ref, sconv_ref,
                     qt_ref, kt_ref, at_ref, v_ref, sr_ref, yconv_ref, sga_ref, sgb_ref,
                     sconv_new_ref):
    x = x_ref[...]
    hb = (_rms(x, gpre_ref[...]) * (1.0 + sc_ref[...]) + sh_ref[...]).astype(BF16)
    pa = _dot(hb, wa_ref[...])
    q = pa[:, 0:QK_WIDTH] * (DK ** -0.5)
    k = pa[:, QK_WIDTH:2 * QK_WIDTH]
    v_ref[...] = pa[:, 2 * QK_WIDTH:2 * QK_WIDTH + V_WIDTH]
    sr_ref[...] = _silu(pa[:, 2 * QK_WIDTH + V_WIDTH:])
    a = jnp.exp(_log_decay(hb, wz_ref, walpha_ref, balpha_ref))
    for hd in range(N_HEADS):
        kc = slice(hd * DK, (hd + 1) * DK)
        qt_ref[hd] = q[:, kc].T
        kt_ref[hd] = k[:, kc].T
        at_ref[hd] = a[:, kc].T

    pb = _dot(hb, wb_ref[...])
    u = pb[:, D_CONV:2 * D_CONV] * pb[:, 2 * D_CONV:3 * D_CONV]
    conv = (wconv_ref[0:1, :] * sconv_ref[0] + wconv_ref[1:2, :] * sconv_ref[1]
            + wconv_ref[2:3, :] * u)
    yconv_ref[...] = pb[:, 0:D_CONV] * conv
    sga_ref[...] = jax.nn.sigmoid(pb[:, 3 * D_CONV:3 * D_CONV + D_MODEL])
    sgb_ref[...] = jax.nn.sigmoid(pb[:, 3 * D_CONV + D_MODEL:])
    sconv_new_ref[0] = sconv_ref[1]
    sconv_new_ref[1] = u


def _smix_state_kernel(s_ref, qt_ref, kt_ref, at_ref, v_ref, snew_ref, o_ref):
    base = pl.program_id(0) * SAMPLE_BLOCK
    lane = lax.broadcasted_iota(jnp.int32, (1, LANES), 1)
    for i in range(SAMPLE_BLOCK):
        pick = jnp.where(lane == base + i, 1.0, 0.0)
        for hd in range(N_HEADS):
            vc = slice(hd * DV, (hd + 1) * DV)
            a_col = jnp.sum(at_ref[hd] * pick, axis=1, keepdims=True)
            k_col = jnp.sum(kt_ref[hd] * pick, axis=1, keepdims=True)
            q_col = jnp.sum(qt_ref[hd] * pick, axis=1, keepdims=True)
            s_new = a_col * s_ref[0, i, hd] + k_col * v_ref[i:i + 1, vc]
            snew_ref[0, i, hd] = s_new
            o_ref[i:i + 1, vc] = jnp.sum(q_col * s_new, axis=0, keepdims=True)


def _smix_post_kernel(x_ref, gt_ref, gpost_ref, o_ref, sr_ref, yconv_ref, sga_ref, sgb_ref,
                      ggla_ref, wbo_ref, wmo_ref, y_ref, ygla_scr):
    for hd in range(N_HEADS):
        vc = slice(hd * DV, (hd + 1) * DV)
        on = _rms(o_ref[:, vc], ggla_ref[:, vc])
        ygla_scr[:, vc] = (on * sr_ref[:, vc]).astype(BF16)
    mix = _merge_and_project(ygla_scr[...], yconv_ref[...].astype(BF16), sga_ref[...],
                             sgb_ref[...], wbo_ref, wmo_ref)
    y_ref[...] = x_ref[...] + gt_ref[...] * _rms(mix, gpost_ref[...])


def _sample_mixer(x, ada, state_gla, sconv_t, g_pre, g_post, wts):
    s = x.shape[0]
    wa, wz, wb, walpha, balpha, ggla, wconv, wbo, wmo = wts
    whole = lambda a: pl.BlockSpec(a.shape, lambda *_: (0,) * a.ndim)
    mod = lambda j: pl.BlockSpec((s, D_MODEL), lambda *_: (0, 3 + j))
    tok = jax.ShapeDtypeStruct((s, D_MODEL), F32)
    tr = jax.ShapeDtypeStruct((N_HEADS, DK, s), F32)

    pre_in = (x, ada, ada, g_pre, wa, wz, wb, walpha, balpha, wconv, sconv_t)
    pre_specs = [whole(x), mod(0), mod(1)] + [whole(a) for a in pre_in[3:]]
    pre_out = [tr, tr, tr, tok, tok, tok, tok, tok, jax.ShapeDtypeStruct(sconv_t.shape, F32)]
    qt, kt, at, v, sr, yconv, sga, sgb, sconv_new = pl.pallas_call(
        _smix_pre_kernel,
        grid=(1,),
        in_specs=pre_specs,
        out_specs=[whole(o) for o in pre_out],
        out_shape=pre_out,
        compiler_params=_params(("arbitrary",)),
        name="smix_pre",
    )(*pre_in)

    state_block = pl.BlockSpec((1, SAMPLE_BLOCK, N_HEADS, DK, DV), lambda j: (0, j, 0, 0, 0))
    snew, o = pl.pallas_call(
        _smix_state_kernel,
        grid=(s // SAMPLE_BLOCK,),
        in_specs=[state_block, whole(qt), whole(kt), whole(at),
                  pl.BlockSpec((SAMPLE_BLOCK, V_WIDTH), lambda j: (j, 0))],
        out_specs=[state_block, pl.BlockSpec((SAMPLE_BLOCK, V_WIDTH), lambda j: (j, 0))],
        out_shape=[jax.ShapeDtypeStruct(state_gla.shape, F32), tok],
        compiler_params=_params(("arbitrary",)),
        name="smix_state",
    )(state_gla, qt, kt, at, v)

    post_in = (x, ada, g_post, o, sr, yconv, sga, sgb, ggla, wbo, wmo)
    post_specs = [whole(x), mod(2)] + [whole(a) for a in post_in[2:]]
    y = pl.pallas_call(
        _smix_post_kernel,
        grid=(1,),
        in_specs=post_specs,
        out_specs=whole(tok),
        out_shape=tok,
        scratch_shapes=[pltpu.VMEM((s, V_WIDTH), BF16)],
        compiler_params=_params(("arbitrary",)),
        name="smix_post",
    )(*post_in)
    return y, snew, sconv_new


def kernel(x_prompt, x_sample, state_gla, state_conv, c_prompt, c_sample, w_ada, b_ada, g_pre,
           g_post, w_ffn1_in, w_ffn1_out, w_ffn2_in, w_ffn2_out, w_mix_in, w_alpha, b_alpha,
           g_gla_norm, w_conv, w_branch_out, w_mix_out):
    depth = w_ada.shape[0]
    n_s = x_sample.shape[0]
    assert depth == 1 and x_sample.shape[1] == 1 and n_s == LANES
    assert c_prompt.shape[0] == SUBLANES and n_s % SUBLANES == 0
    prompt_row_block = n_s // SUBLANES

    yp = x_prompt
    ys = x_sample.reshape(1, n_s, D_MODEL)
    c_all = jnp.concatenate([c_sample, c_prompt], axis=0)
    outs = []
    for i in range(depth):
        ada = _ada(c_all, w_ada[i], b_ada[i])
        gpre = [g_pre[i, j].reshape(1, D_MODEL) for j in range(N_SUBLAYERS)]
        gpost = [g_post[i, j].reshape(1, D_MODEL) for j in range(N_SUBLAYERS)]
        ffn = functools.partial(_ffn, prompt_row_block=prompt_row_block)

        ffn1_w = (w_ffn1_in[i].astype(BF16), w_ffn1_out[i].astype(BF16))
        n_ffn_steps = yp.shape[0] * (yp.shape[1] // FFN_TM)
        n_mix_steps = yp.shape[0] * (yp.shape[1] // MIX_TM)
        w_mix_t = jnp.swapaxes(w_mix_in[i], 0, 1)
        mixer_jobs = (
            _CastJob(w_mix_t, 2 * LANES, _A_WIDTH // (2 * LANES), transpose=True),
            _CastJob(w_mix_t, LANES, 1, transpose=True, row0=_A_WIDTH, valid=GATE_RANK),
            _CastJob(w_mix_t, 4 * LANES, _B_WIDTH // (4 * LANES), transpose=True,
                     row0=_A_WIDTH + GATE_RANK),
            _whole_cast(w_branch_out[i].reshape(2 * D_MODEL, D_MODEL), n_ffn_steps),
            _whole_cast(w_mix_out[i], n_ffn_steps))
        assert all(j.n_blocks <= n_ffn_steps for j in mixer_jobs)
        yp, ys, wa, wz, wb, wbo, wmo = ffn(yp, ys, ada, 0, gpre[0], gpost[0], *ffn1_w,
                                           jobs=mixer_jobs)
        walpha = jnp.pad(w_alpha[i], ((0, LANES - GATE_RANK), (0, 0))).astype(BF16)
        mix_w = (wa, wz, wb, walpha, b_alpha[i].reshape(1, QK_WIDTH),
                 g_gla_norm[i].reshape(1, V_WIDTH),
                 jnp.pad(w_conv[i], ((0, SUBLANES - CONV_WIDTH), (0, 0))),
                 wbo.reshape(2, D_MODEL, D_MODEL), wmo)
        sconv_t = jnp.swapaxes(state_conv[i], 0, 1)
        ys, gla_s, sconv_new = _sample_mixer(ys[0], ada, state_gla[i:i + 1], sconv_t,
                                             gpre[1], gpost[1], mix_w)
        yp, gla_p, conv_p, *ffn2_w = _mixer(
            yp, ada, gpre[1], gpost[1], mix_w, prompt_row_block=prompt_row_block,
            jobs=(_whole_cast(w_ffn2_in[i], n_mix_steps), _whole_cast(w_ffn2_out[i], n_mix_steps)))
        yp, ys = ffn(yp, ys.reshape(1, n_s, D_MODEL), ada, 2, gpre[2], gpost[2], *ffn2_w)
        outs.append((gla_p, conv_p, gla_s, jnp.swapaxes(sconv_new, 0, 1)[None]))

    gla_p, conv_p, gla_s, conv_s = outs[0]
    return (yp, ys.reshape(n_s, 1, D_MODEL), gla_p, conv_p, gla_s, conv_s)
```

```python
import functools
from typing import NamedTuple

import jax
import jax.numpy as jnp
from jax import lax
from jax.experimental import pallas as pl
from jax.experimental.pallas import tpu as pltpu

F32 = jnp.float32
BF16 = jnp.bfloat16

D_MODEL = 1024
D_FF = 2816
N_SUBLAYERS = 3
N_HEADS = 4
DK = 128
DV = 256
QK_WIDTH = N_HEADS * DK
V_WIDTH = N_HEADS * DV
GATE_RANK = 16
GATE_TAU = 16.0
D_CONV = D_MODEL
CONV_WIDTH = 3
RMS_EPS = 1e-6
MACARON_WEIGHT = 0.5

LANES = 128
SUBLANES = 8
GLA_CHUNK = 64
VMEM_LIMIT_BYTES = 56 * 1024 * 1024

FFN_TM = 1024
FFN_PART = 256
MIX_TM = 512
MIX_PART = 256
ADA_TN = 1536
SAMPLE_BLOCK = 16

_A_WIDTH = 2 * QK_WIDTH + 2 * V_WIDTH
_B_WIDTH = 3 * D_CONV + 2 * D_MODEL


def _rms(x, g):
    return x * lax.rsqrt(jnp.mean(x * x, axis=-1, keepdims=True) + RMS_EPS) * g


def _silu(x):
    return x * jax.nn.sigmoid(x)


def _dot(a, b):
    return jnp.dot(a, b, preferred_element_type=F32)


def _dot_nt(a, b):
    return lax.dot_general(a, b, (((1,), (1,)), ((), ())), preferred_element_type=F32)


def _dot_tn(a, b):
    return lax.dot_general(a, b, (((0,), (0,)), ((), ())), preferred_element_type=F32)


def _resident(shape):
    zeros = (0,) * len(shape)
    return pl.BlockSpec(shape, lambda *_: zeros, pipeline_mode=pl.Buffered(1))


def _params(semantics):
    return pltpu.CompilerParams(dimension_semantics=semantics, vmem_limit_bytes=VMEM_LIMIT_BYTES)


def _ada_kernel(cs_ref, cp_ref, w_ref, b_ref, os_ref, op_ref):
    w = w_ref[...].astype(BF16)
    for c_ref, o_ref in ((cs_ref, os_ref), (cp_ref, op_ref)):
        o_ref[...] = _dot(_silu(c_ref[...]).astype(BF16), w) + b_ref[...]


def _ada(c_sample, c_prompt, w_ada, b_ada):
    width = w_ada.shape[1]
    rows = lambda c: pl.BlockSpec((c.shape[0], D_MODEL), lambda j: (0, 0))
    cols = lambda c: pl.BlockSpec((c.shape[0], ADA_TN), lambda j: (0, j))
    return pl.pallas_call(
        _ada_kernel,
        grid=(width // ADA_TN,),
        in_specs=[rows(c_sample), rows(c_prompt),
                  pl.BlockSpec((D_MODEL, ADA_TN), lambda j: (0, j)),
                  pl.BlockSpec((1, ADA_TN), lambda j: (0, j))],
        out_specs=[cols(c_sample), cols(c_prompt)],
        out_shape=[jax.ShapeDtypeStruct((c.shape[0], width), F32) for c in (c_sample, c_prompt)],
        compiler_params=_params(("arbitrary",)),
        name="ada",
    )(c_sample, c_prompt, w_ada, b_ada.reshape(1, width))


def _mod_specs(sub, ada):
    return [pl.BlockSpec((ada.shape[0], D_MODEL), lambda *_, col=sub * 3 + j: (0, col))
            for j in range(3)]


def _load_mod(ref, row):
    if row is None:
        return ref[...]
    return ref[pl.ds(row, 1), :]


BF16_SUBLANES = 16


class _CastJob(NamedTuple):
    src: jax.Array
    block_rows: int
    n_blocks: int
    transpose: bool = False
    row0: int = 0
    valid: int = 0

    def specs(self, step_of):
        cols = self.src.shape[1]
        r, nb = self.block_rows, self.n_blocks
        block = lambda *g: jnp.minimum(step_of(*g), nb - 1)
        if not self.transpose:
            spec = pl.BlockSpec((r, cols), lambda *g: (block(*g), 0))
            return spec, spec, jax.ShapeDtypeStruct((nb * r, cols), BF16)
        assert self.row0 % SUBLANES == 0 and r % SUBLANES == 0
        return (pl.BlockSpec((pl.Element(r), pl.Element(cols)),
                             lambda *g: (pl.multiple_of(self.row0 + r * block(*g), SUBLANES), 0)),
                pl.BlockSpec((cols, r), lambda *g: (0, block(*g))),
                jax.ShapeDtypeStruct((cols, nb * r), BF16))

    @property
    def static(self):
        return (self.transpose, self.valid or self.block_rows)


def _whole_cast(w, n_steps):
    rows = w.shape[0]
    r = next(r for r in range(BF16_SUBLANES, rows + 1, BF16_SUBLANES)
             if rows % r == 0 and rows // r <= n_steps)
    return _CastJob(w, r, rows // r)


def _cast_plumbing(jobs, step_of):
    specs = [job.specs(step_of) for job in jobs]
    return ([i for i, _, _ in specs], [o for _, o, _ in specs], [s for _, _, s in specs])


def _run_casts(statics, src_refs, dst_refs):
    for (transpose, valid), src, dst in zip(statics, src_refs, dst_refs, strict=True):
        blk = src[...]
        if transpose:
            blk = blk.T
            if valid < blk.shape[1]:
                lane = lax.broadcasted_iota(jnp.int32, blk.shape, 1)
                blk = jnp.where(lane < valid, blk, 0.0)
        dst[...] = blk.astype(BF16)


def _ffn_rows(x_ref, o_ref, mods, per_token, g_pre, g_post, win_ref, wout_ref):
    shift, scale, gate = mods
    tm = x_ref.shape[0]
    parts = max(tm // FFN_PART, 1)
    rows = [slice(i * (tm // parts), (i + 1) * (tm // parts)) for i in range(parts)]
    mod = lambda m, rs: m[rs] if per_token else m

    def pre(rs):
        return (_rms(x_ref[rs, :], g_pre) * (1.0 + mod(scale, rs)) + mod(shift, rs)).astype(BF16)

    def post(rs, out):
        o_ref[rs, :] = x_ref[rs, :] + MACARON_WEIGHT * mod(gate, rs) * _rms(out, g_post)

    h = pre(rows[0])
    for i in range(parts):
        g = _dot(h, win_ref[:, :D_FF])
        h_next = pre(rows[i + 1]) if i + 1 < parts else None
        u = _dot(h, win_ref[:, D_FF:])
        a = (_silu(g) * u).astype(BF16)
        if i > 0:
            post(rows[i - 1], out)
        out = _dot(a, wout_ref[...])
        h = h_next
    post(rows[parts - 1], out)


def _ffn_kernel(x_ref, sh_ref, sc_ref, gt_ref, xs_ref, shs_ref, scs_ref, gts_ref, gpre_ref,
                gpost_ref, win_ref, wout_ref, *rest, sub, n_l, job_groups):
    n_jobs = len(job_groups)
    o_ref, os_ref = rest[n_jobs:n_jobs + 2]
    _run_casts(job_groups, rest[:n_jobs], rest[n_jobs + 2:])
    step = pl.program_id(0)
    last = pl.num_programs(0) - 1
    weights = (gpre_ref[sub:sub + 1, :], gpost_ref[sub:sub + 1, :], win_ref, wout_ref)

    @pl.when(step < last)
    def _():
        mods = [_load_mod(r, step // n_l) for r in (sh_ref, sc_ref, gt_ref)]
        _ffn_rows(x_ref, o_ref, mods, False, *weights)

    @pl.when(step == last)
    def _():
        mods = [_load_mod(r, None) for r in (shs_ref, scs_ref, gts_ref)]
        _ffn_rows(xs_ref, os_ref, mods, True, *weights)


def _token_rows_spec(shape):
    if len(shape) == 2:
        return pl.BlockSpec(shape, lambda i: (0, 0))
    return pl.BlockSpec((shape[0], None, shape[2]), lambda i: (0, 0, 0))


def _ffn(x, xs, ada_p, ada_s, sub, g_pre, g_post, w_in, w_out, *, ys_shape, jobs=()):
    n, l, _ = x.shape
    n_l = l // FFN_TM
    n_tiles = n * n_l
    tile = pl.BlockSpec(
        (None, FFN_TM, D_MODEL),
        lambda i: (jnp.minimum(i, n_tiles - 1) // n_l, jnp.minimum(i, n_tiles - 1) % n_l, 0))
    kern = functools.partial(_ffn_kernel, sub=sub, n_l=n_l,
                             job_groups=tuple(j.static for j in jobs))
    cast_in, cast_out, cast_shapes = _cast_plumbing(jobs, lambda i: i)
    return pl.pallas_call(
        kern,
        grid=(n_tiles + 1,),
        in_specs=[tile] + _mod_specs(sub, ada_p) + [_token_rows_spec(xs.shape)]
        + _mod_specs(sub, ada_s)
        + [_resident(g_pre.shape), _resident(g_post.shape),
           _resident((D_MODEL, 2 * D_FF)), _resident((D_FF, D_MODEL))]
        + cast_in,
        out_specs=[tile, _token_rows_spec(ys_shape)] + cast_out,
        out_shape=[jax.ShapeDtypeStruct(x.shape, F32), jax.ShapeDtypeStruct(ys_shape, F32)]
        + cast_shapes,
        compiler_params=_params(("arbitrary",)),
        name="ffn",
    )(x, ada_p, ada_p, ada_p, xs, ada_s, ada_s, ada_s, g_pre, g_post, w_in, w_out,
      *[j.src for j in jobs])


def _log_decay(hb, wz_ref, walpha_ref, balpha_ref):
    z = _dot(hb, wz_ref[...]).astype(BF16)
    xg = _dot(z, walpha_ref[...]) + balpha_ref[...]
    return jax.nn.log_sigmoid(xg) * (1.0 / GATE_TAU)


def _merge_and_project(y_gla, y_conv, sig_a, sig_b, wbo_ref, wmo_ref):
    pg = _dot(y_gla, wbo_ref[0])
    pc = _dot(y_conv, wbo_ref[1])
    merged = (sig_a * pg + sig_b * pc).astype(BF16)
    return _dot(merged, wmo_ref[...])


_PAIR_LEVELS = tuple(GLA_CHUNK >> i for i in range(1, GLA_CHUNK.bit_length()))


def _pair_operands(qs, k, b, log_a, m, tm):
    if m >= SUBLANES:
        halves = lambda a: a.reshape(tm // (2 * m), 2, m, QK_WIDTH)
        b4, k4, q4 = halves(b), halves(k), halves(qs)
        mid = b4[:, 0:1, m - 1:m, :]
        w = jnp.exp(jnp.concatenate([mid - b4[:, 0:1], b4[:, 1:2] - mid], axis=1))
        src = jnp.concatenate([k4[:, 0:1], q4[:, 1:2]], axis=1)
        return (src * w).astype(BF16).reshape(tm, QK_WIDTH)
    tiles = lambda a: a.reshape(tm // SUBLANES, SUBLANES, QK_WIDTH)
    sub = lax.broadcasted_iota(jnp.int32, (1, SUBLANES, QK_WIDTH), 1)
    second = (sub & m) != 0
    if m == 1:
        w = jnp.exp(jnp.where(second, tiles(log_a), 0.0))
    else:
        b4 = tiles(b)
        mid = b4[:, m - 1:m, :]
        for start in range(2 * m, SUBLANES, 2 * m):
            mid = jnp.where(sub >= start, b4[:, start + m - 1:start + m, :], mid)
        w = jnp.exp((b4 - mid) * jnp.where(second, 1.0, -1.0))
    return (jnp.where(second, tiles(qs), tiles(k)) * w).astype(BF16).reshape(tm, QK_WIDTH)


def _pair_mask(m):
    t = lax.broadcasted_iota(jnp.int32, (GLA_CHUNK, GLA_CHUNK), 0)
    s = lax.broadcasted_iota(jnp.int32, (GLA_CHUNK, GLA_CHUNK), 1)
    return ((t & m) != 0) & ((s & m) == 0) & ((t // (2 * m)) == (s // (2 * m)))


def _mixer_kernel(x_ref, sh_ref, sc_ref, gt_ref, gpre_ref, gpost_ref, wa_ref, wz_ref, wb_ref,
                  walpha_ref, balpha_ref, ggla_ref, wconv_ref, wbo_ref, wmo_ref, *rest,
                  job_groups):
    n_jobs = len(job_groups)
    y_ref, sgla_ref, sconv_ref = rest[n_jobs:n_jobs + 3]
    st_scr, u_scr, ygla_scr = rest[-3:]
    _run_casts(job_groups, rest[:n_jobs], rest[n_jobs + 3:-3])
    l = pl.program_id(1)
    n_l = pl.num_programs(1)

    @pl.when(l == 0)
    def _():
        st_scr[...] = jnp.zeros_like(st_scr)
        u_scr[0:SUBLANES, :] = jnp.zeros((SUBLANES, D_CONV), F32)

    mods = [_load_mod(r, pl.program_id(0)) for r in (sh_ref, sc_ref, gt_ref)]
    weights = (gpre_ref.at[1:2], gpost_ref.at[1:2], wa_ref, wz_ref, wb_ref, walpha_ref,
               balpha_ref, ggla_ref, wconv_ref, wbo_ref, wmo_ref)
    n_parts = x_ref.shape[1] // MIX_PART
    parts = [_mixer_part(x_ref, y_ref, mods, weights, (st_scr, u_scr, ygla_scr),
                         slice(i * MIX_PART, (i + 1) * MIX_PART)) for i in range(n_parts)]
    next(parts[0])
    for i, p in enumerate(parts):
        next(p)
        if i + 1 < n_parts:
            next(parts[i + 1])
        if i > 0:
            next(parts[i - 1], None)
        next(p)
    next(parts[-1], None)

    @pl.when(l == n_l - 1)
    def _():
        sgla_ref[0, 0] = st_scr[...]
        sconv_ref[0, 0] = u_scr[SUBLANES - 2:SUBLANES, :]


def _mixer_part(x_ref, y_ref, mods, weights, scratch, prow):
    shift, scale, gate = mods
    (gpre_ref, gpost_ref, wa_ref, wz_ref, wb_ref, walpha_ref, balpha_ref, ggla_ref, wconv_ref,
     wbo_ref, wmo_ref) = weights
    st_scr, u_scr, ygla_scr = scratch
    ygla_scr = ygla_scr.at[prow]
    tm = MIX_PART
    x = x_ref[0, prow, :]
    hb = (_rms(x, gpre_ref[...]) * (1.0 + scale) + shift).astype(BF16)
    yield

    def proj_b(j):
        return _dot(hb, wb_ref[:, j * D_MODEL:(j + 1) * D_MODEL])

    nc = tm // GLA_CHUNK
    log_a = _log_decay(hb, wz_ref, walpha_ref, balpha_ref)
    pa = _dot(hb, wa_ref[...])
    qs = pa[:, 0:QK_WIDTH] * (DK ** -0.5)
    k = pa[:, QK_WIDTH:2 * QK_WIDTH]
    vb = pa[:, 2 * QK_WIDTH:2 * QK_WIDTH + V_WIDTH].astype(BF16)
    r = pa[:, 2 * QK_WIDTH + V_WIDTH:]

    row = lax.broadcasted_iota(jnp.int32, (tm, tm), 0)
    col = lax.broadcasted_iota(jnp.int32, (tm, tm), 1)
    tri = jnp.where((row >= col) & (row // GLA_CHUNK == col // GLA_CHUNK), 1.0, 0.0).astype(BF16)
    la_hi = log_a.astype(BF16)
    rem = log_a - la_hi.astype(F32)
    la_mid = rem.astype(BF16)
    la_lo = (rem - la_mid.astype(F32)).astype(BF16)
    b3 = (_dot(tri, la_hi) + _dot(tri, la_mid) + _dot(tri, la_lo)).reshape(
        nc, GLA_CHUNK, QK_WIDTH)

    u = proj_b(1) * proj_b(2)
    u_scr[SUBLANES:SUBLANES + tm, :] = u
    yield

    b_last = b3[:, GLA_CHUNK - 1:GLA_CHUNK, :]
    b = b3.reshape(tm, QK_WIDTH)
    qd = (qs * jnp.exp(b)).astype(BF16).reshape(nc, GLA_CHUNK, QK_WIDTH)
    kl = (k.reshape(nc, GLA_CHUNK, QK_WIDTH) * jnp.exp(b_last - b3)).astype(BF16)
    b_last_rows = jnp.concatenate(
        [b_last[c, :, hd * DK:(hd + 1) * DK] for c in range(nc) for hd in range(N_HEADS)]
        + [jnp.zeros((LANES - nc * N_HEADS, DK), F32)], axis=0)
    decay_cols = jnp.exp(b_last_rows.T)

    pair_ops = [_pair_operands(qs, k, b, log_a, m, tm) for m in _PAIR_LEVELS]
    pair_masks = [_pair_mask(m) for m in _PAIR_LEVELS]
    qs_b, k_b = qs.astype(BF16), k.astype(BF16)
    crow = lax.broadcasted_iota(jnp.int32, (GLA_CHUNK, GLA_CHUNK), 0)
    ccol = lax.broadcasted_iota(jnp.int32, (GLA_CHUNK, GLA_CHUNK), 1)
    scores, upd = {}, {}
    for c in range(nc):
        rows = slice(c * GLA_CHUNK, (c + 1) * GLA_CHUNK)
        for hd in range(N_HEADS):
            kc = slice(hd * DK, (hd + 1) * DK)
            s = jnp.where(crow == ccol, _dot_nt(qs_b[rows, kc], k_b[rows, kc]), 0.0)
            for p, mask in zip(pair_ops, pair_masks):
                s = jnp.where(mask, _dot_nt(p[rows, kc], p[rows, kc]), s)
            scores[c, hd] = s.astype(BF16)
            upd[c, hd] = _dot_tn(kl[c, :, kc], vb[rows, hd * DV:(hd + 1) * DV])

    conv = (wconv_ref[0:1, :] * u_scr[SUBLANES - 2:SUBLANES - 2 + tm, :]
            + wconv_ref[1:2, :] * u_scr[SUBLANES - 1:SUBLANES - 1 + tm, :]
            + wconv_ref[2:3, :] * u)
    u_scr[0:SUBLANES, :] = u_scr[tm:tm + SUBLANES, :]
    y_conv = (proj_b(0) * conv).astype(BF16)

    outs = {}
    for hd in range(N_HEADS):
        kc = slice(hd * DK, (hd + 1) * DK)
        vc = slice(hd * DV, (hd + 1) * DV)
        st = st_scr[hd]
        for c in range(nc):
            rows = slice(c * GLA_CHUNK, (c + 1) * GLA_CHUNK)
            lhs = jnp.concatenate([qd[c, :, kc], scores[c, hd]], axis=1)
            rhs = jnp.concatenate([st.astype(BF16), vb[rows, vc]], axis=0)
            outs[c, hd] = _dot(lhs, rhs)
            j = c * N_HEADS + hd
            st = st * decay_cols[:, j:j + 1] + upd[c, hd]
        st_scr[hd] = st

    sig_a = jax.nn.sigmoid(proj_b(3))
    sig_b = jax.nn.sigmoid(proj_b(4))

    for hd in range(N_HEADS):
        vc = slice(hd * DV, (hd + 1) * DV)
        for c in range(nc):
            rows = slice(c * GLA_CHUNK, (c + 1) * GLA_CHUNK)
            on = _rms(outs[c, hd], ggla_ref[:, vc])
            ygla_scr[rows, vc] = (on * _silu(r[rows, vc])).astype(BF16)

    yield
    mix = _merge_and_project(ygla_scr[...], y_conv, sig_a, sig_b, wbo_ref, wmo_ref)
    y_ref[0, prow, :] = x + gate * _rms(mix, gpost_ref[...])


def _mixer(x, ada, g_pre, g_post, wts, *, jobs=()):
    n, l, _ = x.shape
    tm = MIX_TM
    n_l = l // tm
    kern = functools.partial(_mixer_kernel, job_groups=tuple(j.static for j in jobs))
    cast_in, cast_out, cast_shapes = _cast_plumbing(jobs, lambda n, l: n * n_l + l)
    wa, wz, wb, walpha, balpha, ggla, wconv, wbo, wmo = wts
    return pl.pallas_call(
        kern,
        grid=(n, n_l),
        in_specs=[pl.BlockSpec((1, tm, D_MODEL), lambda n, l: (n, l, 0))]
        + _mod_specs(1, ada)
        + [_resident(a.shape) for a in (g_pre, g_post, wa, wz, wb, walpha, balpha, ggla, wconv,
                                        wbo, wmo)]
        + cast_in,
        out_specs=[
            pl.BlockSpec((1, tm, D_MODEL), lambda n, l: (n, l, 0)),
            pl.BlockSpec((1, 1, N_HEADS, DK, DV), lambda n, l: (0, n, 0, 0, 0)),
            pl.BlockSpec((1, 1, CONV_WIDTH - 1, D_CONV), lambda n, l: (0, n, 0, 0)),
        ] + cast_out,
        out_shape=[
            jax.ShapeDtypeStruct(x.shape, F32),
            jax.ShapeDtypeStruct((1, n, N_HEADS, DK, DV), F32),
            jax.ShapeDtypeStruct((1, n, CONV_WIDTH - 1, D_CONV), F32),
        ] + cast_shapes,
        scratch_shapes=[
            pltpu.VMEM((N_HEADS, DK, DV), F32),
            pltpu.VMEM((MIX_PART + SUBLANES, D_CONV), F32),
            pltpu.VMEM((tm, V_WIDTH), BF16),
        ],
        compiler_params=_params(("arbitrary", "arbitrary")),
        name="mixer",
    )(x, ada, ada, ada, g_pre, g_post, wa, wz, wb, walpha, balpha, ggla, wconv, wbo, wmo,
      *[j.src for j in jobs])


def _smix_pre_kernel(x_ref, sh_ref, sc_ref, gpre_ref, wa_ref, wz_ref, wb_ref, walpha_ref,
                     balpha_ref, wconv_ref, sconv0_ref, sconv1_ref,
                     qt_ref, kt_ref, at_ref, v_ref, sr_ref, yconv_ref, sga_ref, sgb_ref,
                     sconv0_new_ref, sconv1_new_ref):
    x = x_ref[...]
    hb = (_rms(x, gpre_ref[1:2, :]) * (1.0 + sc_ref[...]) + sh_ref[...]).astype(BF16)
    pa = _dot(hb, wa_ref[...])
    q = pa[:, 0:QK_WIDTH] * (DK ** -0.5)
    k = pa[:, QK_WIDTH:2 * QK_WIDTH]
    v_ref[...] = pa[:, 2 * QK_WIDTH:2 * QK_WIDTH + V_WIDTH]
    sr_ref[...] = _silu(pa[:, 2 * QK_WIDTH + V_WIDTH:])
    a = jnp.exp(_log_decay(hb, wz_ref, walpha_ref, balpha_ref))
    for hd in range(N_HEADS):
        kc = slice(hd * DK, (hd + 1) * DK)
        qt_ref[hd] = q[:, kc].T
        kt_ref[hd] = k[:, kc].T
        at_ref[hd] = a[:, kc].T

    pb = _dot(hb, wb_ref[...])
    u = pb[:, D_CONV:2 * D_CONV] * pb[:, 2 * D_CONV:3 * D_CONV]
    conv = (wconv_ref[0:1, :] * sconv0_ref[...] + wconv_ref[1:2, :] * sconv1_ref[...]
            + wconv_ref[2:3, :] * u)
    yconv_ref[...] = pb[:, 0:D_CONV] * conv
    sga_ref[...] = jax.nn.sigmoid(pb[:, 3 * D_CONV:3 * D_CONV + D_MODEL])
    sgb_ref[...] = jax.nn.sigmoid(pb[:, 3 * D_CONV + D_MODEL:])
    sconv0_new_ref[...] = sconv1_ref[...]
    sconv1_new_ref[...] = u


def _smix_state_kernel(s_ref, qt_ref, kt_ref, at_ref, v_ref, snew_ref, o_ref):
    base = pl.program_id(0) * SAMPLE_BLOCK
    lane = lax.broadcasted_iota(jnp.int32, (1, LANES), 1)
    for i in range(SAMPLE_BLOCK):
        pick = jnp.where(lane == base + i, 1.0, 0.0)
        for hd in range(N_HEADS):
            vc = slice(hd * DV, (hd + 1) * DV)
            a_col = jnp.sum(at_ref[hd] * pick, axis=1, keepdims=True)
            k_col = jnp.sum(kt_ref[hd] * pick, axis=1, keepdims=True)
            q_col = jnp.sum(qt_ref[hd] * pick, axis=1, keepdims=True)
            s_new = a_col * s_ref[0, i, hd] + k_col * v_ref[i:i + 1, vc]
            snew_ref[0, i, hd] = s_new
            o_ref[i:i + 1, vc] = jnp.sum(q_col * s_new, axis=0, keepdims=True)


def _smix_post_kernel(x_ref, gt_ref, gpost_ref, o_ref, sr_ref, yconv_ref, sga_ref, sgb_ref,
                      ggla_ref, wbo_ref, wmo_ref, y_ref, ygla_scr):
    for hd in range(N_HEADS):
        vc = slice(hd * DV, (hd + 1) * DV)
        on = _rms(o_ref[:, vc], ggla_ref[:, vc])
        ygla_scr[:, vc] = (on * sr_ref[:, vc]).astype(BF16)
    mix = _merge_and_project(ygla_scr[...], yconv_ref[...].astype(BF16), sga_ref[...],
                             sgb_ref[...], wbo_ref, wmo_ref)
    y_ref[...] = x_ref[...] + gt_ref[...] * _rms(mix, gpost_ref[1:2, :])


def _sample_mixer(x, ada, state_gla, state_conv, g_pre, g_post, wts):
    s = x.shape[0]
    wa, wz, wb, walpha, balpha, ggla, wconv, wbo, wmo = wts
    whole = lambda a: pl.BlockSpec(a.shape, lambda *_: (0,) * a.ndim)
    mod = lambda j: pl.BlockSpec((s, D_MODEL), lambda *_: (0, 3 + j))
    tok = jax.ShapeDtypeStruct((s, D_MODEL), F32)
    tr = jax.ShapeDtypeStruct((N_HEADS, DK, s), F32)

    pre_in = (x, ada, ada, g_pre, wa, wz, wb, walpha, balpha, wconv, state_conv[:, 0],
              state_conv[:, 1])
    pre_specs = [whole(x), mod(0), mod(1)] + [whole(a) for a in pre_in[3:]]
    pre_out = [tr, tr, tr, tok, tok, tok, tok, tok, tok, tok]
    qt, kt, at, v, sr, yconv, sga, sgb, sconv0_new, sconv1_new = pl.pallas_call(
        _smix_pre_kernel,
        grid=(1,),
        in_specs=pre_specs,
        out_specs=[whole(o) for o in pre_out],
        out_shape=pre_out,
        compiler_params=_params(("arbitrary",)),
        name="smix_pre",
    )(*pre_in)

    state_block = pl.BlockSpec((1, SAMPLE_BLOCK, N_HEADS, DK, DV), lambda j: (0, j, 0, 0, 0))
    snew, o = pl.pallas_call(
        _smix_state_kernel,
        grid=(s // SAMPLE_BLOCK,),
        in_specs=[state_block, whole(qt), whole(kt), whole(at),
                  pl.BlockSpec((SAMPLE_BLOCK, V_WIDTH), lambda j: (j, 0))],
        out_specs=[state_block, pl.BlockSpec((SAMPLE_BLOCK, V_WIDTH), lambda j: (j, 0))],
        out_shape=[jax.ShapeDtypeStruct(state_gla.shape, F32), tok],
        compiler_params=_params(("arbitrary",)),
        name="smix_state",
    )(state_gla, qt, kt, at, v)

    post_in = (x, ada, g_post, o, sr, yconv, sga, sgb, ggla, wbo, wmo)
    post_specs = [whole(x), mod(2)] + [whole(a) for a in post_in[2:]]
    y = pl.pallas_call(
        _smix_post_kernel,
        grid=(1,),
        in_specs=post_specs,
        out_specs=whole(tok),
        out_shape=tok,
        scratch_shapes=[pltpu.VMEM((s, V_WIDTH), BF16)],
        compiler_params=_params(("arbitrary",)),
        name="smix_post",
    )(*post_in)
    return y, snew, (sconv0_new, sconv1_new)


def kernel(x_prompt, x_sample, state_gla, state_conv, c_prompt, c_sample, w_ada, b_ada, g_pre,
           g_post, w_ffn1_in, w_ffn1_out, w_ffn2_in, w_ffn2_out, w_mix_in, w_alpha, b_alpha,
           g_gla_norm, w_conv, w_branch_out, w_mix_out):
    depth = w_ada.shape[0]
    n_s = x_sample.shape[0]
    assert depth == 1 and x_sample.shape[1] == 1 and n_s == LANES

    yp, ys = x_prompt, x_sample
    outs = []
    for i in range(depth):
        ada_s, ada_p = _ada(c_sample, c_prompt, w_ada[i], b_ada[i])

        ffn1_w = (w_ffn1_in[i].astype(BF16), w_ffn1_out[i].astype(BF16))
        n_ffn_steps = yp.shape[0] * (yp.shape[1] // FFN_TM)
        n_mix_steps = yp.shape[0] * (yp.shape[1] // MIX_TM)
        w_mix_t = jnp.swapaxes(w_mix_in[i], 0, 1)
        mixer_jobs = (
            _CastJob(w_mix_t, 2 * LANES, _A_WIDTH // (2 * LANES), transpose=True),
            _CastJob(w_mix_t, LANES, 1, transpose=True, row0=_A_WIDTH, valid=GATE_RANK),
            _CastJob(w_mix_t, 4 * LANES, _B_WIDTH // (4 * LANES), transpose=True,
                     row0=_A_WIDTH + GATE_RANK),
            _whole_cast(w_branch_out[i].reshape(2 * D_MODEL, D_MODEL), n_ffn_steps),
            _whole_cast(w_mix_out[i], n_ffn_steps))
        assert all(j.n_blocks <= n_ffn_steps for j in mixer_jobs)
        yp, ys, wa, wz, wb, wbo, wmo = _ffn(yp, ys, ada_p, ada_s, 0, g_pre[i], g_post[i], *ffn1_w,
                                            ys_shape=(n_s, D_MODEL), jobs=mixer_jobs)
        walpha = jnp.pad(w_alpha[i], ((0, LANES - GATE_RANK), (0, 0))).astype(BF16)
        mix_w = (wa, wz, wb, walpha, b_alpha[i].reshape(1, QK_WIDTH),
                 g_gla_norm[i].reshape(1, V_WIDTH),
                 jnp.pad(w_conv[i], ((0, SUBLANES - CONV_WIDTH), (0, 0))),
                 wbo.reshape(2, D_MODEL, D_MODEL), wmo)
        ys, gla_s, sconv_rows = _sample_mixer(ys, ada_s, state_gla[i:i + 1], state_conv[i],
                                              g_pre[i], g_post[i], mix_w)
        yp, gla_p, conv_p, *ffn2_w = _mixer(
            yp, ada_p, g_pre[i], g_post[i], mix_w,
            jobs=(_whole_cast(w_ffn2_in[i], n_mix_steps), _whole_cast(w_ffn2_out[i], n_mix_steps)))
        yp, ys = _ffn(yp, ys, ada_p, ada_s, 2, g_pre[i], g_post[i], *ffn2_w,
                      ys_shape=x_sample.shape)
        outs.append((gla_p, conv_p, gla_s, jnp.stack(sconv_rows, axis=1)[None]))

    gla_p, conv_p, gla_s, conv_s = outs[0]
    return (yp, ys, gla_p, conv_p, gla_s, conv_s)
```

```python
import functools
from typing import NamedTuple

import jax
import jax.numpy as jnp
from jax import lax
from jax.experimental import pallas as pl
from jax.experimental.pallas import tpu as pltpu

F32 = jnp.float32
BF16 = jnp.bfloat16

D_MODEL = 1024
D_FF = 2816
N_SUBLAYERS = 3
N_HEADS = 4
DK = 128
DV = 256
QK_WIDTH = N_HEADS * DK
V_WIDTH = N_HEADS * DV
GATE_RANK = 16
GATE_TAU = 16.0
D_CONV = D_MODEL
CONV_WIDTH = 3
RMS_EPS = 1e-6
MACARON_WEIGHT = 0.5

LANES = 128
SUBLANES = 8
GLA_CHUNK = 64
VMEM_LIMIT_BYTES = 56 * 1024 * 1024

FFN_TM = 1024
FFN_PART = 256
MIX_TM = 512
MIX_PART = 256
ADA_TN = 1536
SAMPLE_BLOCK = 16

_A_WIDTH = 2 * QK_WIDTH + 2 * V_WIDTH
_B_WIDTH = 3 * D_CONV + 2 * D_MODEL


def _rms(x, g):
    return x * lax.rsqrt(jnp.mean(x * x, axis=-1, keepdims=True) + RMS_EPS) * g


def _silu(x):
    return x * jax.nn.sigmoid(x)


def _dot(a, b):
    return jnp.dot(a, b, preferred_element_type=F32)


def _dot_nt(a, b):
    return lax.dot_general(a, b, (((1,), (1,)), ((), ())), preferred_element_type=F32)


def _dot_tn(a, b):
    return lax.dot_general(a, b, (((0,), (0,)), ((), ())), preferred_element_type=F32)


def _resident(shape):
    zeros = (0,) * len(shape)
    return pl.BlockSpec(shape, lambda *_: zeros, pipeline_mode=pl.Buffered(1))


def _params(semantics):
    return pltpu.CompilerParams(dimension_semantics=semantics, vmem_limit_bytes=VMEM_LIMIT_BYTES)


def _ada_kernel(cs_ref, cp_ref, w_ref, b_ref, os_ref, op_ref):
    w = w_ref[...].astype(BF16)
    for c_ref, o_ref in ((cs_ref, os_ref), (cp_ref, op_ref)):
        o_ref[...] = _dot(_silu(c_ref[...]).astype(BF16), w) + b_ref[...]


def _ada(c_sample, c_prompt, w_ada, b_ada):
    width = w_ada.shape[1]
    rows = lambda c: pl.BlockSpec((c.shape[0], D_MODEL), lambda j: (0, 0))
    cols = lambda c: pl.BlockSpec((c.shape[0], ADA_TN), lambda j: (0, j))
    return pl.pallas_call(
        _ada_kernel,
        grid=(width // ADA_TN,),
        in_specs=[rows(c_sample), rows(c_prompt),
                  pl.BlockSpec((D_MODEL, ADA_TN), lambda j: (0, j)),
                  pl.BlockSpec((1, ADA_TN), lambda j: (0, j))],
        out_specs=[cols(c_sample), cols(c_prompt)],
        out_shape=[jax.ShapeDtypeStruct((c.shape[0], width), F32) for c in (c_sample, c_prompt)],
        compiler_params=_params(("arbitrary",)),
        name="ada",
    )(c_sample, c_prompt, w_ada, b_ada.reshape(1, width))


def _mod_specs(sub, ada):
    return [pl.BlockSpec((ada.shape[0], D_MODEL), lambda *_, col=sub * 3 + j: (0, col))
            for j in range(3)]


def _load_mod(ref, row):
    if row is None:
        return ref[...]
    return ref[pl.ds(row, 1), :]


BF16_SUBLANES = 16


class _CastJob(NamedTuple):
    src: jax.Array
    block_rows: int
    n_blocks: int
    transpose: bool = False
    row0: int = 0
    valid: int = 0

    def specs(self, step_of):
        cols = self.src.shape[1]
        r, nb = self.block_rows, self.n_blocks
        block = lambda *g: jnp.minimum(step_of(*g), nb - 1)
        if not self.transpose:
            spec = pl.BlockSpec((r, cols), lambda *g: (block(*g), 0))
            return spec, spec, jax.ShapeDtypeStruct((nb * r, cols), BF16)
        assert self.row0 % SUBLANES == 0 and r % SUBLANES == 0
        return (pl.BlockSpec((pl.Element(r), pl.Element(cols)),
                             lambda *g: (pl.multiple_of(self.row0 + r * block(*g), SUBLANES), 0)),
                pl.BlockSpec((cols, r), lambda *g: (0, block(*g))),
                jax.ShapeDtypeStruct((cols, nb * r), BF16))

    @property
    def static(self):
        return (self.transpose, self.valid or self.block_rows)


def _whole_cast(w, n_steps):
    rows = w.shape[0]
    r = next(r for r in range(BF16_SUBLANES, rows + 1, BF16_SUBLANES)
             if rows % r == 0 and rows // r <= n_steps)
    return _CastJob(w, r, rows // r)


def _cast_plumbing(jobs, step_of):
    specs = [job.specs(step_of) for job in jobs]
    return ([i for i, _, _ in specs], [o for _, o, _ in specs], [s for _, _, s in specs])


def _run_casts(statics, src_refs, dst_refs):
    for (transpose, valid), src, dst in zip(statics, src_refs, dst_refs, strict=True):
        blk = src[...]
        if transpose:
            blk = blk.T
            if valid < blk.shape[1]:
                lane = lax.broadcasted_iota(jnp.int32, blk.shape, 1)
                blk = jnp.where(lane < valid, blk, 0.0)
        dst[...] = blk.astype(BF16)


def _ffn_rows(x_ref, o_ref, mods, per_token, g_pre, g_post, win_ref, wout_ref):
    shift, scale, gate = mods
    tm = x_ref.shape[0]
    parts = max(tm // FFN_PART, 1)
    rows = [slice(i * (tm // parts), (i + 1) * (tm // parts)) for i in range(parts)]
    mod = lambda m, rs: m[rs] if per_token else m

    def pre(rs):
        return (_rms(x_ref[rs, :], g_pre) * (1.0 + mod(scale, rs)) + mod(shift, rs)).astype(BF16)

    def post(rs, out):
        o_ref[rs, :] = x_ref[rs, :] + MACARON_WEIGHT * mod(gate, rs) * _rms(out, g_post)

    h = pre(rows[0])
    for i in range(parts):
        g = _dot(h, win_ref[:, :D_FF])
        h_next = pre(rows[i + 1]) if i + 1 < parts else None
        u = _dot(h, win_ref[:, D_FF:])
        a = (_silu(g) * u).astype(BF16)
        if i > 0:
            post(rows[i - 1], out)
        out = _dot(a, wout_ref[...])
        h = h_next
    post(rows[parts - 1], out)


def _ffn_kernel(x_ref, sh_ref, sc_ref, gt_ref, xs_ref, shs_ref, scs_ref, gts_ref, gpre_ref,
                gpost_ref, win_ref, wout_ref, *rest, sub, n_l, job_groups):
    n_jobs = len(job_groups)
    o_ref, os_ref = rest[n_jobs:n_jobs + 2]
    _run_casts(job_groups, rest[:n_jobs], rest[n_jobs + 2:])
    step = pl.program_id(0)
    last = pl.num_programs(0) - 1
    weights = (gpre_ref[sub:sub + 1, :], gpost_ref[sub:sub + 1, :], win_ref, wout_ref)

    @pl.when(step < last)
    def _():
        mods = [_load_mod(r, step // n_l) for r in (sh_ref, sc_ref, gt_ref)]
        _ffn_rows(x_ref, o_ref, mods, False, *weights)

    @pl.when(step == last)
    def _():
        mods = [_load_mod(r, None) for r in (shs_ref, scs_ref, gts_ref)]
        _ffn_rows(xs_ref, os_ref, mods, True, *weights)


def _token_rows_spec(shape):
    if len(shape) == 2:
        return pl.BlockSpec(shape, lambda i: (0, 0))
    return pl.BlockSpec((shape[0], None, shape[2]), lambda i: (0, 0, 0))


def _ffn(x, xs, ada_p, ada_s, sub, g_pre, g_post, w_in, w_out, *, ys_shape, jobs=()):
    n, l, _ = x.shape
    n_l = l // FFN_TM
    n_tiles = n * n_l
    tile = pl.BlockSpec(
        (None, FFN_TM, D_MODEL),
        lambda i: (jnp.minimum(i, n_tiles - 1) // n_l, jnp.minimum(i, n_tiles - 1) % n_l, 0))
    kern = functools.partial(_ffn_kernel, sub=sub, n_l=n_l,
                             job_groups=tuple(j.static for j in jobs))
    cast_in, cast_out, cast_shapes = _cast_plumbing(jobs, lambda i: i)
    return pl.pallas_call(
        kern,
        grid=(n_tiles + 1,),
        in_specs=[tile] + _mod_specs(sub, ada_p) + [_token_rows_spec(xs.shape)]
        + _mod_specs(sub, ada_s)
        + [_resident(g_pre.shape), _resident(g_post.shape),
           _resident((D_MODEL, 2 * D_FF)), _resident((D_FF, D_MODEL))]
        + cast_in,
        out_specs=[tile, _token_rows_spec(ys_shape)] + cast_out,
        out_shape=[jax.ShapeDtypeStruct(x.shape, F32), jax.ShapeDtypeStruct(ys_shape, F32)]
        + cast_shapes,
        compiler_params=_params(("arbitrary",)),
        name="ffn",
    )(x, ada_p, ada_p, ada_p, xs, ada_s, ada_s, ada_s, g_pre, g_post, w_in, w_out,
      *[j.src for j in jobs])


def _log_decay(hb, wz_ref, walpha_ref, balpha_ref):
    z = _dot(hb, wz_ref[...]).astype(BF16)
    xg = _dot(z, walpha_ref[...]) + balpha_ref[...]
    return jax.nn.log_sigmoid(xg) * (1.0 / GATE_TAU)


def _merge_and_project(y_gla, y_conv, sig_a, sig_b, wbo_ref, wmo_ref):
    pg = _dot(y_gla, wbo_ref[0])
    pc = _dot(y_conv, wbo_ref[1])
    merged = (sig_a * pg + sig_b * pc).astype(BF16)
    return _dot(merged, wmo_ref[...])


_PAIR_LEVELS = tuple(GLA_CHUNK >> i for i in range(1, GLA_CHUNK.bit_length()))


def _pair_operands(qs, k, b, log_a, m, tm):
    if m >= SUBLANES:
        halves = lambda a: a.reshape(tm // (2 * m), 2, m, QK_WIDTH)
        b4, k4, q4 = halves(b), halves(k), halves(qs)
        mid = b4[:, 0:1, m - 1:m, :]
        w = jnp.exp(jnp.concatenate([mid - b4[:, 0:1], b4[:, 1:2] - mid], axis=1))
        src = jnp.concatenate([k4[:, 0:1], q4[:, 1:2]], axis=1)
        return (src * w).astype(BF16).reshape(tm, QK_WIDTH)
    tiles = lambda a: a.reshape(tm // SUBLANES, SUBLANES, QK_WIDTH)
    sub = lax.broadcasted_iota(jnp.int32, (1, SUBLANES, QK_WIDTH), 1)
    second = (sub & m) != 0
    if m == 1:
        w = jnp.exp(jnp.where(second, tiles(log_a), 0.0))
    else:
        b4 = tiles(b)
        mid = b4[:, m - 1:m, :]
        for start in range(2 * m, SUBLANES, 2 * m):
            mid = jnp.where(sub >= start, b4[:, start + m - 1:start + m, :], mid)
        w = jnp.exp((b4 - mid) * jnp.where(second, 1.0, -1.0))
    return (jnp.where(second, tiles(qs), tiles(k)) * w).astype(BF16).reshape(tm, QK_WIDTH)


def _pair_mask(m):
    t = lax.broadcasted_iota(jnp.int32, (GLA_CHUNK, GLA_CHUNK), 0)
    s = lax.broadcasted_iota(jnp.int32, (GLA_CHUNK, GLA_CHUNK), 1)
    return ((t & m) != 0) & ((s & m) == 0) & ((t // (2 * m)) == (s // (2 * m)))


def _mixer_kernel(x_ref, sh_ref, sc_ref, gt_ref, gpre_ref, gpost_ref, wa_ref, wz_ref, wb_ref,
                  walpha_ref, balpha_ref, ggla_ref, wconv_ref, wbo_ref, wmo_ref, *rest,
                  job_groups):
    n_jobs = len(job_groups)
    y_ref, sgla_ref, sconv_ref = rest[n_jobs:n_jobs + 3]
    st_scr, u_scr, ygla_scr = rest[-3:]
    _run_casts(job_groups, rest[:n_jobs], rest[n_jobs + 3:-3])
    l = pl.program_id(1)
    n_l = pl.num_programs(1)

    @pl.when(l == 0)
    def _():
        st_scr[...] = jnp.zeros_like(st_scr)
        u_scr[0:SUBLANES, :] = jnp.zeros((SUBLANES, D_CONV), F32)

    mods = [_load_mod(r, pl.program_id(0)) for r in (sh_ref, sc_ref, gt_ref)]
    weights = (gpre_ref.at[1:2], gpost_ref.at[1:2], wa_ref, wz_ref, wb_ref, walpha_ref,
               balpha_ref, ggla_ref, wconv_ref, wbo_ref, wmo_ref)
    n_parts = x_ref.shape[1] // MIX_PART
    parts = [_mixer_part(x_ref, y_ref, mods, weights, (st_scr, u_scr, ygla_scr),
                         slice(i * MIX_PART, (i + 1) * MIX_PART)) for i in range(n_parts)]
    next(parts[0])
    for i, p in enumerate(parts):
        next(p)
        if i + 1 < n_parts:
            next(parts[i + 1])
        if i > 0:
            next(parts[i - 1], None)
        next(p)
    next(parts[-1], None)

    @pl.when(l == n_l - 1)
    def _():
        sgla_ref[0, 0] = st_scr[...]
        sconv_ref[0, 0] = u_scr[SUBLANES - 2:SUBLANES, :]


def _mixer_part(x_ref, y_ref, mods, weights, scratch, prow):
    shift, scale, gate = mods
    (gpre_ref, gpost_ref, wa_ref, wz_ref, wb_ref, walpha_ref, balpha_ref, ggla_ref, wconv_ref,
     wbo_ref, wmo_ref) = weights
    st_scr, u_scr, ygla_scr = scratch
    ygla_scr = ygla_scr.at[prow]
    tm = MIX_PART
    x = x_ref[0, prow, :]
    hb = (_rms(x, gpre_ref[...]) * (1.0 + scale) + shift).astype(BF16)
    yield

    def proj_b(j):
        return _dot(hb, wb_ref[:, j * D_MODEL:(j + 1) * D_MODEL])

    nc = tm // GLA_CHUNK
    log_a = _log_decay(hb, wz_ref, walpha_ref, balpha_ref)
    pa = _dot(hb, wa_ref[...])
    qs = pa[:, 0:QK_WIDTH] * (DK ** -0.5)
    k = pa[:, QK_WIDTH:2 * QK_WIDTH]
    vb = pa[:, 2 * QK_WIDTH:2 * QK_WIDTH + V_WIDTH].astype(BF16)
    r = pa[:, 2 * QK_WIDTH + V_WIDTH:]

    row = lax.broadcasted_iota(jnp.int32, (tm, tm), 0)
    col = lax.broadcasted_iota(jnp.int32, (tm, tm), 1)
    tri = jnp.where((row >= col) & (row // GLA_CHUNK == col // GLA_CHUNK), 1.0, 0.0).astype(BF16)
    la_hi = log_a.astype(BF16)
    rem = log_a - la_hi.astype(F32)
    la_mid = rem.astype(BF16)
    la_lo = (rem - la_mid.astype(F32)).astype(BF16)
    b3 = (_dot(tri, la_hi) + _dot(tri, la_mid) + _dot(tri, la_lo)).reshape(
        nc, GLA_CHUNK, QK_WIDTH)

    u = proj_b(1) * proj_b(2)
    u_scr[SUBLANES:SUBLANES + tm, :] = u
    yield

    b_last = b3[:, GLA_CHUNK - 1:GLA_CHUNK, :]
    b = b3.reshape(tm, QK_WIDTH)
    qd = (qs * jnp.exp(b)).astype(BF16).reshape(nc, GLA_CHUNK, QK_WIDTH)
    kl = (k.reshape(nc, GLA_CHUNK, QK_WIDTH) * jnp.exp(b_last - b3)).astype(BF16)
    b_last_rows = jnp.concatenate(
        [b_last[c, :, hd * DK:(hd + 1) * DK] for c in range(nc) for hd in range(N_HEADS)]
        + [jnp.zeros((LANES - nc * N_HEADS, DK), F32)], axis=0)
    decay_cols = jnp.exp(b_last_rows.T)

    pair_ops = [_pair_operands(qs, k, b, log_a, m, tm) for m in _PAIR_LEVELS]
    pair_masks = [_pair_mask(m) for m in _PAIR_LEVELS]
    qs_b, k_b = qs.astype(BF16), k.astype(BF16)
    crow = lax.broadcasted_iota(jnp.int32, (GLA_CHUNK, GLA_CHUNK), 0)
    ccol = lax.broadcasted_iota(jnp.int32, (GLA_CHUNK, GLA_CHUNK), 1)
    scores, upd = {}, {}
    for c in range(nc):
        rows = slice(c * GLA_CHUNK, (c + 1) * GLA_CHUNK)
        for hd in range(N_HEADS):
            kc = slice(hd * DK, (hd + 1) * DK)
            s = jnp.where(crow == ccol, _dot_nt(qs_b[rows, kc], k_b[rows, kc]), 0.0)
            for p, mask in zip(pair_ops, pair_masks):
                s = jnp.where(mask, _dot_nt(p[rows, kc], p[rows, kc]), s)
            scores[c, hd] = s.astype(BF16)
            upd[c, hd] = _dot_tn(kl[c, :, kc], vb[rows, hd * DV:(hd + 1) * DV])

    conv = (wconv_ref[0:1, :] * u_scr[SUBLANES - 2:SUBLANES - 2 + tm, :]
            + wconv_ref[1:2, :] * u_scr[SUBLANES - 1:SUBLANES - 1 + tm, :]
            + wconv_ref[2:3, :] * u)
    u_scr[0:SUBLANES, :] = u_scr[tm:tm + SUBLANES, :]
    y_conv = (proj_b(0) * conv).astype(BF16)

    outs = {}
    for hd in range(N_HEADS):
        kc = slice(hd * DK, (hd + 1) * DK)
        vc = slice(hd * DV, (hd + 1) * DV)
        st = st_scr[hd]
        for c in range(nc):
            rows = slice(c * GLA_CHUNK, (c + 1) * GLA_CHUNK)
            lhs = jnp.concatenate([qd[c, :, kc], scores[c, hd]], axis=1)
            rhs = jnp.concatenate([st.astype(BF16), vb[rows, vc]], axis=0)
            outs[c, hd] = _dot(lhs, rhs)
            j = c * N_HEADS + hd
            st = st * decay_cols[:, j:j + 1] + upd[c, hd]
        st_scr[hd] = st

    sig_a = jax.nn.sigmoid(proj_b(3))
    sig_b = jax.nn.sigmoid(proj_b(4))

    for hd in range(N_HEADS):
        vc = slice(hd * DV, (hd + 1) * DV)
        for c in range(nc):
            rows = slice(c * GLA_CHUNK, (c + 1) * GLA_CHUNK)
            on = _rms(outs[c, hd], ggla_ref[:, vc])
            ygla_scr[rows, vc] = (on * _silu(r[rows, vc])).astype(BF16)

    yield
    mix = _merge_and_project(ygla_scr[...], y_conv, sig_a, sig_b, wbo_ref, wmo_ref)
    y_ref[0, prow, :] = x + gate * _rms(mix, gpost_ref[...])


def _mixer(x, ada, g_pre, g_post, wts, *, jobs=()):
    n, l, _ = x.shape
    tm = MIX_TM
    n_l = l // tm
    kern = functools.partial(_mixer_kernel, job_groups=tuple(j.static for j in jobs))
    cast_in, cast_out, cast_shapes = _cast_plumbing(jobs, lambda n, l: n * n_l + l)
    wa, wz, wb, walpha, balpha, ggla, wconv, wbo, wmo = wts
    return pl.pallas_call(
        kern,
        grid=(n, n_l),
        in_specs=[pl.BlockSpec((1, tm, D_MODEL), lambda n, l: (n, l, 0))]
        + _mod_specs(1, ada)
        + [_resident(a.shape) for a in (g_pre, g_post, wa, wz, wb, walpha, balpha, ggla, wconv,
                                        wbo, wmo)]
        + cast_in,
        out_specs=[
            pl.BlockSpec((1, tm, D_MODEL), lambda n, l: (n, l, 0)),
            pl.BlockSpec((1, 1, N_HEADS, DK, DV), lambda n, l: (0, n, 0, 0, 0)),
            pl.BlockSpec((1, 1, CONV_WIDTH - 1, D_CONV), lambda n, l: (0, n, 0, 0)),
        ] + cast_out,
        out_shape=[
            jax.ShapeDtypeStruct(x.shape, F32),
            jax.ShapeDtypeStruct((1, n, N_HEADS, DK, DV), F32),
            jax.ShapeDtypeStruct((1, n, CONV_WIDTH - 1, D_CONV), F32),
        ] + cast_shapes,
        scratch_shapes=[
            pltpu.VMEM((N_HEADS, DK, DV), F32),
            pltpu.VMEM((MIX_PART + SUBLANES, D_CONV), F32),
            pltpu.VMEM((tm, V_WIDTH), BF16),
        ],
        compiler_params=_params(("arbitrary", "arbitrary")),
        name="mixer",
    )(x, ada, ada, ada, g_pre, g_post, wa, wz, wb, walpha, balpha, ggla, wconv, wbo, wmo,
      *[j.src for j in jobs])


def _smix_pre_kernel(x_ref, sh_ref, sc_ref, gpre_ref, wa_ref, wz_ref, wb_ref, walpha_ref,
                     balpha_ref, wconv_ref, sconv0_ref, sconv1_ref,
                     qt_ref, kt_ref, at_ref, v_ref, sr_ref, yconv_ref, sga_ref, sgb_ref,
                     sconv0_new_ref, sconv1_new_ref):
    x = x_ref[...]
    hb = (_rms(x, gpre_ref[1:2, :]) * (1.0 + sc_ref[...]) + sh_ref[...]).astype(BF16)
    pa = _dot(hb, wa_ref[...])
    q = pa[:, 0:QK_WIDTH] * (DK ** -0.5)
    k = pa[:, QK_WIDTH:2 * QK_WIDTH]
    v_ref[...] = pa[:, 2 * QK_WIDTH:2 * QK_WIDTH + V_WIDTH]
    sr_ref[...] = _silu(pa[:, 2 * QK_WIDTH + V_WIDTH:])
    a = jnp.exp(_log_decay(hb, wz_ref, walpha_ref, balpha_ref))
    for hd in range(N_HEADS):
        kc = slice(hd * DK, (hd + 1) * DK)
        qt_ref[hd] = q[:, kc].T
        kt_ref[hd] = k[:, kc].T
        at_ref[hd] = a[:, kc].T

    pb = _dot(hb, wb_ref[...])
    u = pb[:, D_CONV:2 * D_CONV] * pb[:, 2 * D_CONV:3 * D_CONV]
    conv = (wconv_ref[0:1, :] * sconv0_ref[...] + wconv_ref[1:2, :] * sconv1_ref[...]
            + wconv_ref[2:3, :] * u)
    yconv_ref[...] = pb[:, 0:D_CONV] * conv
    sga_ref[...] = jax.nn.sigmoid(pb[:, 3 * D_CONV:3 * D_CONV + D_MODEL])
    sgb_ref[...] = jax.nn.sigmoid(pb[:, 3 * D_CONV + D_MODEL:])
    sconv0_new_ref[...] = sconv1_ref[...]
    sconv1_new_ref[...] = u


def _smix_state_kernel(s_ref, qt_ref, kt_ref, at_ref, v_ref, snew_ref, o_ref):
    base = pl.program_id(0) * SAMPLE_BLOCK
    lane = lax.broadcasted_iota(jnp.int32, (1, LANES), 1)
    for i in range(SAMPLE_BLOCK):
        pick = jnp.where(lane == base + i, 1.0, 0.0)
        for hd in range(N_HEADS):
            vc = slice(hd * DV, (hd + 1) * DV)
            a_col = jnp.sum(at_ref[hd] * pick, axis=1, keepdims=True)
            k_col = jnp.sum(kt_ref[hd] * pick, axis=1, keepdims=True)
            q_col = jnp.sum(qt_ref[hd] * pick, axis=1, keepdims=True)
            s_new = a_col * s_ref[0, i, hd] + k_col * v_ref[i:i + 1, vc]
            snew_ref[0, i, hd] = s_new
            o_ref[i:i + 1, vc] = jnp.sum(q_col * s_new, axis=0, keepdims=True)


def _smix_post_kernel(x_ref, gt_ref, gpost_ref, o_ref, sr_ref, yconv_ref, sga_ref, sgb_ref,
                      ggla_ref, wbo_ref, wmo_ref, y_ref, ygla_scr):
    for hd in range(N_HEADS):
        vc = slice(hd * DV, (hd + 1) * DV)
        on = _rms(o_ref[:, vc], ggla_ref[:, vc])
        ygla_scr[:, vc] = (on * sr_ref[:, vc]).astype(BF16)
    mix = _merge_and_project(ygla_scr[...], yconv_ref[...].astype(BF16), sga_ref[...],
                             sgb_ref[...], wbo_ref, wmo_ref)
    y_ref[...] = x_ref[...] + gt_ref[...] * _rms(mix, gpost_ref[1:2, :])


def _sample_mixer(x, ada, state_gla, state_conv, g_pre, g_post, wts):
    s = x.shape[0]
    wa, wz, wb, walpha, balpha, ggla, wconv, wbo, wmo = wts
    whole = lambda a: pl.BlockSpec(a.shape, lambda *_: (0,) * a.ndim)
    mod = lambda j: pl.BlockSpec((s, D_MODEL), lambda *_: (0, 3 + j))
    tok = jax.ShapeDtypeStruct((s, D_MODEL), F32)
    tr = jax.ShapeDtypeStruct((N_HEADS, DK, s), F32)

    pre_in = (x, ada, ada, g_pre, wa, wz, wb, walpha, balpha, wconv, state_conv[:, 0],
              state_conv[:, 1])
    pre_specs = [whole(x), mod(0), mod(1)] + [whole(a) for a in pre_in[3:]]
    pre_out = [tr, tr, tr, tok, tok, tok, tok, tok, tok, tok]
    qt, kt, at, v, sr, yconv, sga, sgb, sconv0_new, sconv1_new = pl.pallas_call(
        _smix_pre_kernel,
        grid=(1,),
        in_specs=pre_specs,
        out_specs=[whole(o) for o in pre_out],
        out_shape=pre_out,
        compiler_params=_params(("arbitrary",)),
        name="smix_pre",
    )(*pre_in)

    state_block = pl.BlockSpec((1, SAMPLE_BLOCK, N_HEADS, DK, DV), lambda j: (0, j, 0, 0, 0))
    snew, o = pl.pallas_call(
        _smix_state_kernel,
        grid=(s // SAMPLE_BLOCK,),
        in_specs=[state_block, whole(qt), whole(kt), whole(at),
                  pl.BlockSpec((SAMPLE_BLOCK, V_WIDTH), lambda j: (j, 0))],
        out_specs=[state_block, pl.BlockSpec((SAMPLE_BLOCK, V_WIDTH), lambda j: (j, 0))],
        out_shape=[jax.ShapeDtypeStruct(state_gla.shape, F32), tok],
        compiler_params=_params(("arbitrary",)),
        name="smix_state",
    )(state_gla, qt, kt, at, v)

    post_in = (x, ada, g_post, o, sr, yconv, sga, sgb, ggla, wbo, wmo)
    post_specs = [whole(x), mod(2)] + [whole(a) for a in post_in[2:]]
    y = pl.pallas_call(
        _smix_post_kernel,
        grid=(1,),
        in_specs=post_specs,
        out_specs=whole(tok),
        out_shape=tok,
        scratch_shapes=[pltpu.VMEM((s, V_WIDTH), BF16)],
        compiler_params=_params(("arbitrary",)),
        name="smix_post",
    )(*post_in)
    return y, snew, (sconv0_new, sconv1_new)


def kernel(x_prompt, x_sample, state_gla, state_conv, c_prompt, c_sample, w_ada, b_ada, g_pre,
           g_post, w_ffn1_in, w_ffn1_out, w_ffn2_in, w_ffn2_out, w_mix_in, w_alpha, b_alpha,
           g_gla_norm, w_conv, w_branch_out, w_mix_out):
    depth = w_ada.shape[0]
    n_s = x_sample.shape[0]
    assert depth == 1 and x_sample.shape[1] == 1 and n_s == LANES

    yp, ys = x_prompt, x_sample.reshape(n_s, D_MODEL)
    outs = []
    for i in range(depth):
        ada_s, ada_p = _ada(c_sample, c_prompt, w_ada[i], b_ada[i])

        ffn1_w = (w_ffn1_in[i].astype(BF16), w_ffn1_out[i].astype(BF16))
        n_ffn_steps = yp.shape[0] * (yp.shape[1] // FFN_TM)
        n_mix_steps = yp.shape[0] * (yp.shape[1] // MIX_TM)
        w_mix_t = jnp.swapaxes(w_mix_in[i], 0, 1)
        mixer_jobs = (
            _CastJob(w_mix_t, 2 * LANES, _A_WIDTH // (2 * LANES), transpose=True),
            _CastJob(w_mix_t, LANES, 1, transpose=True, row0=_A_WIDTH, valid=GATE_RANK),
            _CastJob(w_mix_t, 4 * LANES, _B_WIDTH // (4 * LANES), transpose=True,
                     row0=_A_WIDTH + GATE_RANK),
            _whole_cast(w_branch_out[i].reshape(2 * D_MODEL, D_MODEL), n_ffn_steps),
            _whole_cast(w_mix_out[i], n_ffn_steps))
        assert all(j.n_blocks <= n_ffn_steps for j in mixer_jobs)
        yp, ys, wa, wz, wb, wbo, wmo = _ffn(yp, ys, ada_p, ada_s, 0, g_pre[i], g_post[i], *ffn1_w,
                                            ys_shape=(n_s, D_MODEL), jobs=mixer_jobs)
        walpha = jnp.pad(w_alpha[i], ((0, LANES - GATE_RANK), (0, 0))).astype(BF16)
        mix_w = (wa, wz, wb, walpha, b_alpha[i].reshape(1, QK_WIDTH),
                 g_gla_norm[i].reshape(1, V_WIDTH),
                 jnp.pad(w_conv[i], ((0, SUBLANES - CONV_WIDTH), (0, 0))),
                 wbo.reshape(2, D_MODEL, D_MODEL), wmo)
        ys, gla_s, sconv_rows = _sample_mixer(ys, ada_s, state_gla[i:i + 1], state_conv[i],
                                              g_pre[i], g_post[i], mix_w)
        yp, gla_p, conv_p, *ffn2_w = _mixer(
            yp, ada_p, g_pre[i], g_post[i], mix_w,
            jobs=(_whole_cast(w_ffn2_in[i], n_mix_steps), _whole_cast(w_ffn2_out[i], n_mix_steps)))
        yp, ys = _ffn(yp, ys, ada_p, ada_s, 2, g_pre[i], g_post[i], *ffn2_w,
                      ys_shape=x_sample.shape)
        outs.append((gla_p, conv_p, gla_s, jnp.stack(sconv_rows, axis=1)[None]))

    gla_p, conv_p, gla_s, conv_s = outs[0]
    return (yp, ys, gla_p, conv_p, gla_s, conv_s)
```

```python
import functools
from typing import NamedTuple

import jax
import jax.numpy as jnp
from jax import lax
from jax.experimental import pallas as pl
from jax.experimental.pallas import tpu as pltpu

F32 = jnp.float32
BF16 = jnp.bfloat16

D_MODEL = 1024
D_FF = 2816
N_SUBLAYERS = 3
N_HEADS = 4
DK = 128
DV = 256
QK_WIDTH = N_HEADS * DK
V_WIDTH = N_HEADS * DV
GATE_RANK = 16
GATE_TAU = 16.0
D_CONV = D_MODEL
CONV_WIDTH = 3
RMS_EPS = 1e-6
MACARON_WEIGHT = 0.5

LANES = 128
SUBLANES = 8
GLA_CHUNK = 64
VMEM_LIMIT_BYTES = 56 * 1024 * 1024

FFN_TM = 1024
FFN_PART = 256
MIX_TM = 512
MIX_PART = 256
ADA_TN = 3072
SAMPLE_BLOCK = 16

_A_WIDTH = 2 * QK_WIDTH + 2 * V_WIDTH
_B_WIDTH = 3 * D_CONV + 2 * D_MODEL


def _rms(x, g):
    return x * lax.rsqrt(jnp.mean(x * x, axis=-1, keepdims=True) + RMS_EPS) * g


def _silu(x):
    return x * jax.nn.sigmoid(x)


def _dot(a, b):
    return jnp.dot(a, b, preferred_element_type=F32)


def _dot_nt(a, b):
    return lax.dot_general(a, b, (((1,), (1,)), ((), ())), preferred_element_type=F32)


def _dot_tn(a, b):
    return lax.dot_general(a, b, (((0,), (0,)), ((), ())), preferred_element_type=F32)


def _resident(shape):
    zeros = (0,) * len(shape)
    return pl.BlockSpec(shape, lambda *_: zeros, pipeline_mode=pl.Buffered(1))


def _params(semantics):
    return pltpu.CompilerParams(dimension_semantics=semantics, vmem_limit_bytes=VMEM_LIMIT_BYTES)


def _ada_kernel(c_ref, w_ref, b_ref, o_ref):
    c = c_ref[...]
    o_ref[...] = _dot(_silu(c).astype(BF16), w_ref[...].astype(BF16)) + b_ref[...]


def _ada(c_all, w_ada, b_ada):
    rows = c_all.shape[0]
    width = w_ada.shape[1]
    return pl.pallas_call(
        _ada_kernel,
        grid=(width // ADA_TN,),
        in_specs=[
            pl.BlockSpec((rows, D_MODEL), lambda j: (0, 0)),
            pl.BlockSpec((D_MODEL, ADA_TN), lambda j: (0, j)),
            pl.BlockSpec((1, ADA_TN), lambda j: (0, j)),
        ],
        out_specs=pl.BlockSpec((rows, ADA_TN), lambda j: (0, j)),
        out_shape=jax.ShapeDtypeStruct((rows, width), F32),
        compiler_params=_params(("arbitrary",)),
        name="ada",
    )(c_all, w_ada, b_ada.reshape(1, width))


def _mod_specs(sub, per_token, prompt_row_block):
    specs = []
    for j in range(3):
        col = sub * 3 + j
        if per_token:
            specs.append(pl.BlockSpec((LANES, D_MODEL), lambda *_, col=col: (0, col)))
        else:
            specs.append(pl.BlockSpec((SUBLANES, D_MODEL),
                                      lambda *_, col=col: (prompt_row_block, col)))
    return specs


def _load_mod(ref, row):
    if row is None:
        return ref[...]
    return ref[pl.ds(row, 1), :]


BF16_SUBLANES = 16
WEIGHT_PITCH_TILES = 8


class _CastJob(NamedTuple):
    src: jax.Array
    block_rows: int
    n_blocks: int
    transpose: bool = False
    row0: int = 0
    valid: int = 0
    pad: int = 0

    @property
    def n_steps(self):
        return self.n_blocks + (self.pad if self.transpose else 0)

    def specs(self, step_of):
        cols = self.src.shape[1]
        r, nb = self.block_rows, self.n_blocks
        block = lambda *g: jnp.minimum(step_of(*g), nb - 1)
        if not self.transpose:
            return (pl.BlockSpec((r, cols), lambda *g: (block(*g), 0)),
                    pl.BlockSpec((r, cols + self.pad), lambda *g: (block(*g), 0)),
                    jax.ShapeDtypeStruct((nb * r, cols + self.pad), BF16))
        assert self.row0 % SUBLANES == 0 and r % SUBLANES == 0
        out_block = lambda *g: jnp.minimum(step_of(*g), self.n_steps - 1)
        return (pl.BlockSpec((pl.Element(r), pl.Element(cols)),
                             lambda *g: (pl.multiple_of(self.row0 + r * block(*g), SUBLANES), 0)),
                pl.BlockSpec((cols, r), lambda *g: (0, out_block(*g))),
                jax.ShapeDtypeStruct((cols, self.n_steps * r), BF16))

    @property
    def static(self):
        return (self.transpose, self.valid or self.block_rows, self.n_blocks, self.pad)


def _whole_cast(w, n_steps, pad=0):
    rows = w.shape[0]
    r = next(r for r in range(BF16_SUBLANES, rows + 1, BF16_SUBLANES)
             if rows % r == 0 and rows // r <= n_steps)
    return _CastJob(w, r, rows // r, pad=pad)


def _cast_plumbing(jobs, step_of):
    specs = [job.specs(step_of) for job in jobs]
    return ([i for i, _, _ in specs], [o for _, o, _ in specs], [s for _, _, s in specs])


def _run_casts(statics, step, src_refs, dst_refs):
    for (transpose, valid, n_blocks, pad), src, dst in zip(statics, src_refs, dst_refs,
                                                           strict=True):
        blk = src[...]
        if not transpose:
            cols = blk.shape[1]
            dst[:, :cols] = blk.astype(BF16)
            if pad:
                dst[:, cols:] = jnp.zeros((blk.shape[0], pad), BF16)
            continue
        blk = blk.T
        if valid < blk.shape[1]:
            lane = lax.broadcasted_iota(jnp.int32, blk.shape, 1)
            blk = jnp.where(lane < valid, blk, 0.0)
        if pad:
            blk = jnp.where(step < n_blocks, blk, 0.0)
        dst[...] = blk.astype(BF16)


def _ffn_rows(x_ref, o_ref, mods, per_token, gpre_ref, gpost_ref, win_ref, wout_ref):
    shift, scale, gate = mods
    tm = x_ref.shape[1]
    parts = max(tm // FFN_PART, 1)
    rows = [slice(i * (tm // parts), (i + 1) * (tm // parts)) for i in range(parts)]
    mod = lambda m, rs: m[rs] if per_token else m

    def pre(rs):
        return (_rms(x_ref[0, rs, :], gpre_ref[...]) * (1.0 + mod(scale, rs))
                + mod(shift, rs)).astype(BF16)

    def post(rs, out):
        o_ref[0, rs, :] = (x_ref[0, rs, :]
                           + MACARON_WEIGHT * mod(gate, rs) * _rms(out, gpost_ref[...]))

    h = pre(rows[0])
    for i in range(parts):
        g = _dot(h, win_ref[:, :D_FF])
        h_next = pre(rows[i + 1]) if i + 1 < parts else None
        u = _dot(h, win_ref[:, D_FF:])
        a = (_silu(g) * u).astype(BF16)
        if i > 0:
            post(rows[i - 1], out)
        out = _dot(a, wout_ref[:, :D_MODEL])
        h = h_next
    post(rows[parts - 1], out)


def _ffn_kernel(x_ref, sh_ref, sc_ref, gt_ref, xs_ref, shs_ref, scs_ref, gts_ref, gpre_ref,
                gpost_ref, win_ref, wout_ref, *rest, n_l, job_groups):
    n_jobs = len(job_groups)
    o_ref, os_ref = rest[n_jobs:n_jobs + 2]
    step = pl.program_id(0)
    _run_casts(job_groups, step, rest[:n_jobs], rest[n_jobs + 2:])
    last = pl.num_programs(0) - 1
    weights = (gpre_ref, gpost_ref, win_ref, wout_ref)

    @pl.when(step < last)
    def _():
        mods = [_load_mod(r, step // n_l) for r in (sh_ref, sc_ref, gt_ref)]
        _ffn_rows(x_ref, o_ref, mods, False, *weights)

    @pl.when(step == last)
    def _():
        mods = [_load_mod(r, None) for r in (shs_ref, scs_ref, gts_ref)]
        _ffn_rows(xs_ref, os_ref, mods, True, *weights)


def _ffn(x, xs, ada, sub, g_pre, g_post, w_in, w_out, *, prompt_row_block, jobs=()):
    n, l, _ = x.shape
    n_l = l // FFN_TM
    n_tiles = n * n_l
    tile = lambda i: (jnp.minimum(i, n_tiles - 1) // n_l, jnp.minimum(i, n_tiles - 1) % n_l, 0)
    whole = lambda i: (0, 0, 0)
    kern = functools.partial(_ffn_kernel, n_l=n_l, job_groups=tuple(j.static for j in jobs))
    cast_in, cast_out, cast_shapes = _cast_plumbing(jobs, lambda i: i)
    return pl.pallas_call(
        kern,
        grid=(n_tiles + 1,),
        in_specs=[pl.BlockSpec((1, FFN_TM, D_MODEL), tile)]
        + _mod_specs(sub, False, prompt_row_block)
        + [pl.BlockSpec(xs.shape, whole)]
        + _mod_specs(sub, True, prompt_row_block)
        + [_resident((1, D_MODEL)), _resident((1, D_MODEL)),
           _resident(w_in.shape), _resident(w_out.shape)]
        + cast_in,
        out_specs=[pl.BlockSpec((1, FFN_TM, D_MODEL), tile), pl.BlockSpec(xs.shape, whole)]
        + cast_out,
        out_shape=[jax.ShapeDtypeStruct(x.shape, F32), jax.ShapeDtypeStruct(xs.shape, F32)]
        + cast_shapes,
        compiler_params=_params(("arbitrary",)),
        name="ffn",
    )(x, ada, ada, ada, xs, ada, ada, ada, g_pre, g_post, w_in, w_out, *[j.src for j in jobs])


def _log_decay(hb, wz_ref, walpha_ref, balpha_ref):
    z = _dot(hb, wz_ref[...]).astype(BF16)
    xg = _dot(z, walpha_ref[...]) + balpha_ref[...]
    return jax.nn.log_sigmoid(xg) * (1.0 / GATE_TAU)


def _merge_and_project(y_gla, y_conv, sig_a, sig_b, wbo_ref, wmo_ref):
    pg = _dot(y_gla, wbo_ref[0, :, :D_MODEL])
    pc = _dot(y_conv, wbo_ref[1, :, :D_MODEL])
    merged = (sig_a * pg + sig_b * pc).astype(BF16)
    return _dot(merged, wmo_ref[:, :D_MODEL])


_PAIR_LEVELS = tuple(GLA_CHUNK >> i for i in range(1, GLA_CHUNK.bit_length()))


def _pair_operands(qs, k, b, log_a, m, tm):
    if m >= SUBLANES:
        halves = lambda a: a.reshape(tm // (2 * m), 2, m, QK_WIDTH)
        b4, k4, q4 = halves(b), halves(k), halves(qs)
        mid = b4[:, 0:1, m - 1:m, :]
        w = jnp.exp(jnp.concatenate([mid - b4[:, 0:1], b4[:, 1:2] - mid], axis=1))
        src = jnp.concatenate([k4[:, 0:1], q4[:, 1:2]], axis=1)
        return (src * w).astype(BF16).reshape(tm, QK_WIDTH)
    tiles = lambda a: a.reshape(tm // SUBLANES, SUBLANES, QK_WIDTH)
    sub = lax.broadcasted_iota(jnp.int32, (1, SUBLANES, QK_WIDTH), 1)
    second = (sub & m) != 0
    if m == 1:
        w = jnp.exp(jnp.where(second, tiles(log_a), 0.0))
    else:
        b4 = tiles(b)
        mid = b4[:, m - 1:m, :]
        for start in range(2 * m, SUBLANES, 2 * m):
            mid = jnp.where(sub >= start, b4[:, start + m - 1:start + m, :], mid)
        w = jnp.exp((b4 - mid) * jnp.where(second, 1.0, -1.0))
    return (jnp.where(second, tiles(qs), tiles(k)) * w).astype(BF16).reshape(tm, QK_WIDTH)


def _pair_mask(m):
    t = lax.broadcasted_iota(jnp.int32, (GLA_CHUNK, GLA_CHUNK), 0)
    s = lax.broadcasted_iota(jnp.int32, (GLA_CHUNK, GLA_CHUNK), 1)
    return ((t & m) != 0) & ((s & m) == 0) & ((t // (2 * m)) == (s // (2 * m)))


def _mixer_kernel(x_ref, sh_ref, sc_ref, gt_ref, gpre_ref, gpost_ref, wa_ref, wz_ref, wb_ref,
                  walpha_ref, balpha_ref, ggla_ref, wconv_ref, wbo_ref, wmo_ref, *rest,
                  job_groups):
    n_jobs = len(job_groups)
    y_ref, sgla_ref, sconv_ref = rest[n_jobs:n_jobs + 3]
    st_scr, u_scr, ygla_scr = rest[-3:]
    l = pl.program_id(1)
    n_l = pl.num_programs(1)
    _run_casts(job_groups, pl.program_id(0) * n_l + l, rest[:n_jobs], rest[n_jobs + 3:-3])

    @pl.when(l == 0)
    def _():
        st_scr[...] = jnp.zeros_like(st_scr)
        u_scr[0:SUBLANES, :] = jnp.zeros((SUBLANES, D_CONV), F32)

    mods = [_load_mod(r, pl.program_id(0)) for r in (sh_ref, sc_ref, gt_ref)]
    weights = (gpre_ref, gpost_ref, wa_ref, wz_ref, wb_ref, walpha_ref, balpha_ref, ggla_ref,
               wconv_ref, wbo_ref, wmo_ref)
    n_parts = x_ref.shape[1] // MIX_PART
    parts = [_mixer_part(x_ref, y_ref, mods, weights, (st_scr, u_scr, ygla_scr),
                         slice(i * MIX_PART, (i + 1) * MIX_PART)) for i in range(n_parts)]
    next(parts[0])
    for i, p in enumerate(parts):
        next(p)
        if i + 1 < n_parts:
            next(parts[i + 1])
        if i > 0:
            next(parts[i - 1], None)
        next(p)
    next(parts[-1], None)

    @pl.when(l == n_l - 1)
    def _():
        sgla_ref[0, 0] = st_scr[...]
        sconv_ref[0, 0] = u_scr[SUBLANES - 2:SUBLANES, :]


def _mixer_part(x_ref, y_ref, mods, weights, scratch, prow):
    shift, scale, gate = mods
    (gpre_ref, gpost_ref, wa_ref, wz_ref, wb_ref, walpha_ref, balpha_ref, ggla_ref, wconv_ref,
     wbo_ref, wmo_ref) = weights
    st_scr, u_scr, ygla_scr = scratch
    ygla_scr = ygla_scr.at[prow]
    tm = MIX_PART
    x = x_ref[0, prow, :]
    hb = (_rms(x, gpre_ref[...]) * (1.0 + scale) + shift).astype(BF16)
    yield

    def proj_b(j):
        return _dot(hb, wb_ref[:, j * D_MODEL:(j + 1) * D_MODEL])

    nc = tm // GLA_CHUNK
    log_a = _log_decay(hb, wz_ref, walpha_ref, balpha_ref)
    pa = _dot(hb, wa_ref[:, :_A_WIDTH])
    qs = pa[:, 0:QK_WIDTH] * (DK ** -0.5)
    k = pa[:, QK_WIDTH:2 * QK_WIDTH]
    vb = pa[:, 2 * QK_WIDTH:2 * QK_WIDTH + V_WIDTH].astype(BF16)
    r = pa[:, 2 * QK_WIDTH + V_WIDTH:]

    row = lax.broadcasted_iota(jnp.int32, (tm, tm), 0)
    col = lax.broadcasted_iota(jnp.int32, (tm, tm), 1)
    tri = jnp.where((row >= col) & (row // GLA_CHUNK == col // GLA_CHUNK), 1.0, 0.0).astype(BF16)
    la_hi = log_a.astype(BF16)
    rem = log_a - la_hi.astype(F32)
    la_mid = rem.astype(BF16)
    la_lo = (rem - la_mid.astype(F32)).astype(BF16)
    b3 = (_dot(tri, la_hi) + _dot(tri, la_mid) + _dot(tri, la_lo)).reshape(
        nc, GLA_CHUNK, QK_WIDTH)

    u = proj_b(1) * proj_b(2)
    u_scr[SUBLANES:SUBLANES + tm, :] = u
    yield

    b_last = b3[:, GLA_CHUNK - 1:GLA_CHUNK, :]
    b = b3.reshape(tm, QK_WIDTH)
    qd = (qs * jnp.exp(b)).astype(BF16).reshape(nc, GLA_CHUNK, QK_WIDTH)
    kl = (k.reshape(nc, GLA_CHUNK, QK_WIDTH) * jnp.exp(b_last - b3)).astype(BF16)
    b_last_rows = jnp.concatenate(
        [b_last[c, :, hd * DK:(hd + 1) * DK] for c in range(nc) for hd in range(N_HEADS)]
        + [jnp.zeros((LANES - nc * N_HEADS, DK), F32)], axis=0)
    decay_cols = jnp.exp(b_last_rows.T)

    pair_ops = [_pair_operands(qs, k, b, log_a, m, tm) for m in _PAIR_LEVELS]
    pair_masks = [_pair_mask(m) for m in _PAIR_LEVELS]
    qs_b, k_b = qs.astype(BF16), k.astype(BF16)
    crow = lax.broadcasted_iota(jnp.int32, (GLA_CHUNK, GLA_CHUNK), 0)
    ccol = lax.broadcasted_iota(jnp.int32, (GLA_CHUNK, GLA_CHUNK), 1)
    scores, upd = {}, {}
    for c in range(nc):
        rows = slice(c * GLA_CHUNK, (c + 1) * GLA_CHUNK)
        for hd in range(N_HEADS):
            kc = slice(hd * DK, (hd + 1) * DK)
            s = jnp.where(crow == ccol, _dot_nt(qs_b[rows, kc], k_b[rows, kc]), 0.0)
            for p, mask in zip(pair_ops, pair_masks):
                s = jnp.where(mask, _dot_nt(p[rows, kc], p[rows, kc]), s)
            scores[c, hd] = s.astype(BF16)
            upd[c, hd] = _dot_tn(kl[c, :, kc], vb[rows, hd * DV:(hd + 1) * DV])

    conv = (wconv_ref[0:1, :] * u_scr[SUBLANES - 2:SUBLANES - 2 + tm, :]
            + wconv_ref[1:2, :] * u_scr[SUBLANES - 1:SUBLANES - 1 + tm, :]
            + wconv_ref[2:3, :] * u)
    u_scr[0:SUBLANES, :] = u_scr[tm:tm + SUBLANES, :]
    y_conv = (proj_b(0) * conv).astype(BF16)

    outs = {}
    for hd in range(N_HEADS):
        kc = slice(hd * DK, (hd + 1) * DK)
        vc = slice(hd * DV, (hd + 1) * DV)
        st = st_scr[hd]
        for c in range(nc):
            rows = slice(c * GLA_CHUNK, (c + 1) * GLA_CHUNK)
            lhs = jnp.concatenate([qd[c, :, kc], scores[c, hd]], axis=1)
            rhs = jnp.concatenate([st.astype(BF16), vb[rows, vc]], axis=0)
            outs[c, hd] = _dot(lhs, rhs)
            j = c * N_HEADS + hd
            st = st * decay_cols[:, j:j + 1] + upd[c, hd]
        st_scr[hd] = st

    sig_a = jax.nn.sigmoid(proj_b(3))
    sig_b = jax.nn.sigmoid(proj_b(4))

    for hd in range(N_HEADS):
        vc = slice(hd * DV, (hd + 1) * DV)
        for c in range(nc):
            rows = slice(c * GLA_CHUNK, (c + 1) * GLA_CHUNK)
            on = _rms(outs[c, hd], ggla_ref[:, vc])
            ygla_scr[rows, vc] = (on * _silu(r[rows, vc])).astype(BF16)

    yield
    mix = _merge_and_project(ygla_scr[...], y_conv, sig_a, sig_b, wbo_ref, wmo_ref)
    y_ref[0, prow, :] = x + gate * _rms(mix, gpost_ref[...])


def _mixer(x, ada, g_pre, g_post, wts, *, prompt_row_block, jobs=()):
    n, l, _ = x.shape
    tm = MIX_TM
    n_l = l // tm
    kern = functools.partial(_mixer_kernel, job_groups=tuple(j.static for j in jobs))
    cast_in, cast_out, cast_shapes = _cast_plumbing(jobs, lambda n, l: n * n_l + l)
    wa, wz, wb, walpha, balpha, ggla, wconv, wbo, wmo = wts
    return pl.pallas_call(
        kern,
        grid=(n, n_l),
        in_specs=[pl.BlockSpec((1, tm, D_MODEL), lambda n, l: (n, l, 0))]
        + _mod_specs(1, False, prompt_row_block)
        + [_resident(a.shape) for a in (g_pre, g_post, wa, wz, wb, walpha, balpha, ggla, wconv,
                                        wbo, wmo)]
        + cast_in,
        out_specs=[
            pl.BlockSpec((1, tm, D_MODEL), lambda n, l: (n, l, 0)),
            pl.BlockSpec((1, 1, N_HEADS, DK, DV), lambda n, l: (0, n, 0, 0, 0)),
            pl.BlockSpec((1, 1, CONV_WIDTH - 1, D_CONV), lambda n, l: (0, n, 0, 0)),
        ] + cast_out,
        out_shape=[
            jax.ShapeDtypeStruct(x.shape, F32),
            jax.ShapeDtypeStruct((1, n, N_HEADS, DK, DV), F32),
            jax.ShapeDtypeStruct((1, n, CONV_WIDTH - 1, D_CONV), F32),
        ] + cast_shapes,
        scratch_shapes=[
            pltpu.VMEM((N_HEADS, DK, DV), F32),
            pltpu.VMEM((MIX_PART + SUBLANES, D_CONV), F32),
            pltpu.VMEM((tm, V_WIDTH), BF16),
        ],
        compiler_params=_params(("arbitrary", "arbitrary")),
        name="mixer",
    )(x, ada, ada, ada, g_pre, g_post, wa, wz, wb, walpha, balpha, ggla, wconv, wbo, wmo,
      *[j.src for j in jobs])


def _smix_pre_kernel(x_ref, sh_ref, sc_ref, gpre_ref, wa_ref, wz_ref, wb_ref, walpha_ref,
                     balpha_ref, wconv_ref, sconv_ref,
                     qt_ref, kt_ref, at_ref, v_ref, sr_ref, yconv_ref, sga_ref, sgb_ref,
                     sconv_new_ref):
    x = x_ref[...]
    hb = (_rms(x, gpre_ref[...]) * (1.0 + sc_ref[...]) + sh_ref[...]).astype(BF16)
    pa = _dot(hb, wa_ref[:, :_A_WIDTH])
    q = pa[:, 0:QK_WIDTH] * (DK ** -0.5)
    k = pa[:, QK_WIDTH:2 * QK_WIDTH]
    v_ref[...] = pa[:, 2 * QK_WIDTH:2 * QK_WIDTH + V_WIDTH]
    sr_ref[...] = _silu(pa[:, 2 * QK_WIDTH + V_WIDTH:])
    a = jnp.exp(_log_decay(hb, wz_ref, walpha_ref, balpha_ref))
    for hd in range(N_HEADS):
        kc = slice(hd * DK, (hd + 1) * DK)
        qt_ref[hd] = q[:, kc].T
        kt_ref[hd] = k[:, kc].T
        at_ref[hd] = a[:, kc].T

    pb = _dot(hb, wb_ref[:, :_B_WIDTH])
    u = pb[:, D_CONV:2 * D_CONV] * pb[:, 2 * D_CONV:3 * D_CONV]
    conv = (wconv_ref[0:1, :] * sconv_ref[0] + wconv_ref[1:2, :] * sconv_ref[1]
            + wconv_ref[2:3, :] * u)
    yconv_ref[...] = pb[:, 0:D_CONV] * conv
    sga_ref[...] = jax.nn.sigmoid(pb[:, 3 * D_CONV:3 * D_CONV + D_MODEL])
    sgb_ref[...] = jax.nn.sigmoid(pb[:, 3 * D_CONV + D_MODEL:])
    sconv_new_ref[0] = sconv_ref[1]
    sconv_new_ref[1] = u


def _smix_state_kernel(s_ref, qt_ref, kt_ref, at_ref, v_ref, snew_ref, o_ref):
    base = pl.program_id(0) * SAMPLE_BLOCK
    lane = lax.broadcasted_iota(jnp.int32, (1, LANES), 1)
    for i in range(SAMPLE_BLOCK):
        pick = jnp.where(lane == base + i, 1.0, 0.0)
        for hd in range(N_HEADS):
            vc = slice(hd * DV, (hd + 1) * DV)
            a_col = jnp.sum(at_ref[hd] * pick, axis=1, keepdims=True)
            k_col = jnp.sum(kt_ref[hd] * pick, axis=1, keepdims=True)
            q_col = jnp.sum(qt_ref[hd] * pick, axis=1, keepdims=True)
            s_new = a_col * s_ref[0, i, hd] + k_col * v_ref[i:i + 1, vc]
            snew_ref[0, i, hd] = s_new
            o_ref[i:i + 1, vc] = jnp.sum(q_col * s_new, axis=0, keepdims=True)


def _smix_post_kernel(x_ref, gt_ref, gpost_ref, o_ref, sr_ref, yconv_ref, sga_ref, sgb_ref,
                      ggla_ref, wbo_ref, wmo_ref, y_ref, ygla_scr):
    for hd in range(N_HEADS):
        vc = slice(hd * DV, (hd + 1) * DV)
        on = _rms(o_ref[:, vc], ggla_ref[:, vc])
        ygla_scr[:, vc] = (on * sr_ref[:, vc]).astype(BF16)
    mix = _merge_and_project(ygla_scr[...], yconv_ref[...].astype(BF16), sga_ref[...],
                             sgb_ref[...], wbo_ref, wmo_ref)
    y_ref[...] = x_ref[...] + gt_ref[...] * _rms(mix, gpost_ref[...])


def _sample_mixer(x, ada, state_gla, sconv_t, g_pre, g_post, wts):
    s = x.shape[0]
    wa, wz, wb, walpha, balpha, ggla, wconv, wbo, wmo = wts
    whole = lambda a: pl.BlockSpec(a.shape, lambda *_: (0,) * a.ndim)
    mod = lambda j: pl.BlockSpec((s, D_MODEL), lambda *_: (0, 3 + j))
    tok = jax.ShapeDtypeStruct((s, D_MODEL), F32)
    tr = jax.ShapeDtypeStruct((N_HEADS, DK, s), F32)

    pre_in = (x, ada, ada, g_pre, wa, wz, wb, walpha, balpha, wconv, sconv_t)
    pre_specs = [whole(x), mod(0), mod(1)] + [whole(a) for a in pre_in[3:]]
    pre_out = [tr, tr, tr, tok, tok, tok, tok, tok, jax.ShapeDtypeStruct(sconv_t.shape, F32)]
    qt, kt, at, v, sr, yconv, sga, sgb, sconv_new = pl.pallas_call(
        _smix_pre_kernel,
        grid=(1,),
        in_specs=pre_specs,
        out_specs=[whole(o) for o in pre_out],
        out_shape=pre_out,
        compiler_params=_params(("arbitrary",)),
        name="smix_pre",
    )(*pre_in)

    state_block = pl.BlockSpec((1, SAMPLE_BLOCK, N_HEADS, DK, DV), lambda j: (0, j, 0, 0, 0))
    snew, o = pl.pallas_call(
        _smix_state_kernel,
        grid=(s // SAMPLE_BLOCK,),
        in_specs=[state_block, whole(qt), whole(kt), whole(at),
                  pl.BlockSpec((SAMPLE_BLOCK, V_WIDTH), lambda j: (j, 0))],
        out_specs=[state_block, pl.BlockSpec((SAMPLE_BLOCK, V_WIDTH), lambda j: (j, 0))],
        out_shape=[jax.ShapeDtypeStruct(state_gla.shape, F32), tok],
        compiler_params=_params(("arbitrary",)),
        name="smix_state",
    )(state_gla, qt, kt, at, v)

    post_in = (x, ada, g_post, o, sr, yconv, sga, sgb, ggla, wbo, wmo)
    post_specs = [whole(x), mod(2)] + [whole(a) for a in post_in[2:]]
    y = pl.pallas_call(
        _smix_post_kernel,
        grid=(1,),
        in_specs=post_specs,
        out_specs=whole(tok),
        out_shape=tok,
        scratch_shapes=[pltpu.VMEM((s, V_WIDTH), BF16)],
        compiler_params=_params(("arbitrary",)),
        name="smix_post",
    )(*post_in)
    return y, snew, sconv_new


def kernel(x_prompt, x_sample, state_gla, state_conv, c_prompt, c_sample, w_ada, b_ada, g_pre,
           g_post, w_ffn1_in, w_ffn1_out, w_ffn2_in, w_ffn2_out, w_mix_in, w_alpha, b_alpha,
           g_gla_norm, w_conv, w_branch_out, w_mix_out):
    depth = w_ada.shape[0]
    n_s = x_sample.shape[0]
    assert depth == 1 and x_sample.shape[1] == 1 and n_s == LANES
    assert c_prompt.shape[0] == SUBLANES and n_s % SUBLANES == 0
    prompt_row_block = n_s // SUBLANES

    yp = x_prompt
    ys = x_sample.reshape(1, n_s, D_MODEL)
    c_all = jnp.concatenate([c_sample, c_prompt], axis=0)
    outs = []
    for i in range(depth):
        ada = _ada(c_all, w_ada[i], b_ada[i])
        gpre = [g_pre[i, j].reshape(1, D_MODEL) for j in range(N_SUBLAYERS)]
        gpost = [g_post[i, j].reshape(1, D_MODEL) for j in range(N_SUBLAYERS)]
        ffn = functools.partial(_ffn, prompt_row_block=prompt_row_block)

        ffn1_w = (w_ffn1_in[i].astype(BF16),
                  jnp.pad(w_ffn1_out[i].astype(BF16), ((0, 0), (0, LANES))))
        n_ffn_steps = yp.shape[0] * (yp.shape[1] // FFN_TM)
        n_mix_steps = yp.shape[0] * (yp.shape[1] // MIX_TM)
        w_mix_t = jnp.swapaxes(w_mix_in[i], 0, 1)
        mixer_jobs = (
            _CastJob(w_mix_t, 2 * LANES, _A_WIDTH // (2 * LANES), transpose=True, pad=1),
            _CastJob(w_mix_t, LANES, 1, transpose=True, row0=_A_WIDTH, valid=GATE_RANK),
            _CastJob(w_mix_t, 4 * LANES, _B_WIDTH // (4 * LANES), transpose=True,
                     row0=_A_WIDTH + GATE_RANK, pad=1),
            _whole_cast(w_branch_out[i].reshape(2 * D_MODEL, D_MODEL), n_ffn_steps, pad=LANES),
            _whole_cast(w_mix_out[i], n_ffn_steps, pad=LANES))
        assert all(j.n_steps <= n_ffn_steps for j in mixer_jobs)
        assert all((j.n_steps * j.block_rows if j.transpose else j.src.shape[1] + j.pad)
                   % (WEIGHT_PITCH_TILES * LANES) for j in mixer_jobs)
        yp, ys, wa, wz, wb, wbo, wmo = ffn(yp, ys, ada, 0, gpre[0], gpost[0], *ffn1_w,
                                           jobs=mixer_jobs)
        walpha = jnp.pad(w_alpha[i], ((0, LANES - GATE_RANK), (0, 0))).astype(BF16)
        mix_w = (wa, wz, wb, walpha, b_alpha[i].reshape(1, QK_WIDTH),
                 g_gla_norm[i].reshape(1, V_WIDTH),
                 jnp.pad(w_conv[i], ((0, SUBLANES - CONV_WIDTH), (0, 0))),
                 wbo.reshape(2, D_MODEL, D_MODEL + LANES), wmo)
        sconv_t = jnp.swapaxes(state_conv[i], 0, 1)
        ys, gla_s, sconv_new = _sample_mixer(ys[0], ada, state_gla[i:i + 1], sconv_t,
                                             gpre[1], gpost[1], mix_w)
        yp, gla_p, conv_p, *ffn2_w = _mixer(
            yp, ada, gpre[1], gpost[1], mix_w, prompt_row_block=prompt_row_block,
            jobs=(_whole_cast(w_ffn2_in[i], n_mix_steps),
                  _whole_cast(w_ffn2_out[i], n_mix_steps, pad=LANES)))
        yp, ys = ffn(yp, ys.reshape(1, n_s, D_MODEL), ada, 2, gpre[2], gpost[2], *ffn2_w)
        outs.append((gla_p, conv_p, gla_s, jnp.swapaxes(sconv_new, 0, 1)[None]))

    gla_p, conv_p, gla_s, conv_s = outs[0]
    return (yp, ys.reshape(n_s, 1, D_MODEL), gla_p, conv_p, gla_s, conv_s)
```

```python
import functools
from typing import NamedTuple

import jax
import jax.numpy as jnp
from jax import lax
from jax.experimental import pallas as pl
from jax.experimental.pallas import tpu as pltpu

F32 = jnp.float32
BF16 = jnp.bfloat16

D_MODEL = 1024
D_FF = 2816
N_SUBLAYERS = 3
N_HEADS = 4
DK = 128
DV = 256
QK_WIDTH = N_HEADS * DK
V_WIDTH = N_HEADS * DV
GATE_RANK = 16
GATE_TAU = 16.0
D_CONV = D_MODEL
CONV_WIDTH = 3
RMS_EPS = 1e-6
MACARON_WEIGHT = 0.5

LANES = 128
SUBLANES = 8
GLA_CHUNK = 64
VMEM_LIMIT_BYTES = 56 * 1024 * 1024

FFN_TM = 1024
FFN_PART = 256
MIX_TM = 512
MIX_PART = 256
ADA_TN = 3072
SAMPLE_BLOCK = 16

_A_WIDTH = 2 * QK_WIDTH + 2 * V_WIDTH
_B_WIDTH = 3 * D_CONV + 2 * D_MODEL


def _rms(x, g):
    return x * lax.rsqrt(jnp.mean(x * x, axis=-1, keepdims=True) + RMS_EPS) * g


def _silu(x):
    return x * jax.nn.sigmoid(x)


def _dot(a, b):
    return jnp.dot(a, b, preferred_element_type=F32)


def _dot_nt(a, b):
    return lax.dot_general(a, b, (((1,), (1,)), ((), ())), preferred_element_type=F32)


def _dot_tn(a, b):
    return lax.dot_general(a, b, (((0,), (0,)), ((), ())), preferred_element_type=F32)


def _resident(shape):
    zeros = (0,) * len(shape)
    return pl.BlockSpec(shape, lambda *_: zeros, pipeline_mode=pl.Buffered(1))


def _params(semantics):
    return pltpu.CompilerParams(dimension_semantics=semantics, vmem_limit_bytes=VMEM_LIMIT_BYTES)


def _ada_kernel(c_ref, w_ref, b_ref, o_ref):
    c = c_ref[...]
    o_ref[...] = _dot(_silu(c).astype(BF16), w_ref[...].astype(BF16)) + b_ref[...]


def _ada(c_all, w_ada, b_ada):
    rows = c_all.shape[0]
    width = w_ada.shape[1]
    return pl.pallas_call(
        _ada_kernel,
        grid=(width // ADA_TN,),
        in_specs=[
            pl.BlockSpec((rows, D_MODEL), lambda j: (0, 0)),
            pl.BlockSpec((D_MODEL, ADA_TN), lambda j: (0, j)),
            pl.BlockSpec((1, ADA_TN), lambda j: (0, j)),
        ],
        out_specs=pl.BlockSpec((rows, ADA_TN), lambda j: (0, j)),
        out_shape=jax.ShapeDtypeStruct((rows, width), F32),
        compiler_params=_params(("arbitrary",)),
        name="ada",
    )(c_all, w_ada, b_ada.reshape(1, width))


def _mod_specs(sub, per_token, prompt_row_block):
    specs = []
    for j in range(3):
        col = sub * 3 + j
        if per_token:
            specs.append(pl.BlockSpec((LANES, D_MODEL), lambda *_, col=col: (0, col)))
        else:
            specs.append(pl.BlockSpec((SUBLANES, D_MODEL),
                                      lambda *_, col=col: (prompt_row_block, col)))
    return specs


def _load_mod(ref, row):
    if row is None:
        return ref[...]
    return ref[pl.ds(row, 1), :]


BF16_SUBLANES = 16
WEIGHT_PITCH_TILES = 8
MIX_QKVR_CAST_ROWS = 2 * LANES
MIX_CONV_CAST_ROWS = 4 * LANES


class _CastJob(NamedTuple):
    src: jax.Array
    block_rows: int
    n_blocks: int
    transpose: bool = False
    row0: int = 0
    valid: int = 0
    pad: int = 0

    @property
    def n_steps(self):
        return self.n_blocks + (self.pad if self.transpose else 0)

    def specs(self, step_of):
        cols = self.src.shape[1]
        r, nb = self.block_rows, self.n_blocks
        block = lambda *g: jnp.minimum(step_of(*g), nb - 1)
        if not self.transpose:
            return (pl.BlockSpec((r, cols), lambda *g: (block(*g), 0)),
                    pl.BlockSpec((r, cols + self.pad), lambda *g: (block(*g), 0)),
                    jax.ShapeDtypeStruct((nb * r, cols + self.pad), BF16))
        assert self.row0 % SUBLANES == 0 and r % SUBLANES == 0
        out_block = lambda *g: jnp.minimum(step_of(*g), self.n_steps - 1)
        return (pl.BlockSpec((pl.Element(r), pl.Element(cols)),
                             lambda *g: (pl.multiple_of(self.row0 + r * block(*g), SUBLANES), 0)),
                pl.BlockSpec((cols, r), lambda *g: (0, out_block(*g))),
                jax.ShapeDtypeStruct((cols, self.n_steps * r), BF16))

    @property
    def static(self):
        return (self.transpose, self.valid or self.block_rows, self.n_blocks, self.pad)


def _whole_cast(w, n_steps, pad=0):
    rows = w.shape[0]
    r = next(r for r in range(BF16_SUBLANES, rows + 1, BF16_SUBLANES)
             if rows % r == 0 and rows // r <= n_steps)
    return _CastJob(w, r, rows // r, pad=pad)


def _cast_plumbing(jobs, step_of):
    specs = [job.specs(step_of) for job in jobs]
    return ([i for i, _, _ in specs], [o for _, o, _ in specs], [s for _, _, s in specs])


def _run_casts(statics, step, src_refs, dst_refs):
    for (transpose, valid, n_blocks, pad), src, dst in zip(statics, src_refs, dst_refs,
                                                           strict=True):
        blk = src[...]
        if not transpose:
            cols = blk.shape[1]
            dst[:, :cols] = blk.astype(BF16)
            if pad:
                dst[:, cols:] = jnp.zeros((blk.shape[0], pad), BF16)
            continue
        blk = blk.T
        if valid < blk.shape[1]:
            lane = lax.broadcasted_iota(jnp.int32, blk.shape, 1)
            blk = jnp.where(lane < valid, blk, 0.0)
        if pad:
            blk = jnp.where(step < n_blocks, blk, 0.0)
        dst[...] = blk.astype(BF16)


def _ffn_rows(x_ref, o_ref, mods, per_token, gpre_ref, gpost_ref, win_ref, wout_ref):
    shift, scale, gate = mods
    tm = x_ref.shape[1]
    parts = max(tm // FFN_PART, 1)
    rows = [slice(i * (tm // parts), (i + 1) * (tm // parts)) for i in range(parts)]
    mod = lambda m, rs: m[rs] if per_token else m

    def pre(rs):
        return (_rms(x_ref[0, rs, :], gpre_ref[...]) * (1.0 + mod(scale, rs))
                + mod(shift, rs)).astype(BF16)

    def post(rs, out):
        o_ref[0, rs, :] = (x_ref[0, rs, :]
                           + MACARON_WEIGHT * mod(gate, rs) * _rms(out, gpost_ref[...]))

    h = pre(rows[0])
    for i in range(parts):
        g = _dot(h, win_ref[:, :D_FF])
        h_next = pre(rows[i + 1]) if i + 1 < parts else None
        u = _dot(h, win_ref[:, D_FF:])
        a = (_silu(g) * u).astype(BF16)
        if i > 0:
            post(rows[i - 1], out)
        out = _dot(a, wout_ref[:, :D_MODEL])
        h = h_next
    post(rows[parts - 1], out)


def _ffn_kernel(x_ref, sh_ref, sc_ref, gt_ref, xs_ref, shs_ref, scs_ref, gts_ref, gpre_ref,
                gpost_ref, win_ref, wout_ref, *rest, n_l, job_groups):
    n_jobs = len(job_groups)
    o_ref, os_ref = rest[n_jobs:n_jobs + 2]
    step = pl.program_id(0)
    _run_casts(job_groups, step, rest[:n_jobs], rest[n_jobs + 2:])
    last = pl.num_programs(0) - 1
    weights = (gpre_ref, gpost_ref, win_ref, wout_ref)

    @pl.when(step < last)
    def _():
        mods = [_load_mod(r, step // n_l) for r in (sh_ref, sc_ref, gt_ref)]
        _ffn_rows(x_ref, o_ref, mods, False, *weights)

    @pl.when(step == last)
    def _():
        mods = [_load_mod(r, None) for r in (shs_ref, scs_ref, gts_ref)]
        _ffn_rows(xs_ref, os_ref, mods, True, *weights)


def _ffn(x, xs, ada, sub, g_pre, g_post, w_in, w_out, *, prompt_row_block, jobs=()):
    n, l, _ = x.shape
    n_l = l // FFN_TM
    n_tiles = n * n_l
    tile = lambda i: (jnp.minimum(i, n_tiles - 1) // n_l, jnp.minimum(i, n_tiles - 1) % n_l, 0)
    whole = lambda i: (0, 0, 0)
    kern = functools.partial(_ffn_kernel, n_l=n_l, job_groups=tuple(j.static for j in jobs))
    cast_in, cast_out, cast_shapes = _cast_plumbing(jobs, lambda i: i)
    return pl.pallas_call(
        kern,
        grid=(n_tiles + 1,),
        in_specs=[pl.BlockSpec((1, FFN_TM, D_MODEL), tile)]
        + _mod_specs(sub, False, prompt_row_block)
        + [pl.BlockSpec(xs.shape, whole)]
        + _mod_specs(sub, True, prompt_row_block)
        + [_resident((1, D_MODEL)), _resident((1, D_MODEL)),
           _resident(w_in.shape), _resident(w_out.shape)]
        + cast_in,
        out_specs=[pl.BlockSpec((1, FFN_TM, D_MODEL), tile), pl.BlockSpec(xs.shape, whole)]
        + cast_out,
        out_shape=[jax.ShapeDtypeStruct(x.shape, F32), jax.ShapeDtypeStruct(xs.shape, F32)]
        + cast_shapes,
        compiler_params=_params(("arbitrary",)),
        name="ffn",
    )(x, ada, ada, ada, xs, ada, ada, ada, g_pre, g_post, w_in, w_out, *[j.src for j in jobs])


def _log_decay(hb, wz_ref, walpha_ref, balpha_ref):
    z = _dot(hb, wz_ref[...]).astype(BF16)
    xg = _dot(z, walpha_ref[...]) + balpha_ref[...]
    return jax.nn.log_sigmoid(xg) * (1.0 / GATE_TAU)


def _merge_and_project(y_gla, y_conv, sig_a, sig_b, wbo_ref, wmo_ref):
    pg = _dot(y_gla, wbo_ref[0, :, :D_MODEL])
    pc = _dot(y_conv, wbo_ref[1, :, :D_MODEL])
    merged = (sig_a * pg + sig_b * pc).astype(BF16)
    return _dot(merged, wmo_ref[:, :D_MODEL])


_PAIR_LEVELS = tuple(GLA_CHUNK >> i for i in range(1, GLA_CHUNK.bit_length()))


def _pair_operands(qs, k, b, log_a, m, tm):
    if m >= SUBLANES:
        halves = lambda a: a.reshape(tm // (2 * m), 2, m, QK_WIDTH)
        b4, k4, q4 = halves(b), halves(k), halves(qs)
        mid = b4[:, 0:1, m - 1:m, :]
        w = jnp.exp(jnp.concatenate([mid - b4[:, 0:1], b4[:, 1:2] - mid], axis=1))
        src = jnp.concatenate([k4[:, 0:1], q4[:, 1:2]], axis=1)
        return (src * w).astype(BF16).reshape(tm, QK_WIDTH)
    tiles = lambda a: a.reshape(tm // SUBLANES, SUBLANES, QK_WIDTH)
    sub = lax.broadcasted_iota(jnp.int32, (1, SUBLANES, QK_WIDTH), 1)
    second = (sub & m) != 0
    if m == 1:
        w = jnp.exp(jnp.where(second, tiles(log_a), 0.0))
    else:
        b4 = tiles(b)
        mid = b4[:, m - 1:m, :]
        for start in range(2 * m, SUBLANES, 2 * m):
            mid = jnp.where(sub >= start, b4[:, start + m - 1:start + m, :], mid)
        w = jnp.exp((b4 - mid) * jnp.where(second, 1.0, -1.0))
    return (jnp.where(second, tiles(qs), tiles(k)) * w).astype(BF16).reshape(tm, QK_WIDTH)


def _pair_mask(m):
    t = lax.broadcasted_iota(jnp.int32, (GLA_CHUNK, GLA_CHUNK), 0)
    s = lax.broadcasted_iota(jnp.int32, (GLA_CHUNK, GLA_CHUNK), 1)
    return ((t & m) != 0) & ((s & m) == 0) & ((t // (2 * m)) == (s // (2 * m)))


def _mixer_kernel(x_ref, sh_ref, sc_ref, gt_ref, gpre_ref, gpost_ref, wa_ref, wz_ref, wb_ref,
                  walpha_ref, balpha_ref, ggla_ref, wconv_ref, wbo_ref, wmo_ref, *rest,
                  job_groups):
    n_jobs = len(job_groups)
    y_ref, sgla_ref, sconv_ref = rest[n_jobs:n_jobs + 3]
    st_scr, u_scr, ygla_scr = rest[-3:]
    l = pl.program_id(1)
    n_l = pl.num_programs(1)
    _run_casts(job_groups, pl.program_id(0) * n_l + l, rest[:n_jobs], rest[n_jobs + 3:-3])

    @pl.when(l == 0)
    def _():
        st_scr[...] = jnp.zeros_like(st_scr)
        u_scr[0:SUBLANES, :] = jnp.zeros((SUBLANES, D_CONV), F32)

    mods = [_load_mod(r, pl.program_id(0)) for r in (sh_ref, sc_ref, gt_ref)]
    weights = (gpre_ref, gpost_ref, wa_ref, wz_ref, wb_ref, walpha_ref, balpha_ref, ggla_ref,
               wconv_ref, wbo_ref, wmo_ref)
    n_parts = x_ref.shape[1] // MIX_PART
    parts = [_mixer_part(x_ref, y_ref, mods, weights, (st_scr, u_scr, ygla_scr),
                         slice(i * MIX_PART, (i + 1) * MIX_PART)) for i in range(n_parts)]
    next(parts[0])
    for i, p in enumerate(parts):
        next(p)
        if i + 1 < n_parts:
            next(parts[i + 1])
        if i > 0:
            next(parts[i - 1], None)
        next(p)
    next(parts[-1], None)

    @pl.when(l == n_l - 1)
    def _():
        sgla_ref[0, 0] = st_scr[...]
        sconv_ref[0, 0] = u_scr[SUBLANES - 2:SUBLANES, :]


def _mixer_part(x_ref, y_ref, mods, weights, scratch, prow):
    shift, scale, gate = mods
    (gpre_ref, gpost_ref, wa_ref, wz_ref, wb_ref, walpha_ref, balpha_ref, ggla_ref, wconv_ref,
     wbo_ref, wmo_ref) = weights
    st_scr, u_scr, ygla_scr = scratch
    ygla_scr = ygla_scr.at[prow]
    tm = MIX_PART
    x = x_ref[0, prow, :]
    hb = (_rms(x, gpre_ref[...]) * (1.0 + scale) + shift).astype(BF16)
    yield

    def proj_b(j):
        return _dot(hb, wb_ref[:, j * D_MODEL:(j + 1) * D_MODEL])

    nc = tm // GLA_CHUNK
    log_a = _log_decay(hb, wz_ref, walpha_ref, balpha_ref)
    pa = _dot(hb, wa_ref[:, :_A_WIDTH])
    qs = pa[:, 0:QK_WIDTH] * (DK ** -0.5)
    k = pa[:, QK_WIDTH:2 * QK_WIDTH]
    vb = pa[:, 2 * QK_WIDTH:2 * QK_WIDTH + V_WIDTH].astype(BF16)
    r = pa[:, 2 * QK_WIDTH + V_WIDTH:]

    row = lax.broadcasted_iota(jnp.int32, (tm, tm), 0)
    col = lax.broadcasted_iota(jnp.int32, (tm, tm), 1)
    tri = jnp.where((row >= col) & (row // GLA_CHUNK == col // GLA_CHUNK), 1.0, 0.0).astype(BF16)
    la_hi = log_a.astype(BF16)
    rem = log_a - la_hi.astype(F32)
    la_mid = rem.astype(BF16)
    la_lo = (rem - la_mid.astype(F32)).astype(BF16)
    b3 = (_dot(tri, la_hi) + _dot(tri, la_mid) + _dot(tri, la_lo)).reshape(
        nc, GLA_CHUNK, QK_WIDTH)

    u = proj_b(1) * proj_b(2)
    u_scr[SUBLANES:SUBLANES + tm, :] = u
    yield

    b_last = b3[:, GLA_CHUNK - 1:GLA_CHUNK, :]
    b = b3.reshape(tm, QK_WIDTH)
    qd = (qs * jnp.exp(b)).astype(BF16).reshape(nc, GLA_CHUNK, QK_WIDTH)
    kl = (k.reshape(nc, GLA_CHUNK, QK_WIDTH) * jnp.exp(b_last - b3)).astype(BF16)
    b_last_rows = jnp.concatenate(
        [b_last[c, :, hd * DK:(hd + 1) * DK] for c in range(nc) for hd in range(N_HEADS)]
        + [jnp.zeros((LANES - nc * N_HEADS, DK), F32)], axis=0)
    decay_cols = jnp.exp(b_last_rows.T)

    pair_ops = [_pair_operands(qs, k, b, log_a, m, tm) for m in _PAIR_LEVELS]
    pair_masks = [_pair_mask(m) for m in _PAIR_LEVELS]
    qs_b, k_b = qs.astype(BF16), k.astype(BF16)
    crow = lax.broadcasted_iota(jnp.int32, (GLA_CHUNK, GLA_CHUNK), 0)
    ccol = lax.broadcasted_iota(jnp.int32, (GLA_CHUNK, GLA_CHUNK), 1)
    scores, upd = {}, {}
    for c in range(nc):
        rows = slice(c * GLA_CHUNK, (c + 1) * GLA_CHUNK)
        for hd in range(N_HEADS):
            kc = slice(hd * DK, (hd + 1) * DK)
            s = jnp.where(crow == ccol, _dot_nt(qs_b[rows, kc], k_b[rows, kc]), 0.0)
            for p, mask in zip(pair_ops, pair_masks):
                s = jnp.where(mask, _dot_nt(p[rows, kc], p[rows, kc]), s)
            scores[c, hd] = s.astype(BF16)
            upd[c, hd] = _dot_tn(kl[c, :, kc], vb[rows, hd * DV:(hd + 1) * DV])

    conv = (wconv_ref[0:1, :] * u_scr[SUBLANES - 2:SUBLANES - 2 + tm, :]
            + wconv_ref[1:2, :] * u_scr[SUBLANES - 1:SUBLANES - 1 + tm, :]
            + wconv_ref[2:3, :] * u)
    u_scr[0:SUBLANES, :] = u_scr[tm:tm + SUBLANES, :]
    y_conv = (proj_b(0) * conv).astype(BF16)

    outs = {}
    for hd in range(N_HEADS):
        kc = slice(hd * DK, (hd + 1) * DK)
        vc = slice(hd * DV, (hd + 1) * DV)
        st = st_scr[hd]
        for c in range(nc):
            rows = slice(c * GLA_CHUNK, (c + 1) * GLA_CHUNK)
            lhs = jnp.concatenate([qd[c, :, kc], scores[c, hd]], axis=1)
            rhs = jnp.concatenate([st.astype(BF16), vb[rows, vc]], axis=0)
            outs[c, hd] = _dot(lhs, rhs)
            j = c * N_HEADS + hd
            st = st * decay_cols[:, j:j + 1] + upd[c, hd]
        st_scr[hd] = st

    sig_a = jax.nn.sigmoid(proj_b(3))
    sig_b = jax.nn.sigmoid(proj_b(4))

    for hd in range(N_HEADS):
        vc = slice(hd * DV, (hd + 1) * DV)
        for c in range(nc):
            rows = slice(c * GLA_CHUNK, (c + 1) * GLA_CHUNK)
            on = _rms(outs[c, hd], ggla_ref[:, vc])
            ygla_scr[rows, vc] = (on * _silu(r[rows, vc])).astype(BF16)

    yield
    mix = _merge_and_project(ygla_scr[...], y_conv, sig_a, sig_b, wbo_ref, wmo_ref)
    y_ref[0, prow, :] = x + gate * _rms(mix, gpost_ref[...])


def _mixer(x, ada, g_pre, g_post, wts, *, prompt_row_block, jobs=()):
    n, l, _ = x.shape
    tm = MIX_TM
    n_l = l // tm
    kern = functools.partial(_mixer_kernel, job_groups=tuple(j.static for j in jobs))
    cast_in, cast_out, cast_shapes = _cast_plumbing(jobs, lambda n, l: n * n_l + l)
    wa, wz, wb, walpha, balpha, ggla, wconv, wbo, wmo = wts
    return pl.pallas_call(
        kern,
        grid=(n, n_l),
        in_specs=[pl.BlockSpec((1, tm, D_MODEL), lambda n, l: (n, l, 0))]
        + _mod_specs(1, False, prompt_row_block)
        + [_resident(a.shape) for a in (g_pre, g_post, wa, wz, wb, walpha, balpha, ggla, wconv,
                                        wbo, wmo)]
        + cast_in,
        out_specs=[
            pl.BlockSpec((1, tm, D_MODEL), lambda n, l: (n, l, 0)),
            pl.BlockSpec((1, 1, N_HEADS, DK, DV), lambda n, l: (0, n, 0, 0, 0)),
            pl.BlockSpec((1, 1, CONV_WIDTH - 1, D_CONV), lambda n, l: (0, n, 0, 0)),
        ] + cast_out,
        out_shape=[
            jax.ShapeDtypeStruct(x.shape, F32),
            jax.ShapeDtypeStruct((1, n, N_HEADS, DK, DV), F32),
            jax.ShapeDtypeStruct((1, n, CONV_WIDTH - 1, D_CONV), F32),
        ] + cast_shapes,
        scratch_shapes=[
            pltpu.VMEM((N_HEADS, DK, DV), F32),
            pltpu.VMEM((MIX_PART + SUBLANES, D_CONV), F32),
            pltpu.VMEM((tm, V_WIDTH), BF16),
        ],
        compiler_params=_params(("arbitrary", "arbitrary")),
        name="mixer",
    )(x, ada, ada, ada, g_pre, g_post, wa, wz, wb, walpha, balpha, ggla, wconv, wbo, wmo,
      *[j.src for j in jobs])


def _smix_pre_kernel(x_ref, sh_ref, sc_ref, gpre_ref, wa_ref, wz_ref, wb_ref, walpha_ref,
                     balpha_ref, wconv_ref, sconv_ref,
                     qt_ref, kt_ref, at_ref, v_ref, sr_ref, yconv_ref, sga_ref, sgb_ref,
                     sconv_new_ref):
    x = x_ref[...]
    hb = (_rms(x, gpre_ref[...]) * (1.0 + sc_ref[...]) + sh_ref[...]).astype(BF16)
    pa = _dot(hb, wa_ref[:, :_A_WIDTH])
    q = pa[:, 0:QK_WIDTH] * (DK ** -0.5)
    k = pa[:, QK_WIDTH:2 * QK_WIDTH]
    v_ref[...] = pa[:, 2 * QK_WIDTH:2 * QK_WIDTH + V_WIDTH]
    sr_ref[...] = _silu(pa[:, 2 * QK_WIDTH + V_WIDTH:])
    a = jnp.exp(_log_decay(hb, wz_ref, walpha_ref, balpha_ref))
    for hd in range(N_HEADS):
        kc = slice(hd * DK, (hd + 1) * DK)
        qt_ref[hd] = q[:, kc].T
        kt_ref[hd] = k[:, kc].T
        at_ref[hd] = a[:, kc].T

    pb = _dot(hb, wb_ref[:, :_B_WIDTH])
    u = pb[:, D_CONV:2 * D_CONV] * pb[:, 2 * D_CONV:3 * D_CONV]
    conv = (wconv_ref[0:1, :] * sconv_ref[:, 0, :] + wconv_ref[1:2, :] * sconv_ref[:, 1, :]
            + wconv_ref[2:3, :] * u)
    yconv_ref[...] = pb[:, 0:D_CONV] * conv
    sga_ref[...] = jax.nn.sigmoid(pb[:, 3 * D_CONV:3 * D_CONV + D_MODEL])
    sgb_ref[...] = jax.nn.sigmoid(pb[:, 3 * D_CONV + D_MODEL:])
    sconv_new_ref[:, 0, :] = sconv_ref[:, 1, :]
    sconv_new_ref[:, 1, :] = u


def _smix_state_kernel(s_ref, qt_ref, kt_ref, at_ref, v_ref, snew_ref, o_ref):
    base = pl.program_id(0) * SAMPLE_BLOCK
    lane = lax.broadcasted_iota(jnp.int32, (1, LANES), 1)
    for i in range(SAMPLE_BLOCK):
        pick = jnp.where(lane == base + i, 1.0, 0.0)
        for hd in range(N_HEADS):
            vc = slice(hd * DV, (hd + 1) * DV)
            a_col = jnp.sum(at_ref[hd] * pick, axis=1, keepdims=True)
            k_col = jnp.sum(kt_ref[hd] * pick, axis=1, keepdims=True)
            q_col = jnp.sum(qt_ref[hd] * pick, axis=1, keepdims=True)
            s_new = a_col * s_ref[0, i, hd] + k_col * v_ref[i:i + 1, vc]
            snew_ref[0, i, hd] = s_new
            o_ref[i:i + 1, vc] = jnp.sum(q_col * s_new, axis=0, keepdims=True)


def _smix_post_kernel(x_ref, gt_ref, gpost_ref, o_ref, sr_ref, yconv_ref, sga_ref, sgb_ref,
                      ggla_ref, wbo_ref, wmo_ref, y_ref, ygla_scr):
    for hd in range(N_HEADS):
        vc = slice(hd * DV, (hd + 1) * DV)
        on = _rms(o_ref[:, vc], ggla_ref[:, vc])
        ygla_scr[:, vc] = (on * sr_ref[:, vc]).astype(BF16)
    mix = _merge_and_project(ygla_scr[...], yconv_ref[...].astype(BF16), sga_ref[...],
                             sgb_ref[...], wbo_ref, wmo_ref)
    y_ref[...] = x_ref[...] + gt_ref[...] * _rms(mix, gpost_ref[...])


def _sample_mixer(x, ada, state_gla, state_conv, g_pre, g_post, wts):
    s = x.shape[0]
    wa, wz, wb, walpha, balpha, ggla, wconv, wbo, wmo = wts
    whole = lambda a: pl.BlockSpec(a.shape, lambda *_: (0,) * a.ndim)
    mod = lambda j: pl.BlockSpec((s, D_MODEL), lambda *_: (0, 3 + j))
    tok = jax.ShapeDtypeStruct((s, D_MODEL), F32)
    tr = jax.ShapeDtypeStruct((N_HEADS, DK, s), F32)

    pre_in = (x, ada, ada, g_pre, wa, wz, wb, walpha, balpha, wconv, state_conv)
    pre_specs = [whole(x), mod(0), mod(1)] + [whole(a) for a in pre_in[3:]]
    pre_out = [tr, tr, tr, tok, tok, tok, tok, tok, jax.ShapeDtypeStruct(state_conv.shape, F32)]
    qt, kt, at, v, sr, yconv, sga, sgb, sconv_new = pl.pallas_call(
        _smix_pre_kernel,
        grid=(1,),
        in_specs=pre_specs,
        out_specs=[whole(o) for o in pre_out],
        out_shape=pre_out,
        compiler_params=_params(("arbitrary",)),
        name="smix_pre",
    )(*pre_in)

    state_block = pl.BlockSpec((1, SAMPLE_BLOCK, N_HEADS, DK, DV), lambda j: (0, j, 0, 0, 0))
    snew, o = pl.pallas_call(
        _smix_state_kernel,
        grid=(s // SAMPLE_BLOCK,),
        in_specs=[state_block, whole(qt), whole(kt), whole(at),
                  pl.BlockSpec((SAMPLE_BLOCK, V_WIDTH), lambda j: (j, 0))],
        out_specs=[state_block, pl.BlockSpec((SAMPLE_BLOCK, V_WIDTH), lambda j: (j, 0))],
        out_shape=[jax.ShapeDtypeStruct(state_gla.shape, F32), tok],
        compiler_params=_params(("arbitrary",)),
        name="smix_state",
    )(state_gla, qt, kt, at, v)

    post_in = (x, ada, g_post, o, sr, yconv, sga, sgb, ggla, wbo, wmo)
    post_specs = [whole(x), mod(2)] + [whole(a) for a in post_in[2:]]
    y = pl.pallas_call(
        _smix_post_kernel,
        grid=(1,),
        in_specs=post_specs,
        out_specs=whole(tok),
        out_shape=tok,
        scratch_shapes=[pltpu.VMEM((s, V_WIDTH), BF16)],
        compiler_params=_params(("arbitrary",)),
        name="smix_post",
    )(*post_in)
    return y, snew, sconv_new


def kernel(x_prompt, x_sample, state_gla, state_conv, c_prompt, c_sample, w_ada, b_ada, g_pre,
           g_post, w_ffn1_in, w_ffn1_out, w_ffn2_in, w_ffn2_out, w_mix_in, w_alpha, b_alpha,
           g_gla_norm, w_conv, w_branch_out, w_mix_out):
    depth = w_ada.shape[0]
    n_s = x_sample.shape[0]
    assert depth == 1 and x_sample.shape[1] == 1 and n_s == LANES
    assert c_prompt.shape[0] == SUBLANES and n_s % SUBLANES == 0
    prompt_row_block = n_s // SUBLANES

    yp = x_prompt
    ys = x_sample.reshape(1, n_s, D_MODEL)
    c_all = jnp.concatenate([c_sample, c_prompt], axis=0)
    outs = []
    for i in range(depth):
        ada = _ada(c_all, w_ada[i], b_ada[i])
        gpre = [g_pre[i, j].reshape(1, D_MODEL) for j in range(N_SUBLAYERS)]
        gpost = [g_post[i, j].reshape(1, D_MODEL) for j in range(N_SUBLAYERS)]
        ffn = functools.partial(_ffn, prompt_row_block=prompt_row_block)

        ffn1_w = (w_ffn1_in[i].astype(BF16),
                  jnp.pad(w_ffn1_out[i].astype(BF16), ((0, 0), (0, LANES))))
        n_ffn_steps = yp.shape[0] * (yp.shape[1] // FFN_TM)
        n_mix_steps = yp.shape[0] * (yp.shape[1] // MIX_TM)
        w_mix_t = jnp.swapaxes(w_mix_in[i], 0, 1)
        mixer_jobs = (
            _CastJob(w_mix_t, MIX_QKVR_CAST_ROWS, _A_WIDTH // MIX_QKVR_CAST_ROWS, transpose=True,
                     pad=1),
            _CastJob(w_mix_t, LANES, 1, transpose=True, row0=_A_WIDTH, valid=GATE_RANK),
            _CastJob(w_mix_t, MIX_CONV_CAST_ROWS, _B_WIDTH // MIX_CONV_CAST_ROWS, transpose=True,
                     row0=_A_WIDTH + GATE_RANK, pad=1),
            _whole_cast(w_branch_out[i].reshape(2 * D_MODEL, D_MODEL), n_ffn_steps, pad=LANES),
            _whole_cast(w_mix_out[i], n_ffn_steps, pad=LANES))
        assert all(j.n_steps <= n_ffn_steps for j in mixer_jobs)
        assert all((j.n_steps * j.block_rows if j.transpose else j.src.shape[1] + j.pad)
                   % (WEIGHT_PITCH_TILES * LANES) for j in mixer_jobs)
        yp, ys, wa, wz, wb, wbo, wmo = ffn(yp, ys, ada, 0, gpre[0], gpost[0], *ffn1_w,
                                           jobs=mixer_jobs)
        walpha = jnp.pad(w_alpha[i], ((0, LANES - GATE_RANK), (0, 0))).astype(BF16)
        mix_w = (wa, wz, wb, walpha, b_alpha[i].reshape(1, QK_WIDTH),
                 g_gla_norm[i].reshape(1, V_WIDTH),
                 jnp.pad(w_conv[i], ((0, SUBLANES - CONV_WIDTH), (0, 0))),
                 wbo.reshape(2, D_MODEL, D_MODEL + LANES), wmo)
        ys, gla_s, sconv_new = _sample_mixer(ys[0], ada, state_gla[i:i + 1], state_conv[i],
                                             gpre[1], gpost[1], mix_w)
        yp, gla_p, conv_p, *ffn2_w = _mixer(
            yp, ada, gpre[1], gpost[1], mix_w, prompt_row_block=prompt_row_block,
            jobs=(_whole_cast(w_ffn2_in[i], n_mix_steps),
                  _whole_cast(w_ffn2_out[i], n_mix_steps, pad=LANES)))
        yp, ys = ffn(yp, ys.reshape(1, n_s, D_MODEL), ada, 2, gpre[2], gpost[2], *ffn2_w)
        outs.append((gla_p, conv_p, gla_s, sconv_new[None]))

    gla_p, conv_p, gla_s, conv_s = outs[0]
    return (yp, ys.reshape(n_s, 1, D_MODEL), gla_p, conv_p, gla_s, conv_s)
```

```python
import functools
from typing import NamedTuple

import jax
import jax.numpy as jnp
from jax import lax
from jax.experimental import pallas as pl
from jax.experimental.pallas import tpu as pltpu

F32 = jnp.float32
BF16 = jnp.bfloat16

D_MODEL = 1024
D_FF = 2816
N_SUBLAYERS = 3
N_HEADS = 4
DK = 128
DV = 256
QK_WIDTH = N_HEADS * DK
V_WIDTH = N_HEADS * DV
GATE_RANK = 16
GATE_TAU = 16.0
D_CONV = D_MODEL
CONV_WIDTH = 3
RMS_EPS = 1e-6
MACARON_WEIGHT = 0.5

LANES = 128
SUBLANES = 8
GLA_CHUNK = 64
VMEM_LIMIT_BYTES = 56 * 1024 * 1024

FFN_TM = 1024
FFN_PART = 256
MIX_TM = 512
MIX_PART = 256
ADA_TN = 3072
SAMPLE_BLOCK = 16

_A_WIDTH = 2 * QK_WIDTH + 2 * V_WIDTH
_B_WIDTH = 3 * D_CONV + 2 * D_MODEL


def _rms(x, g):
    return x * lax.rsqrt(jnp.mean(x * x, axis=-1, keepdims=True) + RMS_EPS) * g


def _silu(x):
    return x * jax.nn.sigmoid(x)


def _dot(a, b):
    return jnp.dot(a, b, preferred_element_type=F32)


def _dot_nt(a, b):
    return lax.dot_general(a, b, (((1,), (1,)), ((), ())), preferred_element_type=F32)


def _dot_tn(a, b):
    return lax.dot_general(a, b, (((0,), (0,)), ((), ())), preferred_element_type=F32)


def _resident(shape):
    zeros = (0,) * len(shape)
    return pl.BlockSpec(shape, lambda *_: zeros, pipeline_mode=pl.Buffered(1))


def _params(semantics):
    return pltpu.CompilerParams(dimension_semantics=semantics, vmem_limit_bytes=VMEM_LIMIT_BYTES)


def _ada_kernel(cs_ref, cp_ref, w_ref, b_ref, os_ref, op_ref):
    w = w_ref[...].astype(BF16)
    for c_ref, o_ref in ((cs_ref, os_ref), (cp_ref, op_ref)):
        o_ref[...] = _dot(_silu(c_ref[...]).astype(BF16), w) + b_ref[...]


def _ada(c_sample, c_prompt, w_ada, b_ada):
    width = w_ada.shape[1]
    rows = lambda c: pl.BlockSpec((c.shape[0], D_MODEL), lambda j: (0, 0))
    cols = lambda c: pl.BlockSpec((c.shape[0], ADA_TN), lambda j: (0, j))
    return pl.pallas_call(
        _ada_kernel,
        grid=(width // ADA_TN,),
        in_specs=[rows(c_sample), rows(c_prompt),
                  pl.BlockSpec((D_MODEL, ADA_TN), lambda j: (0, j)),
                  pl.BlockSpec((1, ADA_TN), lambda j: (0, j))],
        out_specs=[cols(c_sample), cols(c_prompt)],
        out_shape=[jax.ShapeDtypeStruct((c.shape[0], width), F32) for c in (c_sample, c_prompt)],
        compiler_params=_params(("arbitrary",)),
        name="ada",
    )(c_sample, c_prompt, w_ada, b_ada.reshape(1, width))


def _mod_specs(sub, ada):
    return [pl.BlockSpec((ada.shape[0], D_MODEL), lambda *_, col=sub * 3 + j: (0, col))
            for j in range(3)]


def _load_mod(ref, row):
    if row is None:
        return ref[...]
    return ref[pl.ds(row, 1), :]


BF16_SUBLANES = 16
WEIGHT_PITCH_TILES = 8
MIX_QKVR_CAST_ROWS = 2 * LANES
MIX_CONV_CAST_ROWS = 4 * LANES


class _CastJob(NamedTuple):
    src: jax.Array
    block_rows: int
    n_blocks: int
    transpose: bool = False
    row0: int = 0
    valid: int = 0
    pad: int = 0

    @property
    def n_steps(self):
        return self.n_blocks + (self.pad if self.transpose else 0)

    def specs(self, step_of):
        cols = self.src.shape[1]
        r, nb = self.block_rows, self.n_blocks
        block = lambda *g: jnp.minimum(step_of(*g), nb - 1)
        if not self.transpose:
            return (pl.BlockSpec((r, cols), lambda *g: (block(*g), 0)),
                    pl.BlockSpec((r, cols + self.pad), lambda *g: (block(*g), 0)),
                    jax.ShapeDtypeStruct((nb * r, cols + self.pad), BF16))
        assert self.row0 % SUBLANES == 0 and r % SUBLANES == 0
        out_block = lambda *g: jnp.minimum(step_of(*g), self.n_steps - 1)
        return (pl.BlockSpec((pl.Element(r), pl.Element(cols)),
                             lambda *g: (pl.multiple_of(self.row0 + r * block(*g), SUBLANES), 0)),
                pl.BlockSpec((cols, r), lambda *g: (0, out_block(*g))),
                jax.ShapeDtypeStruct((cols, self.n_steps * r), BF16))

    @property
    def static(self):
        return (self.transpose, self.valid or self.block_rows, self.n_blocks, self.pad)


def _whole_cast(w, n_steps, pad=0):
    rows = w.shape[0]
    r = next(r for r in range(BF16_SUBLANES, rows + 1, BF16_SUBLANES)
             if rows % r == 0 and rows // r <= n_steps)
    return _CastJob(w, r, rows // r, pad=pad)


def _cast_plumbing(jobs, step_of):
    specs = [job.specs(step_of) for job in jobs]
    return ([i for i, _, _ in specs], [o for _, o, _ in specs], [s for _, _, s in specs])


def _run_casts(statics, step, src_refs, dst_refs):
    for (transpose, valid, n_blocks, pad), src, dst in zip(statics, src_refs, dst_refs,
                                                           strict=True):
        blk = src[...]
        if not transpose:
            cols = blk.shape[1]
            dst[:, :cols] = blk.astype(BF16)
            if pad:
                dst[:, cols:] = jnp.zeros((blk.shape[0], pad), BF16)
            continue
        blk = blk.T
        if valid < blk.shape[1]:
            lane = lax.broadcasted_iota(jnp.int32, blk.shape, 1)
            blk = jnp.where(lane < valid, blk, 0.0)
        if pad:
            blk = jnp.where(step < n_blocks, blk, 0.0)
        dst[...] = blk.astype(BF16)


def _ffn_rows(x_ref, o_ref, mods, per_token, gpre_ref, gpost_ref, win_ref, wout_ref):
    shift, scale, gate = mods
    tm = x_ref.shape[1]
    parts = max(tm // FFN_PART, 1)
    rows = [slice(i * (tm // parts), (i + 1) * (tm // parts)) for i in range(parts)]
    mod = lambda m, rs: m[rs] if per_token else m

    def pre(rs):
        return (_rms(x_ref[0, rs, :], gpre_ref[...]) * (1.0 + mod(scale, rs))
                + mod(shift, rs)).astype(BF16)

    def post(rs, out):
        o_ref[0, rs, :] = (x_ref[0, rs, :]
                           + MACARON_WEIGHT * mod(gate, rs) * _rms(out, gpost_ref[...]))

    h = pre(rows[0])
    for i in range(parts):
        g = _dot(h, win_ref[:, :D_FF])
        h_next = pre(rows[i + 1]) if i + 1 < parts else None
        u = _dot(h, win_ref[:, D_FF:])
        a = (_silu(g) * u).astype(BF16)
        if i > 0:
            post(rows[i - 1], out)
        out = _dot(a, wout_ref[:, :D_MODEL])
        h = h_next
    post(rows[parts - 1], out)


def _ffn_kernel(x_ref, sh_ref, sc_ref, gt_ref, xs_ref, shs_ref, scs_ref, gts_ref, gpre_ref,
                gpost_ref, win_ref, wout_ref, *rest, n_l, job_groups):
    n_jobs = len(job_groups)
    o_ref, os_ref = rest[n_jobs:n_jobs + 2]
    step = pl.program_id(0)
    _run_casts(job_groups, step, rest[:n_jobs], rest[n_jobs + 2:])
    last = pl.num_programs(0) - 1
    weights = (gpre_ref, gpost_ref, win_ref, wout_ref)

    @pl.when(step < last)
    def _():
        mods = [_load_mod(r, step // n_l) for r in (sh_ref, sc_ref, gt_ref)]
        _ffn_rows(x_ref, o_ref, mods, False, *weights)

    @pl.when(step == last)
    def _():
        mods = [_load_mod(r, None) for r in (shs_ref, scs_ref, gts_ref)]
        _ffn_rows(xs_ref, os_ref, mods, True, *weights)


def _ffn(x, xs, ada_p, ada_s, sub, g_pre, g_post, w_in, w_out, *, jobs=()):
    n, l, _ = x.shape
    n_l = l // FFN_TM
    n_tiles = n * n_l
    tile = lambda i: (jnp.minimum(i, n_tiles - 1) // n_l, jnp.minimum(i, n_tiles - 1) % n_l, 0)
    whole = lambda i: (0, 0, 0)
    kern = functools.partial(_ffn_kernel, n_l=n_l, job_groups=tuple(j.static for j in jobs))
    cast_in, cast_out, cast_shapes = _cast_plumbing(jobs, lambda i: i)
    return pl.pallas_call(
        kern,
        grid=(n_tiles + 1,),
        in_specs=[pl.BlockSpec((1, FFN_TM, D_MODEL), tile)]
        + _mod_specs(sub, ada_p)
        + [pl.BlockSpec(xs.shape, whole)]
        + _mod_specs(sub, ada_s)
        + [_resident((1, D_MODEL)), _resident((1, D_MODEL)),
           _resident(w_in.shape), _resident(w_out.shape)]
        + cast_in,
        out_specs=[pl.BlockSpec((1, FFN_TM, D_MODEL), tile), pl.BlockSpec(xs.shape, whole)]
        + cast_out,
        out_shape=[jax.ShapeDtypeStruct(x.shape, F32), jax.ShapeDtypeStruct(xs.shape, F32)]
        + cast_shapes,
        compiler_params=_params(("arbitrary",)),
        name="ffn",
    )(x, ada_p, ada_p, ada_p, xs, ada_s, ada_s, ada_s, g_pre, g_post, w_in, w_out,
      *[j.src for j in jobs])


def _log_decay(hb, wz_ref, walpha_ref, balpha_ref):
    z = _dot(hb, wz_ref[...]).astype(BF16)
    xg = _dot(z, walpha_ref[...]) + balpha_ref[...]
    return jax.nn.log_sigmoid(xg) * (1.0 / GATE_TAU)


def _merge_and_project(y_gla, y_conv, sig_a, sig_b, wbo_ref, wmo_ref):
    pg = _dot(y_gla, wbo_ref[0, :, :D_MODEL])
    pc = _dot(y_conv, wbo_ref[1, :, :D_MODEL])
    merged = (sig_a * pg + sig_b * pc).astype(BF16)
    return _dot(merged, wmo_ref[:, :D_MODEL])


_PAIR_LEVELS = tuple(GLA_CHUNK >> i for i in range(1, GLA_CHUNK.bit_length()))


def _pair_operands(qs, k, b, log_a, m, tm):
    if m >= SUBLANES:
        halves = lambda a: a.reshape(tm // (2 * m), 2, m, QK_WIDTH)
        b4, k4, q4 = halves(b), halves(k), halves(qs)
        mid = b4[:, 0:1, m - 1:m, :]
        w = jnp.exp(jnp.concatenate([mid - b4[:, 0:1], b4[:, 1:2] - mid], axis=1))
        src = jnp.concatenate([k4[:, 0:1], q4[:, 1:2]], axis=1)
        return (src * w).astype(BF16).reshape(tm, QK_WIDTH)
    tiles = lambda a: a.reshape(tm // SUBLANES, SUBLANES, QK_WIDTH)
    sub = lax.broadcasted_iota(jnp.int32, (1, SUBLANES, QK_WIDTH), 1)
    second = (sub & m) != 0
    if m == 1:
        w = jnp.exp(jnp.where(second, tiles(log_a), 0.0))
    else:
        b4 = tiles(b)
        mid = b4[:, m - 1:m, :]
        for start in range(2 * m, SUBLANES, 2 * m):
            mid = jnp.where(sub >= start, b4[:, start + m - 1:start + m, :], mid)
        w = jnp.exp((b4 - mid) * jnp.where(second, 1.0, -1.0))
    return (jnp.where(second, tiles(qs), tiles(k)) * w).astype(BF16).reshape(tm, QK_WIDTH)


def _pair_mask(m):
    t = lax.broadcasted_iota(jnp.int32, (GLA_CHUNK, GLA_CHUNK), 0)
    s = lax.broadcasted_iota(jnp.int32, (GLA_CHUNK, GLA_CHUNK), 1)
    return ((t & m) != 0) & ((s & m) == 0) & ((t // (2 * m)) == (s // (2 * m)))


def _mixer_kernel(x_ref, sh_ref, sc_ref, gt_ref, gpre_ref, gpost_ref, wa_ref, wz_ref, wb_ref,
                  walpha_ref, balpha_ref, ggla_ref, wconv_ref, wbo_ref, wmo_ref, *rest,
                  job_groups):
    n_jobs = len(job_groups)
    y_ref, sgla_ref, sconv_ref = rest[n_jobs:n_jobs + 3]
    st_scr, u_scr, ygla_scr = rest[-3:]
    l = pl.program_id(1)
    n_l = pl.num_programs(1)
    _run_casts(job_groups, pl.program_id(0) * n_l + l, rest[:n_jobs], rest[n_jobs + 3:-3])

    @pl.when(l == 0)
    def _():
        st_scr[...] = jnp.zeros_like(st_scr)
        u_scr[0:SUBLANES, :] = jnp.zeros((SUBLANES, D_CONV), F32)

    mods = [_load_mod(r, pl.program_id(0)) for r in (sh_ref, sc_ref, gt_ref)]
    weights = (gpre_ref, gpost_ref, wa_ref, wz_ref, wb_ref, walpha_ref, balpha_ref, ggla_ref,
               wconv_ref, wbo_ref, wmo_ref)
    n_parts = x_ref.shape[1] // MIX_PART
    parts = [_mixer_part(x_ref, y_ref, mods, weights, (st_scr, u_scr, ygla_scr),
                         slice(i * MIX_PART, (i + 1) * MIX_PART)) for i in range(n_parts)]
    next(parts[0])
    for i, p in enumerate(parts):
        next(p)
        if i + 1 < n_parts:
            next(parts[i + 1])
        if i > 0:
            next(parts[i - 1], None)
        next(p)
    next(parts[-1], None)

    @pl.when(l == n_l - 1)
    def _():
        sgla_ref[0, 0] = st_scr[...]
        sconv_ref[0, 0] = u_scr[SUBLANES - 2:SUBLANES, :]


def _mixer_part(x_ref, y_ref, mods, weights, scratch, prow):
    shift, scale, gate = mods
    (gpre_ref, gpost_ref, wa_ref, wz_ref, wb_ref, walpha_ref, balpha_ref, ggla_ref, wconv_ref,
     wbo_ref, wmo_ref) = weights
    st_scr, u_scr, ygla_scr = scratch
    ygla_scr = ygla_scr.at[prow]
    tm = MIX_PART
    x = x_ref[0, prow, :]
    hb = (_rms(x, gpre_ref[...]) * (1.0 + scale) + shift).astype(BF16)
    yield

    def proj_b(j):
        return _dot(hb, wb_ref[:, j * D_MODEL:(j + 1) * D_MODEL])

    nc = tm // GLA_CHUNK
    log_a = _log_decay(hb, wz_ref, walpha_ref, balpha_ref)
    pa = _dot(hb, wa_ref[:, :_A_WIDTH])
    qs = pa[:, 0:QK_WIDTH] * (DK ** -0.5)
    k = pa[:, QK_WIDTH:2 * QK_WIDTH]
    vb = pa[:, 2 * QK_WIDTH:2 * QK_WIDTH + V_WIDTH].astype(BF16)
    r = pa[:, 2 * QK_WIDTH + V_WIDTH:]

    row = lax.broadcasted_iota(jnp.int32, (tm, tm), 0)
    col = lax.broadcasted_iota(jnp.int32, (tm, tm), 1)
    tri = jnp.where((row >= col) & (row // GLA_CHUNK == col // GLA_CHUNK), 1.0, 0.0).astype(BF16)
    la_hi = log_a.astype(BF16)
    rem = log_a - la_hi.astype(F32)
    la_mid = rem.astype(BF16)
    la_lo = (rem - la_mid.astype(F32)).astype(BF16)
    b3 = (_dot(tri, la_hi) + _dot(tri, la_mid) + _dot(tri, la_lo)).reshape(
        nc, GLA_CHUNK, QK_WIDTH)

    u = proj_b(1) * proj_b(2)
    u_scr[SUBLANES:SUBLANES + tm, :] = u
    yield

    b_last = b3[:, GLA_CHUNK - 1:GLA_CHUNK, :]
    b = b3.reshape(tm, QK_WIDTH)
    qd = (qs * jnp.exp(b)).astype(BF16).reshape(nc, GLA_CHUNK, QK_WIDTH)
    kl = (k.reshape(nc, GLA_CHUNK, QK_WIDTH) * jnp.exp(b_last - b3)).astype(BF16)
    b_last_rows = jnp.concatenate(
        [b_last[c, :, hd * DK:(hd + 1) * DK] for c in range(nc) for hd in range(N_HEADS)]
        + [jnp.zeros((LANES - nc * N_HEADS, DK), F32)], axis=0)
    decay_cols = jnp.exp(b_last_rows.T)

    pair_ops = [_pair_operands(qs, k, b, log_a, m, tm) for m in _PAIR_LEVELS]
    pair_masks = [_pair_mask(m) for m in _PAIR_LEVELS]
    qs_b, k_b = qs.astype(BF16), k.astype(BF16)
    crow = lax.broadcasted_iota(jnp.int32, (GLA_CHUNK, GLA_CHUNK), 0)
    ccol = lax.broadcasted_iota(jnp.int32, (GLA_CHUNK, GLA_CHUNK), 1)
    scores, upd = {}, {}
    for c in range(nc):
        rows = slice(c * GLA_CHUNK, (c + 1) * GLA_CHUNK)
        for hd in range(N_HEADS):
            kc = slice(hd * DK, (hd + 1) * DK)
            s = jnp.where(crow == ccol, _dot_nt(qs_b[rows, kc], k_b[rows, kc]), 0.0)
            for p, mask in zip(pair_ops, pair_masks):
                s = jnp.where(mask, _dot_nt(p[rows, kc], p[rows, kc]), s)
            scores[c, hd] = s.astype(BF16)
            upd[c, hd] = _dot_tn(kl[c, :, kc], vb[rows, hd * DV:(hd + 1) * DV])

    conv = (wconv_ref[0:1, :] * u_scr[SUBLANES - 2:SUBLANES - 2 + tm, :]
            + wconv_ref[1:2, :] * u_scr[SUBLANES - 1:SUBLANES - 1 + tm, :]
            + wconv_ref[2:3, :] * u)
    u_scr[0:SUBLANES, :] = u_scr[tm:tm + SUBLANES, :]
    y_conv = (proj_b(0) * conv).astype(BF16)

    outs = {}
    for hd in range(N_HEADS):
        kc = slice(hd * DK, (hd + 1) * DK)
        vc = slice(hd * DV, (hd + 1) * DV)
        st = st_scr[hd]
        for c in range(nc):
            rows = slice(c * GLA_CHUNK, (c + 1) * GLA_CHUNK)
            lhs = jnp.concatenate([qd[c, :, kc], scores[c, hd]], axis=1)
            rhs = jnp.concatenate([st.astype(BF16), vb[rows, vc]], axis=0)
            outs[c, hd] = _dot(lhs, rhs)
            j = c * N_HEADS + hd
            st = st * decay_cols[:, j:j + 1] + upd[c, hd]
        st_scr[hd] = st

    sig_a = jax.nn.sigmoid(proj_b(3))
    sig_b = jax.nn.sigmoid(proj_b(4))

    for hd in range(N_HEADS):
        vc = slice(hd * DV, (hd + 1) * DV)
        for c in range(nc):
            rows = slice(c * GLA_CHUNK, (c + 1) * GLA_CHUNK)
            on = _rms(outs[c, hd], ggla_ref[:, vc])
            ygla_scr[rows, vc] = (on * _silu(r[rows, vc])).astype(BF16)

    yield
    mix = _merge_and_project(ygla_scr[...], y_conv, sig_a, sig_b, wbo_ref, wmo_ref)
    y_ref[0, prow, :] = x + gate * _rms(mix, gpost_ref[...])


def _mixer(x, ada, g_pre, g_post, wts, *, jobs=()):
    n, l, _ = x.shape
    tm = MIX_TM
    n_l = l // tm
    kern = functools.partial(_mixer_kernel, job_groups=tuple(j.static for j in jobs))
    cast_in, cast_out, cast_shapes = _cast_plumbing(jobs, lambda n, l: n * n_l + l)
    wa, wz, wb, walpha, balpha, ggla, wconv, wbo, wmo = wts
    return pl.pallas_call(
        kern,
        grid=(n, n_l),
        in_specs=[pl.BlockSpec((1, tm, D_MODEL), lambda n, l: (n, l, 0))]
        + _mod_specs(1, ada)
        + [_resident(a.shape) for a in (g_pre, g_post, wa, wz, wb, walpha, balpha, ggla, wconv,
                                        wbo, wmo)]
        + cast_in,
        out_specs=[
            pl.BlockSpec((1, tm, D_MODEL), lambda n, l: (n, l, 0)),
            pl.BlockSpec((1, 1, N_HEADS, DK, DV), lambda n, l: (0, n, 0, 0, 0)),
            pl.BlockSpec((1, 1, CONV_WIDTH - 1, D_CONV), lambda n, l: (0, n, 0, 0)),
        ] + cast_out,
        out_shape=[
            jax.ShapeDtypeStruct(x.shape, F32),
            jax.ShapeDtypeStruct((1, n, N_HEADS, DK, DV), F32),
            jax.ShapeDtypeStruct((1, n, CONV_WIDTH - 1, D_CONV), F32),
        ] + cast_shapes,
        scratch_shapes=[
            pltpu.VMEM((N_HEADS, DK, DV), F32),
            pltpu.VMEM((MIX_PART + SUBLANES, D_CONV), F32),
            pltpu.VMEM((tm, V_WIDTH), BF16),
        ],
        compiler_params=_params(("arbitrary", "arbitrary")),
        name="mixer",
    )(x, ada, ada, ada, g_pre, g_post, wa, wz, wb, walpha, balpha, ggla, wconv, wbo, wmo,
      *[j.src for j in jobs])


def _smix_pre_kernel(x_ref, sh_ref, sc_ref, gpre_ref, wa_ref, wz_ref, wb_ref, walpha_ref,
                     balpha_ref, wconv_ref, sconv_ref,
                     qt_ref, kt_ref, at_ref, v_ref, sr_ref, yconv_ref, sga_ref, sgb_ref,
                     sconv_new_ref):
    x = x_ref[...]
    hb = (_rms(x, gpre_ref[...]) * (1.0 + sc_ref[...]) + sh_ref[...]).astype(BF16)
    pa = _dot(hb, wa_ref[:, :_A_WIDTH])
    q = pa[:, 0:QK_WIDTH] * (DK ** -0.5)
    k = pa[:, QK_WIDTH:2 * QK_WIDTH]
    v_ref[...] = pa[:, 2 * QK_WIDTH:2 * QK_WIDTH + V_WIDTH]
    sr_ref[...] = _silu(pa[:, 2 * QK_WIDTH + V_WIDTH:])
    a = jnp.exp(_log_decay(hb, wz_ref, walpha_ref, balpha_ref))
    for hd in range(N_HEADS):
        kc = slice(hd * DK, (hd + 1) * DK)
        qt_ref[hd] = q[:, kc].T
        kt_ref[hd] = k[:, kc].T
        at_ref[hd] = a[:, kc].T

    pb = _dot(hb, wb_ref[:, :_B_WIDTH])
    u = pb[:, D_CONV:2 * D_CONV] * pb[:, 2 * D_CONV:3 * D_CONV]
    conv = (wconv_ref[0:1, :] * sconv_ref[:, 0, :] + wconv_ref[1:2, :] * sconv_ref[:, 1, :]
            + wconv_ref[2:3, :] * u)
    yconv_ref[...] = pb[:, 0:D_CONV] * conv
    sga_ref[...] = jax.nn.sigmoid(pb[:, 3 * D_CONV:3 * D_CONV + D_MODEL])
    sgb_ref[...] = jax.nn.sigmoid(pb[:, 3 * D_CONV + D_MODEL:])
    sconv_new_ref[:, 0, :] = sconv_ref[:, 1, :]
    sconv_new_ref[:, 1, :] = u


def _smix_state_kernel(s_ref, qt_ref, kt_ref, at_ref, v_ref, snew_ref, o_ref):
    base = pl.program_id(0) * SAMPLE_BLOCK
    lane = lax.broadcasted_iota(jnp.int32, (1, LANES), 1)
    for i in range(SAMPLE_BLOCK):
        pick = jnp.where(lane == base + i, 1.0, 0.0)
        for hd in range(N_HEADS):
            vc = slice(hd * DV, (hd + 1) * DV)
            a_col = jnp.sum(at_ref[hd] * pick, axis=1, keepdims=True)
            k_col = jnp.sum(kt_ref[hd] * pick, axis=1, keepdims=True)
            q_col = jnp.sum(qt_ref[hd] * pick, axis=1, keepdims=True)
            s_new = a_col * s_ref[0, i, hd] + k_col * v_ref[i:i + 1, vc]
            snew_ref[0, i, hd] = s_new
            o_ref[i:i + 1, vc] = jnp.sum(q_col * s_new, axis=0, keepdims=True)


def _smix_post_kernel(x_ref, gt_ref, gpost_ref, o_ref, sr_ref, yconv_ref, sga_ref, sgb_ref,
                      ggla_ref, wbo_ref, wmo_ref, y_ref, ygla_scr):
    for hd in range(N_HEADS):
        vc = slice(hd * DV, (hd + 1) * DV)
        on = _rms(o_ref[:, vc], ggla_ref[:, vc])
        ygla_scr[:, vc] = (on * sr_ref[:, vc]).astype(BF16)
    mix = _merge_and_project(ygla_scr[...], yconv_ref[...].astype(BF16), sga_ref[...],
                             sgb_ref[...], wbo_ref, wmo_ref)
    y_ref[...] = x_ref[...] + gt_ref[...] * _rms(mix, gpost_ref[...])


def _sample_mixer(x, ada, state_gla, state_conv, g_pre, g_post, wts):
    s = x.shape[0]
    wa, wz, wb, walpha, balpha, ggla, wconv, wbo, wmo = wts
    whole = lambda a: pl.BlockSpec(a.shape, lambda *_: (0,) * a.ndim)
    mod = lambda j: pl.BlockSpec((s, D_MODEL), lambda *_: (0, 3 + j))
    tok = jax.ShapeDtypeStruct((s, D_MODEL), F32)
    tr = jax.ShapeDtypeStruct((N_HEADS, DK, s), F32)

    pre_in = (x, ada, ada, g_pre, wa, wz, wb, walpha, balpha, wconv, state_conv)
    pre_specs = [whole(x), mod(0), mod(1)] + [whole(a) for a in pre_in[3:]]
    pre_out = [tr, tr, tr, tok, tok, tok, tok, tok, jax.ShapeDtypeStruct(state_conv.shape, F32)]
    qt, kt, at, v, sr, yconv, sga, sgb, sconv_new = pl.pallas_call(
        _smix_pre_kernel,
        grid=(1,),
        in_specs=pre_specs,
        out_specs=[whole(o) for o in pre_out],
        out_shape=pre_out,
        compiler_params=_params(("arbitrary",)),
        name="smix_pre",
    )(*pre_in)

    state_block = pl.BlockSpec((1, SAMPLE_BLOCK, N_HEADS, DK, DV), lambda j: (0, j, 0, 0, 0))
    snew, o = pl.pallas_call(
        _smix_state_kernel,
        grid=(s // SAMPLE_BLOCK,),
        in_specs=[state_block, whole(qt), whole(kt), whole(at),
                  pl.BlockSpec((SAMPLE_BLOCK, V_WIDTH), lambda j: (j, 0))],
        out_specs=[state_block, pl.BlockSpec((SAMPLE_BLOCK, V_WIDTH), lambda j: (j, 0))],
        out_shape=[jax.ShapeDtypeStruct(state_gla.shape, F32), tok],
        compiler_params=_params(("arbitrary",)),
        name="smix_state",
    )(state_gla, qt, kt, at, v)

    post_in = (x, ada, g_post, o, sr, yconv, sga, sgb, ggla, wbo, wmo)
    post_specs = [whole(x), mod(2)] + [whole(a) for a in post_in[2:]]
    y = pl.pallas_call(
        _smix_post_kernel,
        grid=(1,),
        in_specs=post_specs,
        out_specs=whole(tok),
        out_shape=tok,
        scratch_shapes=[pltpu.VMEM((s, V_WIDTH), BF16)],
        compiler_params=_params(("arbitrary",)),
        name="smix_post",
    )(*post_in)
    return y, snew, sconv_new


def kernel(x_prompt, x_sample, state_gla, state_conv, c_prompt, c_sample, w_ada, b_ada, g_pre,
           g_post, w_ffn1_in, w_ffn1_out, w_ffn2_in, w_ffn2_out, w_mix_in, w_alpha, b_alpha,
           g_gla_norm, w_conv, w_branch_out, w_mix_out):
    depth = w_ada.shape[0]
    n_s = x_sample.shape[0]
    assert depth == 1 and x_sample.shape[1] == 1 and n_s == LANES

    yp = x_prompt
    ys = x_sample.reshape(1, n_s, D_MODEL)
    outs = []
    for i in range(depth):
        ada_s, ada_p = _ada(c_sample, c_prompt, w_ada[i], b_ada[i])
        gpre = [g_pre[i, j].reshape(1, D_MODEL) for j in range(N_SUBLAYERS)]
        gpost = [g_post[i, j].reshape(1, D_MODEL) for j in range(N_SUBLAYERS)]

        ffn1_w = (w_ffn1_in[i].astype(BF16),
                  jnp.pad(w_ffn1_out[i].astype(BF16), ((0, 0), (0, LANES))))
        n_ffn_steps = yp.shape[0] * (yp.shape[1] // FFN_TM)
        n_mix_steps = yp.shape[0] * (yp.shape[1] // MIX_TM)
        w_mix_t = jnp.swapaxes(w_mix_in[i], 0, 1)
        mixer_jobs = (
            _CastJob(w_mix_t, MIX_QKVR_CAST_ROWS, _A_WIDTH // MIX_QKVR_CAST_ROWS, transpose=True,
                     pad=1),
            _CastJob(w_mix_t, LANES, 1, transpose=True, row0=_A_WIDTH, valid=GATE_RANK),
            _CastJob(w_mix_t, MIX_CONV_CAST_ROWS, _B_WIDTH // MIX_CONV_CAST_ROWS, transpose=True,
                     row0=_A_WIDTH + GATE_RANK, pad=1),
            _whole_cast(w_branch_out[i].reshape(2 * D_MODEL, D_MODEL), n_ffn_steps, pad=LANES),
            _whole_cast(w_mix_out[i], n_ffn_steps, pad=LANES))
        assert all(j.n_steps <= n_ffn_steps for j in mixer_jobs)
        assert all((j.n_steps * j.block_rows if j.transpose else j.src.shape[1] + j.pad)
                   % (WEIGHT_PITCH_TILES * LANES) for j in mixer_jobs)
        yp, ys, wa, wz, wb, wbo, wmo = _ffn(yp, ys, ada_p, ada_s, 0, gpre[0], gpost[0], *ffn1_w,
                                            jobs=mixer_jobs)
        walpha = jnp.pad(w_alpha[i], ((0, LANES - GATE_RANK), (0, 0))).astype(BF16)
        mix_w = (wa, wz, wb, walpha, b_alpha[i].reshape(1, QK_WIDTH),
                 g_gla_norm[i].reshape(1, V_WIDTH),
                 jnp.pad(w_conv[i], ((0, SUBLANES - CONV_WIDTH), (0, 0))),
                 wbo.reshape(2, D_MODEL, D_MODEL + LANES), wmo)
        ys, gla_s, sconv_new = _sample_mixer(ys[0], ada_s, state_gla[i:i + 1], state_conv[i],
                                             gpre[1], gpost[1], mix_w)
        yp, gla_p, conv_p, *ffn2_w = _mixer(
            yp, ada_p, gpre[1], gpost[1], mix_w,
            jobs=(_whole_cast(w_ffn2_in[i], n_mix_steps),
                  _whole_cast(w_ffn2_out[i], n_mix_steps, pad=LANES)))
        yp, ys = _ffn(yp, ys.reshape(1, n_s, D_MODEL), ada_p, ada_s, 2, gpre[2], gpost[2],
                      *ffn2_w)
        outs.append((gla_p, conv_p, gla_s, sconv_new[None]))

    gla_p, conv_p, gla_s, conv_s = outs[0]
    return (yp, ys.reshape(n_s, 1, D_MODEL), gla_p, conv_p, gla_s, conv_s)
```

```python
import functools
from typing import NamedTuple

import jax
import jax.numpy as jnp
from jax import lax
from jax.experimental import pallas as pl
from jax.experimental.pallas import tpu as pltpu

F32 = jnp.float32
BF16 = jnp.bfloat16

D_MODEL = 1024
D_FF = 2816
N_HEADS = 4
DK = 128
DV = 256
QK_WIDTH = N_HEADS * DK
V_WIDTH = N_HEADS * DV
GATE_RANK = 16
GATE_TAU = 16.0
D_CONV = D_MODEL
CONV_WIDTH = 3
RMS_EPS = 1e-6
MACARON_WEIGHT = 0.5

LANES = 128
SUBLANES = 8
GLA_CHUNK = 64
VMEM_LIMIT_BYTES = 56 * 1024 * 1024

FFN_TM = 1024
FFN_PART = 256
MIX_TM = 512
MIX_PART = 256
ADA_TN = 3072
SAMPLE_BLOCK = 16

_A_WIDTH = 2 * QK_WIDTH + 2 * V_WIDTH
_B_WIDTH = 3 * D_CONV + 2 * D_MODEL


def _rms(x, g):
    return x * lax.rsqrt(jnp.mean(x * x, axis=-1, keepdims=True) + RMS_EPS) * g


def _silu(x):
    return x * jax.nn.sigmoid(x)


def _dot(a, b):
    return jnp.dot(a, b, preferred_element_type=F32)


def _dot_nt(a, b):
    return lax.dot_general(a, b, (((1,), (1,)), ((), ())), preferred_element_type=F32)


def _dot_tn(a, b):
    return lax.dot_general(a, b, (((0,), (0,)), ((), ())), preferred_element_type=F32)


def _resident(shape):
    zeros = (0,) * len(shape)
    return pl.BlockSpec(shape, lambda *_: zeros, pipeline_mode=pl.Buffered(1))


def _params(semantics):
    return pltpu.CompilerParams(dimension_semantics=semantics, vmem_limit_bytes=VMEM_LIMIT_BYTES)


def _ada_kernel(cs_ref, cp_ref, w_ref, b_ref, os_ref, op_ref):
    w = w_ref[...].astype(BF16)
    for c_ref, o_ref in ((cs_ref, os_ref), (cp_ref, op_ref)):
        o_ref[...] = _dot(_silu(c_ref[...]).astype(BF16), w) + b_ref[...]


def _ada(c_sample, c_prompt, w_ada, b_ada):
    width = w_ada.shape[1]
    rows = lambda c: pl.BlockSpec((c.shape[0], D_MODEL), lambda j: (0, 0))
    cols = lambda c: pl.BlockSpec((c.shape[0], ADA_TN), lambda j: (0, j))
    return pl.pallas_call(
        _ada_kernel,
        grid=(width // ADA_TN,),
        in_specs=[rows(c_sample), rows(c_prompt),
                  pl.BlockSpec((D_MODEL, ADA_TN), lambda j: (0, j)),
                  pl.BlockSpec((1, ADA_TN), lambda j: (0, j))],
        out_specs=[cols(c_sample), cols(c_prompt)],
        out_shape=[jax.ShapeDtypeStruct((c.shape[0], width), F32) for c in (c_sample, c_prompt)],
        compiler_params=_params(("arbitrary",)),
        name="ada",
    )(c_sample, c_prompt, w_ada, b_ada.reshape(1, width))


def _mod_specs(sub, ada):
    return [pl.BlockSpec((ada.shape[0], D_MODEL), lambda *_, col=sub * 3 + j: (0, col))
            for j in range(3)]


def _load_mod(ref, row):
    if row is None:
        return ref[...]
    return ref[pl.ds(row, 1), :]


BF16_SUBLANES = 16
WEIGHT_PITCH_TILES = 8
MIX_QKVR_CAST_ROWS = 2 * LANES
MIX_CONV_CAST_ROWS = 4 * LANES


class _CastJob(NamedTuple):
    src: jax.Array
    block_rows: int
    n_blocks: int
    transpose: bool = False
    row0: int = 0
    valid: int = 0
    pad: int = 0

    @property
    def n_steps(self):
        return self.n_blocks + (self.pad if self.transpose else 0)

    def specs(self, step_of):
        cols = self.src.shape[1]
        r, nb = self.block_rows, self.n_blocks
        block = lambda *g: jnp.minimum(step_of(*g), nb - 1)
        if not self.transpose:
            return (pl.BlockSpec((r, cols), lambda *g: (block(*g), 0)),
                    pl.BlockSpec((r, cols + self.pad), lambda *g: (block(*g), 0)),
                    jax.ShapeDtypeStruct((nb * r, cols + self.pad), BF16))
        assert self.row0 % SUBLANES == 0 and r % SUBLANES == 0
        out_block = lambda *g: jnp.minimum(step_of(*g), self.n_steps - 1)
        return (pl.BlockSpec((pl.Element(r), pl.Element(cols)),
                             lambda *g: (pl.multiple_of(self.row0 + r * block(*g), SUBLANES), 0)),
                pl.BlockSpec((cols, r), lambda *g: (0, out_block(*g))),
                jax.ShapeDtypeStruct((cols, self.n_steps * r), BF16))

    @property
    def static(self):
        return (self.transpose, self.valid or self.block_rows, self.n_blocks, self.pad)


def _whole_cast(w, n_steps, pad=0):
    rows = w.shape[0]
    r = next(r for r in range(BF16_SUBLANES, rows + 1, BF16_SUBLANES)
             if rows % r == 0 and rows // r <= n_steps)
    return _CastJob(w, r, rows // r, pad=pad)


def _cast_plumbing(jobs, step_of):
    specs = [job.specs(step_of) for job in jobs]
    return ([i for i, _, _ in specs], [o for _, o, _ in specs], [s for _, _, s in specs])


def _run_casts(statics, step, src_refs, dst_refs):
    for (transpose, valid, n_blocks, pad), src, dst in zip(statics, src_refs, dst_refs,
                                                           strict=True):
        blk = src[...]
        if not transpose:
            cols = blk.shape[1]
            dst[:, :cols] = blk.astype(BF16)
            if pad:
                dst[:, cols:] = jnp.zeros((blk.shape[0], pad), BF16)
            continue
        blk = blk.T
        if valid < blk.shape[1]:
            lane = lax.broadcasted_iota(jnp.int32, blk.shape, 1)
            blk = jnp.where(lane < valid, blk, 0.0)
        if pad:
            blk = jnp.where(step < n_blocks, blk, 0.0)
        dst[...] = blk.astype(BF16)


def _ffn_rows(x_ref, o_ref, mods, per_token, gpre_ref, gpost_ref, win_ref, wout_ref):
    shift, scale, gate = mods
    tm = x_ref.shape[1]
    parts = max(tm // FFN_PART, 1)
    rows = [slice(i * (tm // parts), (i + 1) * (tm // parts)) for i in range(parts)]
    mod = lambda m, rs: m[rs] if per_token else m

    def pre(rs):
        return (_rms(x_ref[0, rs, :], gpre_ref[...]) * (1.0 + mod(scale, rs))
                + mod(shift, rs)).astype(BF16)

    def post(rs, out):
        o_ref[0, rs, :] = (x_ref[0, rs, :]
                           + MACARON_WEIGHT * mod(gate, rs) * _rms(out, gpost_ref[...]))

    h = pre(rows[0])
    for i in range(parts):
        g = _dot(h, win_ref[:, :D_FF])
        h_next = pre(rows[i + 1]) if i + 1 < parts else None
        u = _dot(h, win_ref[:, D_FF:])
        a = (_silu(g) * u).astype(BF16)
        if i > 0:
            post(rows[i - 1], out)
        out = _dot(a, wout_ref[:, :D_MODEL])
        h = h_next
    post(rows[parts - 1], out)


def _ffn_kernel(x_ref, sh_ref, sc_ref, gt_ref, xs_ref, shs_ref, scs_ref, gts_ref, gpre_ref,
                gpost_ref, win_ref, wout_ref, *rest, sub, n_l, job_groups):
    n_jobs = len(job_groups)
    o_ref, os_ref = rest[n_jobs:n_jobs + 2]
    step = pl.program_id(0)
    _run_casts(job_groups, step, rest[:n_jobs], rest[n_jobs + 2:])
    last = pl.num_programs(0) - 1
    weights = (gpre_ref.at[sub:sub + 1], gpost_ref.at[sub:sub + 1], win_ref, wout_ref)

    @pl.when(step < last)
    def _():
        mods = [_load_mod(r, step // n_l) for r in (sh_ref, sc_ref, gt_ref)]
        _ffn_rows(x_ref, o_ref, mods, False, *weights)

    @pl.when(step == last)
    def _():
        mods = [_load_mod(r, None) for r in (shs_ref, scs_ref, gts_ref)]
        _ffn_rows(xs_ref, os_ref, mods, True, *weights)


def _ffn(x, xs, ada_p, ada_s, sub, g_pre, g_post, w_in, w_out, *, jobs=()):
    n, l, _ = x.shape
    n_l = l // FFN_TM
    n_tiles = n * n_l
    tile = lambda i: (jnp.minimum(i, n_tiles - 1) // n_l, jnp.minimum(i, n_tiles - 1) % n_l, 0)
    whole = lambda i: (0, 0, 0)
    kern = functools.partial(_ffn_kernel, sub=sub, n_l=n_l,
                             job_groups=tuple(j.static for j in jobs))
    cast_in, cast_out, cast_shapes = _cast_plumbing(jobs, lambda i: i)
    return pl.pallas_call(
        kern,
        grid=(n_tiles + 1,),
        in_specs=[pl.BlockSpec((1, FFN_TM, D_MODEL), tile)]
        + _mod_specs(sub, ada_p)
        + [pl.BlockSpec(xs.shape, whole)]
        + _mod_specs(sub, ada_s)
        + [_resident(g_pre.shape), _resident(g_post.shape),
           _resident(w_in.shape), _resident(w_out.shape)]
        + cast_in,
        out_specs=[pl.BlockSpec((1, FFN_TM, D_MODEL), tile), pl.BlockSpec(xs.shape, whole)]
        + cast_out,
        out_shape=[jax.ShapeDtypeStruct(x.shape, F32), jax.ShapeDtypeStruct(xs.shape, F32)]
        + cast_shapes,
        compiler_params=_params(("arbitrary",)),
        name="ffn",
    )(x, ada_p, ada_p, ada_p, xs, ada_s, ada_s, ada_s, g_pre, g_post, w_in, w_out,
      *[j.src for j in jobs])


def _log_decay(hb, wz_ref, walpha_ref, balpha_ref):
    z = _dot(hb, wz_ref[...]).astype(BF16)
    xg = _dot(z, walpha_ref[...]) + balpha_ref[...]
    return jax.nn.log_sigmoid(xg) * (1.0 / GATE_TAU)


def _merge_and_project(y_gla, y_conv, sig_a, sig_b, wbo_ref, wmo_ref):
    pg = _dot(y_gla, wbo_ref[0, :, :D_MODEL])
    pc = _dot(y_conv, wbo_ref[1, :, :D_MODEL])
    merged = (sig_a * pg + sig_b * pc).astype(BF16)
    return _dot(merged, wmo_ref[:, :D_MODEL])


_PAIR_LEVELS = tuple(GLA_CHUNK >> i for i in range(1, GLA_CHUNK.bit_length()))


def _pair_operands(qs, k, b, log_a, m, tm):
    if m >= SUBLANES:
        halves = lambda a: a.reshape(tm // (2 * m), 2, m, QK_WIDTH)
        b4, k4, q4 = halves(b), halves(k), halves(qs)
        mid = b4[:, 0:1, m - 1:m, :]
        w = jnp.exp(jnp.concatenate([mid - b4[:, 0:1], b4[:, 1:2] - mid], axis=1))
        src = jnp.concatenate([k4[:, 0:1], q4[:, 1:2]], axis=1)
        return (src * w).astype(BF16).reshape(tm, QK_WIDTH)
    tiles = lambda a: a.reshape(tm // SUBLANES, SUBLANES, QK_WIDTH)
    sub = lax.broadcasted_iota(jnp.int32, (1, SUBLANES, QK_WIDTH), 1)
    second = (sub & m) != 0
    if m == 1:
        w = jnp.exp(jnp.where(second, tiles(log_a), 0.0))
    else:
        b4 = tiles(b)
        mid = b4[:, m - 1:m, :]
        for start in range(2 * m, SUBLANES, 2 * m):
            mid = jnp.where(sub >= start, b4[:, start + m - 1:start + m, :], mid)
        w = jnp.exp((b4 - mid) * jnp.where(second, 1.0, -1.0))
    return (jnp.where(second, tiles(qs), tiles(k)) * w).astype(BF16).reshape(tm, QK_WIDTH)


def _pair_mask(m):
    t = lax.broadcasted_iota(jnp.int32, (GLA_CHUNK, GLA_CHUNK), 0)
    s = lax.broadcasted_iota(jnp.int32, (GLA_CHUNK, GLA_CHUNK), 1)
    return ((t & m) != 0) & ((s & m) == 0) & ((t // (2 * m)) == (s // (2 * m)))


def _mixer_kernel(x_ref, sh_ref, sc_ref, gt_ref, gpre_ref, gpost_ref, wa_ref, wz_ref, wb_ref,
                  walpha_ref, balpha_ref, ggla_ref, wconv_ref, wbo_ref, wmo_ref, *rest,
                  job_groups):
    n_jobs = len(job_groups)
    y_ref, sgla_ref, sconv_ref = rest[n_jobs:n_jobs + 3]
    st_scr, u_scr, ygla_scr = rest[-3:]
    l = pl.program_id(1)
    n_l = pl.num_programs(1)
    _run_casts(job_groups, pl.program_id(0) * n_l + l, rest[:n_jobs], rest[n_jobs + 3:-3])

    @pl.when(l == 0)
    def _():
        st_scr[...] = jnp.zeros_like(st_scr)
        u_scr[0:SUBLANES, :] = jnp.zeros((SUBLANES, D_CONV), F32)

    mods = [_load_mod(r, pl.program_id(0)) for r in (sh_ref, sc_ref, gt_ref)]
    weights = (gpre_ref.at[1:2], gpost_ref.at[1:2], wa_ref, wz_ref, wb_ref, walpha_ref,
               balpha_ref, ggla_ref, wconv_ref, wbo_ref, wmo_ref)
    n_parts = x_ref.shape[1] // MIX_PART
    parts = [_mixer_part(x_ref, y_ref, mods, weights, (st_scr, u_scr, ygla_scr),
                         slice(i * MIX_PART, (i + 1) * MIX_PART)) for i in range(n_parts)]
    next(parts[0])
    for i, p in enumerate(parts):
        next(p)
        if i + 1 < n_parts:
            next(parts[i + 1])
        if i > 0:
            next(parts[i - 1], None)
        next(p)
    next(parts[-1], None)

    @pl.when(l == n_l - 1)
    def _():
        sgla_ref[0, 0] = st_scr[...]
        sconv_ref[0, 0] = u_scr[SUBLANES - 2:SUBLANES, :]


def _mixer_part(x_ref, y_ref, mods, weights, scratch, prow):
    shift, scale, gate = mods
    (gpre_ref, gpost_ref, wa_ref, wz_ref, wb_ref, walpha_ref, balpha_ref, ggla_ref, wconv_ref,
     wbo_ref, wmo_ref) = weights
    st_scr, u_scr, ygla_scr = scratch
    ygla_scr = ygla_scr.at[prow]
    tm = MIX_PART
    x = x_ref[0, prow, :]
    hb = (_rms(x, gpre_ref[...]) * (1.0 + scale) + shift).astype(BF16)
    yield

    def proj_b(j):
        return _dot(hb, wb_ref[:, j * D_MODEL:(j + 1) * D_MODEL])

    nc = tm // GLA_CHUNK
    log_a = _log_decay(hb, wz_ref, walpha_ref, balpha_ref)
    pa = _dot(hb, wa_ref[:, :_A_WIDTH])
    qs = pa[:, 0:QK_WIDTH] * (DK ** -0.5)
    k = pa[:, QK_WIDTH:2 * QK_WIDTH]
    vb = pa[:, 2 * QK_WIDTH:2 * QK_WIDTH + V_WIDTH].astype(BF16)
    r = pa[:, 2 * QK_WIDTH + V_WIDTH:]

    row = lax.broadcasted_iota(jnp.int32, (tm, tm), 0)
    col = lax.broadcasted_iota(jnp.int32, (tm, tm), 1)
    tri = jnp.where((row >= col) & (row // GLA_CHUNK == col // GLA_CHUNK), 1.0, 0.0).astype(BF16)
    la_hi = log_a.astype(BF16)
    rem = log_a - la_hi.astype(F32)
    la_mid = rem.astype(BF16)
    la_lo = (rem - la_mid.astype(F32)).astype(BF16)
    b3 = (_dot(tri, la_hi) + _dot(tri, la_mid) + _dot(tri, la_lo)).reshape(
        nc, GLA_CHUNK, QK_WIDTH)

    u = proj_b(1) * proj_b(2)
    u_scr[SUBLANES:SUBLANES + tm, :] = u
    yield

    b_last = b3[:, GLA_CHUNK - 1:GLA_CHUNK, :]
    b = b3.reshape(tm, QK_WIDTH)
    qd = (qs * jnp.exp(b)).astype(BF16).reshape(nc, GLA_CHUNK, QK_WIDTH)
    kl = (k.reshape(nc, GLA_CHUNK, QK_WIDTH) * jnp.exp(b_last - b3)).astype(BF16)
    b_last_rows = jnp.concatenate(
        [b_last[c, :, hd * DK:(hd + 1) * DK] for c in range(nc) for hd in range(N_HEADS)]
        + [jnp.zeros((LANES - nc * N_HEADS, DK), F32)], axis=0)
    decay_cols = jnp.exp(b_last_rows.T)

    pair_ops = [_pair_operands(qs, k, b, log_a, m, tm) for m in _PAIR_LEVELS]
    pair_masks = [_pair_mask(m) for m in _PAIR_LEVELS]
    qs_b, k_b = qs.astype(BF16), k.astype(BF16)
    crow = lax.broadcasted_iota(jnp.int32, (GLA_CHUNK, GLA_CHUNK), 0)
    ccol = lax.broadcasted_iota(jnp.int32, (GLA_CHUNK, GLA_CHUNK), 1)
    scores, upd = {}, {}
    for c in range(nc):
        rows = slice(c * GLA_CHUNK, (c + 1) * GLA_CHUNK)
        for hd in range(N_HEADS):
            kc = slice(hd * DK, (hd + 1) * DK)
            s = jnp.where(crow == ccol, _dot_nt(qs_b[rows, kc], k_b[rows, kc]), 0.0)
            for p, mask in zip(pair_ops, pair_masks):
                s = jnp.where(mask, _dot_nt(p[rows, kc], p[rows, kc]), s)
            scores[c, hd] = s.astype(BF16)
            upd[c, hd] = _dot_tn(kl[c, :, kc], vb[rows, hd * DV:(hd + 1) * DV])

    conv = (wconv_ref[0:1, :] * u_scr[SUBLANES - 2:SUBLANES - 2 + tm, :]
            + wconv_ref[1:2, :] * u_scr[SUBLANES - 1:SUBLANES - 1 + tm, :]
            + wconv_ref[2:3, :] * u)
    u_scr[0:SUBLANES, :] = u_scr[tm:tm + SUBLANES, :]
    y_conv = (proj_b(0) * conv).astype(BF16)

    outs = {}
    for hd in range(N_HEADS):
        kc = slice(hd * DK, (hd + 1) * DK)
        vc = slice(hd * DV, (hd + 1) * DV)
        st = st_scr[hd]
        for c in range(nc):
            rows = slice(c * GLA_CHUNK, (c + 1) * GLA_CHUNK)
            lhs = jnp.concatenate([qd[c, :, kc], scores[c, hd]], axis=1)
            rhs = jnp.concatenate([st.astype(BF16), vb[rows, vc]], axis=0)
            outs[c, hd] = _dot(lhs, rhs)
            j = c * N_HEADS + hd
            st = st * decay_cols[:, j:j + 1] + upd[c, hd]
        st_scr[hd] = st

    sig_a = jax.nn.sigmoid(proj_b(3))
    sig_b = jax.nn.sigmoid(proj_b(4))

    for hd in range(N_HEADS):
        vc = slice(hd * DV, (hd + 1) * DV)
        for c in range(nc):
            rows = slice(c * GLA_CHUNK, (c + 1) * GLA_CHUNK)
            on = _rms(outs[c, hd], ggla_ref[:, vc])
            ygla_scr[rows, vc] = (on * _silu(r[rows, vc])).astype(BF16)

    yield
    mix = _merge_and_project(ygla_scr[...], y_conv, sig_a, sig_b, wbo_ref, wmo_ref)
    y_ref[0, prow, :] = x + gate * _rms(mix, gpost_ref[...])


def _mixer(x, ada, g_pre, g_post, wts, *, jobs=()):
    n, l, _ = x.shape
    tm = MIX_TM
    n_l = l // tm
    kern = functools.partial(_mixer_kernel, job_groups=tuple(j.static for j in jobs))
    cast_in, cast_out, cast_shapes = _cast_plumbing(jobs, lambda n, l: n * n_l + l)
    wa, wz, wb, walpha, balpha, ggla, wconv, wbo, wmo = wts
    return pl.pallas_call(
        kern,
        grid=(n, n_l),
        in_specs=[pl.BlockSpec((1, tm, D_MODEL), lambda n, l: (n, l, 0))]
        + _mod_specs(1, ada)
        + [_resident(a.shape) for a in (g_pre, g_post, wa, wz, wb, walpha, balpha, ggla, wconv,
                                        wbo, wmo)]
        + cast_in,
        out_specs=[
            pl.BlockSpec((1, tm, D_MODEL), lambda n, l: (n, l, 0)),
            pl.BlockSpec((1, 1, N_HEADS, DK, DV), lambda n, l: (0, n, 0, 0, 0)),
            pl.BlockSpec((1, 1, CONV_WIDTH - 1, D_CONV), lambda n, l: (0, n, 0, 0)),
        ] + cast_out,
        out_shape=[
            jax.ShapeDtypeStruct(x.shape, F32),
            jax.ShapeDtypeStruct((1, n, N_HEADS, DK, DV), F32),
            jax.ShapeDtypeStruct((1, n, CONV_WIDTH - 1, D_CONV), F32),
        ] + cast_shapes,
        scratch_shapes=[
            pltpu.VMEM((N_HEADS, DK, DV), F32),
            pltpu.VMEM((MIX_PART + SUBLANES, D_CONV), F32),
            pltpu.VMEM((tm, V_WIDTH), BF16),
        ],
        compiler_params=_params(("arbitrary", "arbitrary")),
        name="mixer",
    )(x, ada, ada, ada, g_pre, g_post, wa, wz, wb, walpha, balpha, ggla, wconv, wbo, wmo,
      *[j.src for j in jobs])


def _smix_pre_kernel(x_ref, sh_ref, sc_ref, gpre_ref, wa_ref, wz_ref, wb_ref, walpha_ref,
                     balpha_ref, wconv_ref, sconv_ref,
                     qt_ref, kt_ref, at_ref, v_ref, sr_ref, yconv_ref, sga_ref, sgb_ref,
                     sconv_new_ref):
    x = x_ref[...]
    hb = (_rms(x, gpre_ref[1:2, :]) * (1.0 + sc_ref[...]) + sh_ref[...]).astype(BF16)
    pa = _dot(hb, wa_ref[:, :_A_WIDTH])
    q = pa[:, 0:QK_WIDTH] * (DK ** -0.5)
    k = pa[:, QK_WIDTH:2 * QK_WIDTH]
    v_ref[...] = pa[:, 2 * QK_WIDTH:2 * QK_WIDTH + V_WIDTH]
    sr_ref[...] = _silu(pa[:, 2 * QK_WIDTH + V_WIDTH:])
    a = jnp.exp(_log_decay(hb, wz_ref, walpha_ref, balpha_ref))
    for hd in range(N_HEADS):
        kc = slice(hd * DK, (hd + 1) * DK)
        qt_ref[hd] = q[:, kc].T
        kt_ref[hd] = k[:, kc].T
        at_ref[hd] = a[:, kc].T

    pb = _dot(hb, wb_ref[:, :_B_WIDTH])
    u = pb[:, D_CONV:2 * D_CONV] * pb[:, 2 * D_CONV:3 * D_CONV]
    conv = (wconv_ref[0:1, :] * sconv_ref[:, 0, :] + wconv_ref[1:2, :] * sconv_ref[:, 1, :]
            + wconv_ref[2:3, :] * u)
    yconv_ref[...] = pb[:, 0:D_CONV] * conv
    sga_ref[...] = jax.nn.sigmoid(pb[:, 3 * D_CONV:3 * D_CONV + D_MODEL])
    sgb_ref[...] = jax.nn.sigmoid(pb[:, 3 * D_CONV + D_MODEL:])
    sconv_new_ref[:, 0, :] = sconv_ref[:, 1, :]
    sconv_new_ref[:, 1, :] = u


def _smix_state_kernel(s_ref, qt_ref, kt_ref, at_ref, v_ref, snew_ref, o_ref):
    base = pl.program_id(0) * SAMPLE_BLOCK
    lane = lax.broadcasted_iota(jnp.int32, (1, LANES), 1)
    for i in range(SAMPLE_BLOCK):
        pick = jnp.where(lane == base + i, 1.0, 0.0)
        for hd in range(N_HEADS):
            vc = slice(hd * DV, (hd + 1) * DV)
            a_col = jnp.sum(at_ref[hd] * pick, axis=1, keepdims=True)
            k_col = jnp.sum(kt_ref[hd] * pick, axis=1, keepdims=True)
            q_col = jnp.sum(qt_ref[hd] * pick, axis=1, keepdims=True)
            s_new = a_col * s_ref[0, i, hd] + k_col * v_ref[i:i + 1, vc]
            snew_ref[0, i, hd] = s_new
            o_ref[i:i + 1, vc] = jnp.sum(q_col * s_new, axis=0, keepdims=True)


def _smix_post_kernel(x_ref, gt_ref, gpost_ref, o_ref, sr_ref, yconv_ref, sga_ref, sgb_ref,
                      ggla_ref, wbo_ref, wmo_ref, y_ref, ygla_scr):
    for hd in range(N_HEADS):
        vc = slice(hd * DV, (hd + 1) * DV)
        on = _rms(o_ref[:, vc], ggla_ref[:, vc])
        ygla_scr[:, vc] = (on * sr_ref[:, vc]).astype(BF16)
    mix = _merge_and_project(ygla_scr[...], yconv_ref[...].astype(BF16), sga_ref[...],
                             sgb_ref[...], wbo_ref, wmo_ref)
    y_ref[...] = x_ref[...] + gt_ref[...] * _rms(mix, gpost_ref[1:2, :])


def _sample_mixer(x, ada, state_gla, state_conv, g_pre, g_post, wts):
    s = x.shape[0]
    wa, wz, wb, walpha, balpha, ggla, wconv, wbo, wmo = wts
    whole = lambda a: pl.BlockSpec(a.shape, lambda *_: (0,) * a.ndim)
    mod = lambda j: pl.BlockSpec((s, D_MODEL), lambda *_: (0, 3 + j))
    tok = jax.ShapeDtypeStruct((s, D_MODEL), F32)
    tr = jax.ShapeDtypeStruct((N_HEADS, DK, s), F32)

    pre_in = (x, ada, ada, g_pre, wa, wz, wb, walpha, balpha, wconv, state_conv)
    pre_specs = [whole(x), mod(0), mod(1)] + [whole(a) for a in pre_in[3:]]
    pre_out = [tr, tr, tr, tok, tok, tok, tok, tok, jax.ShapeDtypeStruct(state_conv.shape, F32)]
    qt, kt, at, v, sr, yconv, sga, sgb, sconv_new = pl.pallas_call(
        _smix_pre_kernel,
        grid=(1,),
        in_specs=pre_specs,
        out_specs=[whole(o) for o in pre_out],
        out_shape=pre_out,
        compiler_params=_params(("arbitrary",)),
        name="smix_pre",
    )(*pre_in)

    state_block = pl.BlockSpec((1, SAMPLE_BLOCK, N_HEADS, DK, DV), lambda j: (0, j, 0, 0, 0))
    snew, o = pl.pallas_call(
        _smix_state_kernel,
        grid=(s // SAMPLE_BLOCK,),
        in_specs=[state_block, whole(qt), whole(kt), whole(at),
                  pl.BlockSpec((SAMPLE_BLOCK, V_WIDTH), lambda j: (j, 0))],
        out_specs=[state_block, pl.BlockSpec((SAMPLE_BLOCK, V_WIDTH), lambda j: (j, 0))],
        out_shape=[jax.ShapeDtypeStruct(state_gla.shape, F32), tok],
        compiler_params=_params(("arbitrary",)),
        name="smix_state",
    )(state_gla, qt, kt, at, v)

    post_in = (x, ada, g_post, o, sr, yconv, sga, sgb, ggla, wbo, wmo)
    post_specs = [whole(x), mod(2)] + [whole(a) for a in post_in[2:]]
    y = pl.pallas_call(
        _smix_post_kernel,
        grid=(1,),
        in_specs=post_specs,
        out_specs=whole(tok),
        out_shape=tok,
        scratch_shapes=[pltpu.VMEM((s, V_WIDTH), BF16)],
        compiler_params=_params(("arbitrary",)),
        name="smix_post",
    )(*post_in)
    return y, snew, sconv_new


def kernel(x_prompt, x_sample, state_gla, state_conv, c_prompt, c_sample, w_ada, b_ada, g_pre,
           g_post, w_ffn1_in, w_ffn1_out, w_ffn2_in, w_ffn2_out, w_mix_in, w_alpha, b_alpha,
           g_gla_norm, w_conv, w_branch_out, w_mix_out):
    depth = w_ada.shape[0]
    n_s = x_sample.shape[0]
    assert depth == 1 and x_sample.shape[1] == 1 and n_s == LANES

    yp = x_prompt
    ys = x_sample.reshape(1, n_s, D_MODEL)
    outs = []
    for i in range(depth):
        ada_s, ada_p = _ada(c_sample, c_prompt, w_ada[i], b_ada[i])

        ffn1_w = (w_ffn1_in[i].astype(BF16),
                  jnp.pad(w_ffn1_out[i].astype(BF16), ((0, 0), (0, LANES))))
        n_ffn_steps = yp.shape[0] * (yp.shape[1] // FFN_TM)
        n_mix_steps = yp.shape[0] * (yp.shape[1] // MIX_TM)
        w_mix_t = jnp.swapaxes(w_mix_in[i], 0, 1)
        mixer_jobs = (
            _CastJob(w_mix_t, MIX_QKVR_CAST_ROWS, _A_WIDTH // MIX_QKVR_CAST_ROWS, transpose=True,
                     pad=1),
            _CastJob(w_mix_t, LANES, 1, transpose=True, row0=_A_WIDTH, valid=GATE_RANK),
            _CastJob(w_mix_t, MIX_CONV_CAST_ROWS, _B_WIDTH // MIX_CONV_CAST_ROWS, transpose=True,
                     row0=_A_WIDTH + GATE_RANK, pad=1),
            _whole_cast(w_branch_out[i].reshape(2 * D_MODEL, D_MODEL), n_ffn_steps, pad=LANES),
            _whole_cast(w_mix_out[i], n_ffn_steps, pad=LANES))
        assert all(j.n_steps <= n_ffn_steps for j in mixer_jobs)
        assert all((j.n_steps * j.block_rows if j.transpose else j.src.shape[1] + j.pad)
                   % (WEIGHT_PITCH_TILES * LANES) for j in mixer_jobs)
        yp, ys, wa, wz, wb, wbo, wmo = _ffn(yp, ys, ada_p, ada_s, 0, g_pre[i], g_post[i], *ffn1_w,
                                            jobs=mixer_jobs)
        walpha = jnp.pad(w_alpha[i], ((0, LANES - GATE_RANK), (0, 0))).astype(BF16)
        mix_w = (wa, wz, wb, walpha, b_alpha[i].reshape(1, QK_WIDTH),
                 g_gla_norm[i].reshape(1, V_WIDTH),
                 jnp.pad(w_conv[i], ((0, SUBLANES - CONV_WIDTH), (0, 0))),
                 wbo.reshape(2, D_MODEL, D_MODEL + LANES), wmo)
        ys, gla_s, sconv_new = _sample_mixer(ys[0], ada_s, state_gla[i:i + 1], state_conv[i],
                                             g_pre[i], g_post[i], mix_w)
        yp, gla_p, conv_p, *ffn2_w = _mixer(
            yp, ada_p, g_pre[i], g_post[i], mix_w,
            jobs=(_whole_cast(w_ffn2_in[i], n_mix_steps),
                  _whole_cast(w_ffn2_out[i], n_mix_steps, pad=LANES)))
        yp, ys = _ffn(yp, ys.reshape(1, n_s, D_MODEL), ada_p, ada_s, 2, g_pre[i], g_post[i],
                      *ffn2_w)
        outs.append((gla_p, conv_p, gla_s, sconv_new[None]))

    gla_p, conv_p, gla_s, conv_s = outs[0]
    return (yp, ys.reshape(n_s, 1, D_MODEL), gla_p, conv_p, gla_s, conv_s)
```

```python
import functools
from typing import NamedTuple

import jax
import jax.numpy as jnp
from jax import lax
from jax.experimental import pallas as pl
from jax.experimental.pallas import tpu as pltpu

F32 = jnp.float32
BF16 = jnp.bfloat16

D_MODEL = 1024
D_FF = 2816
N_HEADS = 4
DK = 128
DV = 256
QK_WIDTH = N_HEADS * DK
V_WIDTH = N_HEADS * DV
GATE_RANK = 16
GATE_TAU = 16.0
D_CONV = D_MODEL
CONV_WIDTH = 3
RMS_EPS = 1e-6
MACARON_WEIGHT = 0.5

LANES = 128
SUBLANES = 8
GLA_CHUNK = 64
VMEM_LIMIT_BYTES = 56 * 1024 * 1024

FFN_TM = 1024
FFN_PART = 256
MIX_TM = 512
MIX_PART = 256
ADA_TN = 1536
SAMPLE_BLOCK = 16

_A_WIDTH = 2 * QK_WIDTH + 2 * V_WIDTH
_B_WIDTH = 3 * D_CONV + 2 * D_MODEL


def _rms(x, g):
    return x * lax.rsqrt(jnp.mean(x * x, axis=-1, keepdims=True) + RMS_EPS) * g


def _silu(x):
    return x * jax.nn.sigmoid(x)


def _dot(a, b):
    return jnp.dot(a, b, preferred_element_type=F32)


def _dot_nt(a, b):
    return lax.dot_general(a, b, (((1,), (1,)), ((), ())), preferred_element_type=F32)


def _dot_tn(a, b):
    return lax.dot_general(a, b, (((0,), (0,)), ((), ())), preferred_element_type=F32)


def _resident(shape):
    zeros = (0,) * len(shape)
    return pl.BlockSpec(shape, lambda *_: zeros, pipeline_mode=pl.Buffered(1))


def _params(semantics):
    return pltpu.CompilerParams(dimension_semantics=semantics, vmem_limit_bytes=VMEM_LIMIT_BYTES)


def _ada_kernel(cs_ref, cp_ref, w_ref, b_ref, os_ref, op_ref):
    w = w_ref[...].astype(BF16)
    for c_ref, o_ref in ((cs_ref, os_ref), (cp_ref, op_ref)):
        o_ref[...] = _dot(_silu(c_ref[...]).astype(BF16), w) + b_ref[...]


def _ada(c_sample, c_prompt, w_ada, b_ada):
    width = w_ada.shape[1]
    rows = lambda c: pl.BlockSpec((c.shape[0], D_MODEL), lambda j: (0, 0))
    cols = lambda c: pl.BlockSpec((c.shape[0], ADA_TN), lambda j: (0, j))
    return pl.pallas_call(
        _ada_kernel,
        grid=(width // ADA_TN,),
        in_specs=[rows(c_sample), rows(c_prompt),
                  pl.BlockSpec((D_MODEL, ADA_TN), lambda j: (0, j)),
                  pl.BlockSpec((1, ADA_TN), lambda j: (0, j))],
        out_specs=[cols(c_sample), cols(c_prompt)],
        out_shape=[jax.ShapeDtypeStruct((c.shape[0], width), F32) for c in (c_sample, c_prompt)],
        compiler_params=_params(("arbitrary",)),
        name="ada",
    )(c_sample, c_prompt, w_ada, b_ada.reshape(1, width))


def _mod_specs(sub, ada):
    return [pl.BlockSpec((ada.shape[0], D_MODEL), lambda *_, col=sub * 3 + j: (0, col))
            for j in range(3)]


def _load_mod(ref, row):
    if row is None:
        return ref[...]
    return ref[pl.ds(row, 1), :]


BF16_SUBLANES = 16
WEIGHT_PITCH_TILES = 8
MIX_QKVR_CAST_ROWS = 2 * LANES
MIX_CONV_CAST_ROWS = 4 * LANES


class _CastJob(NamedTuple):
    src: jax.Array
    block_rows: int
    n_blocks: int
    transpose: bool = False
    row0: int = 0
    valid: int = 0
    pad: int = 0

    @property
    def n_steps(self):
        return self.n_blocks + (self.pad if self.transpose else 0)

    def specs(self, step_of):
        cols = self.src.shape[1]
        r, nb = self.block_rows, self.n_blocks
        block = lambda *g: jnp.minimum(step_of(*g), nb - 1)
        if not self.transpose:
            return (pl.BlockSpec((r, cols), lambda *g: (block(*g), 0)),
                    pl.BlockSpec((r, cols + self.pad), lambda *g: (block(*g), 0)),
                    jax.ShapeDtypeStruct((nb * r, cols + self.pad), BF16))
        assert self.row0 % SUBLANES == 0 and r % SUBLANES == 0
        out_block = lambda *g: jnp.minimum(step_of(*g), self.n_steps - 1)
        return (pl.BlockSpec((pl.Element(r), pl.Element(cols)),
                             lambda *g: (pl.multiple_of(self.row0 + r * block(*g), SUBLANES), 0)),
                pl.BlockSpec((cols, r), lambda *g: (0, out_block(*g))),
                jax.ShapeDtypeStruct((cols, self.n_steps * r), BF16))

    @property
    def static(self):
        return (self.transpose, self.valid or self.block_rows, self.n_blocks, self.pad)


def _whole_cast(w, n_steps, pad=0):
    rows = w.shape[0]
    r = next(r for r in range(BF16_SUBLANES, rows + 1, BF16_SUBLANES)
             if rows % r == 0 and rows // r <= n_steps)
    return _CastJob(w, r, rows // r, pad=pad)


def _cast_plumbing(jobs, step_of):
    specs = [job.specs(step_of) for job in jobs]
    return ([i for i, _, _ in specs], [o for _, o, _ in specs], [s for _, _, s in specs])


def _run_casts(statics, step, src_refs, dst_refs):
    for (transpose, valid, n_blocks, pad), src, dst in zip(statics, src_refs, dst_refs,
                                                           strict=True):
        blk = src[...]
        if not transpose:
            cols = blk.shape[1]
            dst[:, :cols] = blk.astype(BF16)
            if pad:
                dst[:, cols:] = jnp.zeros((blk.shape[0], pad), BF16)
            continue
        blk = blk.T
        if valid < blk.shape[1]:
            lane = lax.broadcasted_iota(jnp.int32, blk.shape, 1)
            blk = jnp.where(lane < valid, blk, 0.0)
        if pad:
            blk = jnp.where(step < n_blocks, blk, 0.0)
        dst[...] = blk.astype(BF16)


def _ffn_rows(x_ref, o_ref, mods, per_token, gpre_ref, gpost_ref, win_ref, wout_ref):
    shift, scale, gate = mods
    tm = x_ref.shape[1]
    parts = max(tm // FFN_PART, 1)
    rows = [slice(i * (tm // parts), (i + 1) * (tm // parts)) for i in range(parts)]
    mod = lambda m, rs: m[rs] if per_token else m

    def pre(rs):
        return (_rms(x_ref[0, rs, :], gpre_ref[...]) * (1.0 + mod(scale, rs))
                + mod(shift, rs)).astype(BF16)

    def post(rs, out):
        o_ref[0, rs, :] = (x_ref[0, rs, :]
                           + MACARON_WEIGHT * mod(gate, rs) * _rms(out, gpost_ref[...]))

    h = pre(rows[0])
    for i in range(parts):
        g = _dot(h, win_ref[:, :D_FF])
        h_next = pre(rows[i + 1]) if i + 1 < parts else None
        u = _dot(h, win_ref[:, D_FF:])
        a = (_silu(g) * u).astype(BF16)
        if i > 0:
            post(rows[i - 1], out)
        out = _dot(a, wout_ref[:, :D_MODEL])
        h = h_next
    post(rows[parts - 1], out)


def _ffn_kernel(x_ref, sh_ref, sc_ref, gt_ref, xs_ref, shs_ref, scs_ref, gts_ref, gpre_ref,
                gpost_ref, win_ref, wout_ref, *rest, sub, n_l, job_groups):
    n_jobs = len(job_groups)
    o_ref, os_ref = rest[n_jobs:n_jobs + 2]
    step = pl.program_id(0)
    _run_casts(job_groups, step, rest[:n_jobs], rest[n_jobs + 2:])
    last = pl.num_programs(0) - 1
    weights = (gpre_ref.at[sub:sub + 1], gpost_ref.at[sub:sub + 1], win_ref, wout_ref)

    @pl.when(step < last)
    def _():
        mods = [_load_mod(r, step // n_l) for r in (sh_ref, sc_ref, gt_ref)]
        _ffn_rows(x_ref, o_ref, mods, False, *weights)

    @pl.when(step == last)
    def _():
        mods = [_load_mod(r, None) for r in (shs_ref, scs_ref, gts_ref)]
        _ffn_rows(xs_ref, os_ref, mods, True, *weights)


def _ffn(x, xs, ada_p, ada_s, sub, g_pre, g_post, w_in, w_out, *, jobs=()):
    n, l, _ = x.shape
    n_l = l // FFN_TM
    n_tiles = n * n_l
    tile = lambda i: (jnp.minimum(i, n_tiles - 1) // n_l, jnp.minimum(i, n_tiles - 1) % n_l, 0)
    whole = lambda i: (0, 0, 0)
    kern = functools.partial(_ffn_kernel, sub=sub, n_l=n_l,
                             job_groups=tuple(j.static for j in jobs))
    cast_in, cast_out, cast_shapes = _cast_plumbing(jobs, lambda i: i)
    return pl.pallas_call(
        kern,
        grid=(n_tiles + 1,),
        in_specs=[pl.BlockSpec((1, FFN_TM, D_MODEL), tile)]
        + _mod_specs(sub, ada_p)
        + [pl.BlockSpec(xs.shape, whole)]
        + _mod_specs(sub, ada_s)
        + [_resident(g_pre.shape), _resident(g_post.shape),
           _resident(w_in.shape), _resident(w_out.shape)]
        + cast_in,
        out_specs=[pl.BlockSpec((1, FFN_TM, D_MODEL), tile), pl.BlockSpec(xs.shape, whole)]
        + cast_out,
        out_shape=[jax.ShapeDtypeStruct(x.shape, F32), jax.ShapeDtypeStruct(xs.shape, F32)]
        + cast_shapes,
        compiler_params=_params(("arbitrary",)),
        name="ffn",
    )(x, ada_p, ada_p, ada_p, xs, ada_s, ada_s, ada_s, g_pre, g_post, w_in, w_out,
      *[j.src for j in jobs])


def _log_decay(hb, wz_ref, walpha_ref, balpha_ref):
    z = _dot(hb, wz_ref[...]).astype(BF16)
    xg = _dot(z, walpha_ref[...]) + balpha_ref[...]
    return jax.nn.log_sigmoid(xg) * (1.0 / GATE_TAU)


def _merge_and_project(y_gla, y_conv, sig_a, sig_b, wbo_ref, wmo_ref):
    pg = _dot(y_gla, wbo_ref[0, :, :D_MODEL])
    pc = _dot(y_conv, wbo_ref[1, :, :D_MODEL])
    merged = (sig_a * pg + sig_b * pc).astype(BF16)
    return _dot(merged, wmo_ref[:, :D_MODEL])


_PAIR_LEVELS = tuple(GLA_CHUNK >> i for i in range(1, GLA_CHUNK.bit_length()))


def _pair_operands(qs, k, b, log_a, m, tm):
    if m >= SUBLANES:
        halves = lambda a: a.reshape(tm // (2 * m), 2, m, QK_WIDTH)
        b4, k4, q4 = halves(b), halves(k), halves(qs)
        mid = b4[:, 0:1, m - 1:m, :]
        w = jnp.exp(jnp.concatenate([mid - b4[:, 0:1], b4[:, 1:2] - mid], axis=1))
        src = jnp.concatenate([k4[:, 0:1], q4[:, 1:2]], axis=1)
        return (src * w).astype(BF16).reshape(tm, QK_WIDTH)
    tiles = lambda a: a.reshape(tm // SUBLANES, SUBLANES, QK_WIDTH)
    sub = lax.broadcasted_iota(jnp.int32, (1, SUBLANES, QK_WIDTH), 1)
    second = (sub & m) != 0
    if m == 1:
        w = jnp.exp(jnp.where(second, tiles(log_a), 0.0))
    else:
        b4 = tiles(b)
        mid = b4[:, m - 1:m, :]
        for start in range(2 * m, SUBLANES, 2 * m):
            mid = jnp.where(sub >= start, b4[:, start + m - 1:start + m, :], mid)
        w = jnp.exp((b4 - mid) * jnp.where(second, 1.0, -1.0))
    return (jnp.where(second, tiles(qs), tiles(k)) * w).astype(BF16).reshape(tm, QK_WIDTH)


def _pair_mask(m):
    t = lax.broadcasted_iota(jnp.int32, (GLA_CHUNK, GLA_CHUNK), 0)
    s = lax.broadcasted_iota(jnp.int32, (GLA_CHUNK, GLA_CHUNK), 1)
    return ((t & m) != 0) & ((s & m) == 0) & ((t // (2 * m)) == (s // (2 * m)))


def _mixer_kernel(x_ref, sh_ref, sc_ref, gt_ref, gpre_ref, gpost_ref, wa_ref, wz_ref, wb_ref,
                  walpha_ref, balpha_ref, ggla_ref, wconv_ref, wbo_ref, wmo_ref, *rest,
                  job_groups):
    n_jobs = len(job_groups)
    y_ref, sgla_ref, sconv_ref = rest[n_jobs:n_jobs + 3]
    st_scr, u_scr, ygla_scr = rest[-3:]
    l = pl.program_id(1)
    n_l = pl.num_programs(1)
    _run_casts(job_groups, pl.program_id(0) * n_l + l, rest[:n_jobs], rest[n_jobs + 3:-3])

    @pl.when(l == 0)
    def _():
        st_scr[...] = jnp.zeros_like(st_scr)
        u_scr[0:SUBLANES, :] = jnp.zeros((SUBLANES, D_CONV), F32)

    mods = [_load_mod(r, pl.program_id(0)) for r in (sh_ref, sc_ref, gt_ref)]
    weights = (gpre_ref.at[1:2], gpost_ref.at[1:2], wa_ref, wz_ref, wb_ref, walpha_ref,
               balpha_ref, ggla_ref, wconv_ref, wbo_ref, wmo_ref)
    n_parts = x_ref.shape[1] // MIX_PART
    parts = [_mixer_part(x_ref, y_ref, mods, weights, (st_scr, u_scr, ygla_scr),
                         slice(i * MIX_PART, (i + 1) * MIX_PART)) for i in range(n_parts)]
    next(parts[0])
    for i, p in enumerate(parts):
        next(p)
        if i + 1 < n_parts:
            next(parts[i + 1])
        if i > 0:
            next(parts[i - 1], None)
        next(p)
    next(parts[-1], None)

    @pl.when(l == n_l - 1)
    def _():
        sgla_ref[0, 0] = st_scr[...]
        sconv_ref[0, 0] = u_scr[SUBLANES - 2:SUBLANES, :]


def _mixer_part(x_ref, y_ref, mods, weights, scratch, prow):
    shift, scale, gate = mods
    (gpre_ref, gpost_ref, wa_ref, wz_ref, wb_ref, walpha_ref, balpha_ref, ggla_ref, wconv_ref,
     wbo_ref, wmo_ref) = weights
    st_scr, u_scr, ygla_scr = scratch
    ygla_scr = ygla_scr.at[prow]
    tm = MIX_PART
    x = x_ref[0, prow, :]
    hb = (_rms(x, gpre_ref[...]) * (1.0 + scale) + shift).astype(BF16)
    yield

    def proj_b(j):
        return _dot(hb, wb_ref[:, j * D_MODEL:(j + 1) * D_MODEL])

    nc = tm // GLA_CHUNK
    log_a = _log_decay(hb, wz_ref, walpha_ref, balpha_ref)
    pa = _dot(hb, wa_ref[:, :_A_WIDTH])
    qs = pa[:, 0:QK_WIDTH] * (DK ** -0.5)
    k = pa[:, QK_WIDTH:2 * QK_WIDTH]
    vb = pa[:, 2 * QK_WIDTH:2 * QK_WIDTH + V_WIDTH].astype(BF16)
    r = pa[:, 2 * QK_WIDTH + V_WIDTH:]

    row = lax.broadcasted_iota(jnp.int32, (tm, tm), 0)
    col = lax.broadcasted_iota(jnp.int32, (tm, tm), 1)
    tri = jnp.where((row >= col) & (row // GLA_CHUNK == col // GLA_CHUNK), 1.0, 0.0).astype(BF16)
    la_hi = log_a.astype(BF16)
    rem = log_a - la_hi.astype(F32)
    la_mid = rem.astype(BF16)
    la_lo = (rem - la_mid.astype(F32)).astype(BF16)
    b3 = (_dot(tri, la_hi) + _dot(tri, la_mid) + _dot(tri, la_lo)).reshape(
        nc, GLA_CHUNK, QK_WIDTH)

    u = proj_b(1) * proj_b(2)
    u_scr[SUBLANES:SUBLANES + tm, :] = u
    yield

    b_last = b3[:, GLA_CHUNK - 1:GLA_CHUNK, :]
    b = b3.reshape(tm, QK_WIDTH)
    qd = (qs * jnp.exp(b)).astype(BF16).reshape(nc, GLA_CHUNK, QK_WIDTH)
    kl = (k.reshape(nc, GLA_CHUNK, QK_WIDTH) * jnp.exp(b_last - b3)).astype(BF16)
    b_last_rows = jnp.concatenate(
        [b_last[c, :, hd * DK:(hd + 1) * DK] for c in range(nc) for hd in range(N_HEADS)]
        + [jnp.zeros((LANES - nc * N_HEADS, DK), F32)], axis=0)
    decay_cols = jnp.exp(b_last_rows.T)

    pair_ops = [_pair_operands(qs, k, b, log_a, m, tm) for m in _PAIR_LEVELS]
    pair_masks = [_pair_mask(m) for m in _PAIR_LEVELS]
    qs_b, k_b = qs.astype(BF16), k.astype(BF16)
    crow = lax.broadcasted_iota(jnp.int32, (GLA_CHUNK, GLA_CHUNK), 0)
    ccol = lax.broadcasted_iota(jnp.int32, (GLA_CHUNK, GLA_CHUNK), 1)
    scores, upd = {}, {}
    for c in range(nc):
        rows = slice(c * GLA_CHUNK, (c + 1) * GLA_CHUNK)
        for hd in range(N_HEADS):
            kc = slice(hd * DK, (hd + 1) * DK)
            s = jnp.where(crow == ccol, _dot_nt(qs_b[rows, kc], k_b[rows, kc]), 0.0)
            for p, mask in zip(pair_ops, pair_masks):
                s = jnp.where(mask, _dot_nt(p[rows, kc], p[rows, kc]), s)
            scores[c, hd] = s.astype(BF16)
            upd[c, hd] = _dot_tn(kl[c, :, kc], vb[rows, hd * DV:(hd + 1) * DV])

    conv = (wconv_ref[0:1, :] * u_scr[SUBLANES - 2:SUBLANES - 2 + tm, :]
            + wconv_ref[1:2, :] * u_scr[SUBLANES - 1:SUBLANES - 1 + tm, :]
            + wconv_ref[2:3, :] * u)
    u_scr[0:SUBLANES, :] = u_scr[tm:tm + SUBLANES, :]
    y_conv = (proj_b(0) * conv).astype(BF16)

    outs = {}
    for hd in range(N_HEADS):
        kc = slice(hd * DK, (hd + 1) * DK)
        vc = slice(hd * DV, (hd + 1) * DV)
        st = st_scr[hd]
        for c in range(nc):
            rows = slice(c * GLA_CHUNK, (c + 1) * GLA_CHUNK)
            lhs = jnp.concatenate([qd[c, :, kc], scores[c, hd]], axis=1)
            rhs = jnp.concatenate([st.astype(BF16), vb[rows, vc]], axis=0)
            outs[c, hd] = _dot(lhs, rhs)
            j = c * N_HEADS + hd
            st = st * decay_cols[:, j:j + 1] + upd[c, hd]
        st_scr[hd] = st

    sig_a = jax.nn.sigmoid(proj_b(3))
    sig_b = jax.nn.sigmoid(proj_b(4))

    for hd in range(N_HEADS):
        vc = slice(hd * DV, (hd + 1) * DV)
        for c in range(nc):
            rows = slice(c * GLA_CHUNK, (c + 1) * GLA_CHUNK)
            on = _rms(outs[c, hd], ggla_ref[:, vc])
            ygla_scr[rows, vc] = (on * _silu(r[rows, vc])).astype(BF16)

    yield
    mix = _merge_and_project(ygla_scr[...], y_conv, sig_a, sig_b, wbo_ref, wmo_ref)
    y_ref[0, prow, :] = x + gate * _rms(mix, gpost_ref[...])


def _mixer(x, ada, g_pre, g_post, wts, *, jobs=()):
    n, l, _ = x.shape
    tm = MIX_TM
    n_l = l // tm
    kern = functools.partial(_mixer_kernel, job_groups=tuple(j.static for j in jobs))
    cast_in, cast_out, cast_shapes = _cast_plumbing(jobs, lambda n, l: n * n_l + l)
    wa, wz, wb, walpha, balpha, ggla, wconv, wbo, wmo = wts
    return pl.pallas_call(
        kern,
        grid=(n, n_l),
        in_specs=[pl.BlockSpec((1, tm, D_MODEL), lambda n, l: (n, l, 0))]
        + _mod_specs(1, ada)
        + [_resident(a.shape) for a in (g_pre, g_post, wa, wz, wb, walpha, balpha, ggla, wconv,
                                        wbo, wmo)]
        + cast_in,
        out_specs=[
            pl.BlockSpec((1, tm, D_MODEL), lambda n, l: (n, l, 0)),
            pl.BlockSpec((1, 1, N_HEADS, DK, DV), lambda n, l: (0, n, 0, 0, 0)),
            pl.BlockSpec((1, 1, CONV_WIDTH - 1, D_CONV), lambda n, l: (0, n, 0, 0)),
        ] + cast_out,
        out_shape=[
            jax.ShapeDtypeStruct(x.shape, F32),
            jax.ShapeDtypeStruct((1, n, N_HEADS, DK, DV), F32),
            jax.ShapeDtypeStruct((1, n, CONV_WIDTH - 1, D_CONV), F32),
        ] + cast_shapes,
        scratch_shapes=[
            pltpu.VMEM((N_HEADS, DK, DV), F32),
            pltpu.VMEM((MIX_PART + SUBLANES, D_CONV), F32),
            pltpu.VMEM((tm, V_WIDTH), BF16),
        ],
        compiler_params=_params(("arbitrary", "arbitrary")),
        name="mixer",
    )(x, ada, ada, ada, g_pre, g_post, wa, wz, wb, walpha, balpha, ggla, wconv, wbo, wmo,
      *[j.src for j in jobs])


def _smix_pre_kernel(x_ref, sh_ref, sc_ref, gpre_ref, wa_ref, wz_ref, wb_ref, walpha_ref,
                     balpha_ref, wconv_ref, sconv_ref,
                     qt_ref, kt_ref, at_ref, v_ref, sr_ref, yconv_ref, sga_ref, sgb_ref,
                     sconv_new_ref):
    x = x_ref[...]
    hb = (_rms(x, gpre_ref[1:2, :]) * (1.0 + sc_ref[...]) + sh_ref[...]).astype(BF16)
    pa = _dot(hb, wa_ref[:, :_A_WIDTH])
    q = pa[:, 0:QK_WIDTH] * (DK ** -0.5)
    k = pa[:, QK_WIDTH:2 * QK_WIDTH]
    v_ref[...] = pa[:, 2 * QK_WIDTH:2 * QK_WIDTH + V_WIDTH]
    sr_ref[...] = _silu(pa[:, 2 * QK_WIDTH + V_WIDTH:])
    a = jnp.exp(_log_decay(hb, wz_ref, walpha_ref, balpha_ref))
    for hd in range(N_HEADS):
        kc = slice(hd * DK, (hd + 1) * DK)
        qt_ref[hd] = q[:, kc].T
        kt_ref[hd] = k[:, kc].T
        at_ref[hd] = a[:, kc].T

    pb = _dot(hb, wb_ref[:, :_B_WIDTH])
    u = pb[:, D_CONV:2 * D_CONV] * pb[:, 2 * D_CONV:3 * D_CONV]
    conv = (wconv_ref[0:1, :] * sconv_ref[:, 0, :] + wconv_ref[1:2, :] * sconv_ref[:, 1, :]
            + wconv_ref[2:3, :] * u)
    yconv_ref[...] = pb[:, 0:D_CONV] * conv
    sga_ref[...] = jax.nn.sigmoid(pb[:, 3 * D_CONV:3 * D_CONV + D_MODEL])
    sgb_ref[...] = jax.nn.sigmoid(pb[:, 3 * D_CONV + D_MODEL:])
    sconv_new_ref[:, 0, :] = sconv_ref[:, 1, :]
    sconv_new_ref[:, 1, :] = u


def _smix_state_kernel(s_ref, qt_ref, kt_ref, at_ref, v_ref, snew_ref, o_ref):
    base = pl.program_id(0) * SAMPLE_BLOCK
    lane = lax.broadcasted_iota(jnp.int32, (1, LANES), 1)
    for i in range(SAMPLE_BLOCK):
        pick = jnp.where(lane == base + i, 1.0, 0.0)
        for hd in range(N_HEADS):
            vc = slice(hd * DV, (hd + 1) * DV)
            a_col = jnp.sum(at_ref[hd] * pick, axis=1, keepdims=True)
            k_col = jnp.sum(kt_ref[hd] * pick, axis=1, keepdims=True)
            q_col = jnp.sum(qt_ref[hd] * pick, axis=1, keepdims=True)
            s_new = a_col * s_ref[0, i, hd] + k_col * v_ref[i:i + 1, vc]
            snew_ref[0, i, hd] = s_new
            o_ref[i:i + 1, vc] = jnp.sum(q_col * s_new, axis=0, keepdims=True)


def _smix_post_kernel(x_ref, gt_ref, gpost_ref, o_ref, sr_ref, yconv_ref, sga_ref, sgb_ref,
                      ggla_ref, wbo_ref, wmo_ref, y_ref, ygla_scr):
    for hd in range(N_HEADS):
        vc = slice(hd * DV, (hd + 1) * DV)
        on = _rms(o_ref[:, vc], ggla_ref[:, vc])
        ygla_scr[:, vc] = (on * sr_ref[:, vc]).astype(BF16)
    mix = _merge_and_project(ygla_scr[...], yconv_ref[...].astype(BF16), sga_ref[...],
                             sgb_ref[...], wbo_ref, wmo_ref)
    y_ref[...] = x_ref[...] + gt_ref[...] * _rms(mix, gpost_ref[1:2, :])


def _sample_mixer(x, ada, state_gla, state_conv, g_pre, g_post, wts):
    s = x.shape[0]
    wa, wz, wb, walpha, balpha, ggla, wconv, wbo, wmo = wts
    whole = lambda a: pl.BlockSpec(a.shape, lambda *_: (0,) * a.ndim)
    mod = lambda j: pl.BlockSpec((s, D_MODEL), lambda *_: (0, 3 + j))
    tok = jax.ShapeDtypeStruct((s, D_MODEL), F32)
    tr = jax.ShapeDtypeStruct((N_HEADS, DK, s), F32)

    pre_in = (x, ada, ada, g_pre, wa, wz, wb, walpha, balpha, wconv, state_conv)
    pre_specs = [whole(x), mod(0), mod(1)] + [whole(a) for a in pre_in[3:]]
    pre_out = [tr, tr, tr, tok, tok, tok, tok, tok, jax.ShapeDtypeStruct(state_conv.shape, F32)]
    qt, kt, at, v, sr, yconv, sga, sgb, sconv_new = pl.pallas_call(
        _smix_pre_kernel,
        grid=(1,),
        in_specs=pre_specs,
        out_specs=[whole(o) for o in pre_out],
        out_shape=pre_out,
        compiler_params=_params(("arbitrary",)),
        name="smix_pre",
    )(*pre_in)

    state_block = pl.BlockSpec((1, SAMPLE_BLOCK, N_HEADS, DK, DV), lambda j: (0, j, 0, 0, 0))
    snew, o = pl.pallas_call(
        _smix_state_kernel,
        grid=(s // SAMPLE_BLOCK,),
        in_specs=[state_block, whole(qt), whole(kt), whole(at),
                  pl.BlockSpec((SAMPLE_BLOCK, V_WIDTH), lambda j: (j, 0))],
        out_specs=[state_block, pl.BlockSpec((SAMPLE_BLOCK, V_WIDTH), lambda j: (j, 0))],
        out_shape=[jax.ShapeDtypeStruct(state_gla.shape, F32), tok],
        compiler_params=_params(("arbitrary",)),
        name="smix_state",
    )(state_gla, qt, kt, at, v)

    post_in = (x, ada, g_post, o, sr, yconv, sga, sgb, ggla, wbo, wmo)
    post_specs = [whole(x), mod(2)] + [whole(a) for a in post_in[2:]]
    y = pl.pallas_call(
        _smix_post_kernel,
        grid=(1,),
        in_specs=post_specs,
        out_specs=whole(tok),
        out_shape=tok,
        scratch_shapes=[pltpu.VMEM((s, V_WIDTH), BF16)],
        compiler_params=_params(("arbitrary",)),
        name="smix_post",
    )(*post_in)
    return y, snew, sconv_new


def kernel(x_prompt, x_sample, state_gla, state_conv, c_prompt, c_sample, w_ada, b_ada, g_pre,
           g_post, w_ffn1_in, w_ffn1_out, w_ffn2_in, w_ffn2_out, w_mix_in, w_alpha, b_alpha,
           g_gla_norm, w_conv, w_branch_out, w_mix_out):
    depth = w_ada.shape[0]
    n_s = x_sample.shape[0]
    assert depth == 1 and x_sample.shape[1] == 1 and n_s == LANES

    yp = x_prompt
    ys = x_sample.reshape(1, n_s, D_MODEL)
    outs = []
    for i in range(depth):
        ada_s, ada_p = _ada(c_sample, c_prompt, w_ada[i], b_ada[i])

        ffn1_w = (w_ffn1_in[i].astype(BF16),
                  jnp.pad(w_ffn1_out[i].astype(BF16), ((0, 0), (0, LANES))))
        n_ffn_steps = yp.shape[0] * (yp.shape[1] // FFN_TM)
        n_mix_steps = yp.shape[0] * (yp.shape[1] // MIX_TM)
        w_mix_t = jnp.swapaxes(w_mix_in[i], 0, 1)
        mixer_jobs = (
            _CastJob(w_mix_t, MIX_QKVR_CAST_ROWS, _A_WIDTH // MIX_QKVR_CAST_ROWS, transpose=True,
                     pad=1),
            _CastJob(w_mix_t, LANES, 1, transpose=True, row0=_A_WIDTH, valid=GATE_RANK),
            _CastJob(w_mix_t, MIX_CONV_CAST_ROWS, _B_WIDTH // MIX_CONV_CAST_ROWS, transpose=True,
                     row0=_A_WIDTH + GATE_RANK, pad=1),
            _whole_cast(w_branch_out[i].reshape(2 * D_MODEL, D_MODEL), n_ffn_steps, pad=LANES),
            _whole_cast(w_mix_out[i], n_ffn_steps, pad=LANES))
        assert all(j.n_steps <= n_ffn_steps for j in mixer_jobs)
        assert all((j.n_steps * j.block_rows if j.transpose else j.src.shape[1] + j.pad)
                   % (WEIGHT_PITCH_TILES * LANES) for j in mixer_jobs)
        yp, ys, wa, wz, wb, wbo, wmo = _ffn(yp, ys, ada_p, ada_s, 0, g_pre[i], g_post[i], *ffn1_w,
                                            jobs=mixer_jobs)
        walpha = jnp.pad(w_alpha[i], ((0, LANES - GATE_RANK), (0, 0))).astype(BF16)
        mix_w = (wa, wz, wb, walpha, b_alpha[i].reshape(1, QK_WIDTH),
                 g_gla_norm[i].reshape(1, V_WIDTH),
                 jnp.pad(w_conv[i], ((0, SUBLANES - CONV_WIDTH), (0, 0))),
                 wbo.reshape(2, D_MODEL, D_MODEL + LANES), wmo)
        ys, gla_s, sconv_new = _sample_mixer(ys[0], ada_s, state_gla[i:i + 1], state_conv[i],
                                             g_pre[i], g_post[i], mix_w)
        yp, gla_p, conv_p, *ffn2_w = _mixer(
            yp, ada_p, g_pre[i], g_post[i], mix_w,
            jobs=(_whole_cast(w_ffn2_in[i], n_mix_steps),
                  _whole_cast(w_ffn2_out[i], n_mix_steps, pad=LANES)))
        yp, ys = _ffn(yp, ys.reshape(1, n_s, D_MODEL), ada_p, ada_s, 2, g_pre[i], g_post[i],
                      *ffn2_w)
        outs.append((gla_p, conv_p, gla_s, sconv_new[None]))

    gla_p, conv_p, gla_s, conv_s = outs[0]
    return (yp, ys.reshape(n_s, 1, D_MODEL), gla_p, conv_p, gla_s, conv_s)
```

```python
import functools
from typing import NamedTuple

import jax
import jax.numpy as jnp
from jax import lax
from jax.experimental import pallas as pl
from jax.experimental.pallas import tpu as pltpu

F32 = jnp.float32
BF16 = jnp.bfloat16

D_MODEL = 1024
D_FF = 2816
N_HEADS = 4
DK = 128
DV = 256
QK_WIDTH = N_HEADS * DK
V_WIDTH = N_HEADS * DV
GATE_RANK = 16
GATE_TAU = 16.0
D_CONV = D_MODEL
CONV_WIDTH = 3
RMS_EPS = 1e-6
MACARON_WEIGHT = 0.5

LANES = 128
SUBLANES = 8
GLA_CHUNK = 64
VMEM_LIMIT_BYTES = 56 * 1024 * 1024

FFN_TM = 1024
FFN_PART = 256
MIX_TM = 512
MIX_PART = 256
ADA_TN = 1536
SAMPLE_BLOCK = 16

_A_WIDTH = 2 * QK_WIDTH + 2 * V_WIDTH
_B_WIDTH = 3 * D_CONV + 2 * D_MODEL


def _rms(x, g):
    return x * lax.rsqrt(jnp.mean(x * x, axis=-1, keepdims=True) + RMS_EPS) * g


def _silu(x):
    return x * jax.nn.sigmoid(x)


def _dot(a, b):
    return jnp.dot(a, b, preferred_element_type=F32)


def _dot_nt(a, b):
    return lax.dot_general(a, b, (((1,), (1,)), ((), ())), preferred_element_type=F32)


def _dot_tn(a, b):
    return lax.dot_general(a, b, (((0,), (0,)), ((), ())), preferred_element_type=F32)


def _resident(shape):
    zeros = (0,) * len(shape)
    return pl.BlockSpec(shape, lambda *_: zeros, pipeline_mode=pl.Buffered(1))


def _params(semantics):
    return pltpu.CompilerParams(dimension_semantics=semantics, vmem_limit_bytes=VMEM_LIMIT_BYTES)


def _ada_kernel(cs_ref, cp_ref, w_ref, b_ref, os_ref, op_ref):
    w = w_ref[...].astype(BF16)
    for c_ref, o_ref in ((cs_ref, os_ref), (cp_ref, op_ref)):
        o_ref[...] = _dot(_silu(c_ref[...]).astype(BF16), w) + b_ref[...]


def _ada(c_sample, c_prompt, w_ada, b_ada):
    width = w_ada.shape[1]
    rows = lambda c: pl.BlockSpec((c.shape[0], D_MODEL), lambda j: (0, 0))
    cols = lambda c: pl.BlockSpec((c.shape[0], ADA_TN), lambda j: (0, j))
    return pl.pallas_call(
        _ada_kernel,
        grid=(width // ADA_TN,),
        in_specs=[rows(c_sample), rows(c_prompt),
                  pl.BlockSpec((D_MODEL, ADA_TN), lambda j: (0, j)),
                  pl.BlockSpec((1, ADA_TN), lambda j: (0, j))],
        out_specs=[cols(c_sample), cols(c_prompt)],
        out_shape=[jax.ShapeDtypeStruct((c.shape[0], width), F32) for c in (c_sample, c_prompt)],
        compiler_params=_params(("arbitrary",)),
        name="ada",
    )(c_sample, c_prompt, w_ada, b_ada.reshape(1, width))


def _mod_specs(sub, ada):
    return [pl.BlockSpec((ada.shape[0], D_MODEL), lambda *_, col=sub * 3 + j: (0, col))
            for j in range(3)]


def _load_mod(ref, row):
    if row is None:
        return ref[...]
    return ref[pl.ds(row, 1), :]


BF16_SUBLANES = 16
WEIGHT_PITCH_TILES = 8
MIX_QKVR_CAST_ROWS = 2 * LANES
MIX_CONV_CAST_ROWS = 4 * LANES


class _CastJob(NamedTuple):
    src: jax.Array
    block_rows: int
    n_blocks: int
    transpose: bool = False
    row0: int = 0
    valid: int = 0
    pad: int = 0

    @property
    def n_steps(self):
        return self.n_blocks + (self.pad if self.transpose else 0)

    def specs(self, step_of):
        cols = self.src.shape[1]
        r, nb = self.block_rows, self.n_blocks
        block = lambda *g: jnp.minimum(step_of(*g), nb - 1)
        if not self.transpose:
            return (pl.BlockSpec((r, cols), lambda *g: (block(*g), 0)),
                    pl.BlockSpec((r, cols + self.pad), lambda *g: (block(*g), 0)),
                    jax.ShapeDtypeStruct((nb * r, cols + self.pad), BF16))
        assert self.row0 % SUBLANES == 0 and r % SUBLANES == 0
        out_block = lambda *g: jnp.minimum(step_of(*g), self.n_steps - 1)
        return (pl.BlockSpec((pl.Element(r), pl.Element(cols)),
                             lambda *g: (pl.multiple_of(self.row0 + r * block(*g), SUBLANES), 0)),
                pl.BlockSpec((cols, r), lambda *g: (0, out_block(*g))),
                jax.ShapeDtypeStruct((cols, self.n_steps * r), BF16))

    @property
    def static(self):
        return (self.transpose, self.valid or self.block_rows, self.n_blocks, self.pad)


def _whole_cast(w, n_steps, pad=0):
    rows = w.shape[0]
    r = next(r for r in range(BF16_SUBLANES, rows + 1, BF16_SUBLANES)
             if rows % r == 0 and rows // r <= n_steps)
    return _CastJob(w, r, rows // r, pad=pad)


def _cast_plumbing(jobs, step_of):
    specs = [job.specs(step_of) for job in jobs]
    return ([i for i, _, _ in specs], [o for _, o, _ in specs], [s for _, _, s in specs])


def _run_casts(statics, step, src_refs, dst_refs):
    for (transpose, valid, n_blocks, pad), src, dst in zip(statics, src_refs, dst_refs,
                                                           strict=True):
        blk = src[...]
        if not transpose:
            cols = blk.shape[1]
            dst[:, :cols] = blk.astype(BF16)
            if pad:
                dst[:, cols:] = jnp.zeros((blk.shape[0], pad), BF16)
            continue
        blk = blk.T
        if valid < blk.shape[1]:
            lane = lax.broadcasted_iota(jnp.int32, blk.shape, 1)
            blk = jnp.where(lane < valid, blk, 0.0)
        if pad:
            blk = jnp.where(step < n_blocks, blk, 0.0)
        dst[...] = blk.astype(BF16)


def _ffn_rows(x_ref, o_ref, mods, per_token, gpre_ref, gpost_ref, win_ref, wout_ref):
    shift, scale, gate = mods
    tm = x_ref.shape[1]
    parts = max(tm // FFN_PART, 1)
    rows = [slice(i * (tm // parts), (i + 1) * (tm // parts)) for i in range(parts)]
    mod = lambda m, rs: m[rs] if per_token else m

    def pre(rs):
        return (_rms(x_ref[0, rs, :], gpre_ref[...]) * (1.0 + mod(scale, rs))
                + mod(shift, rs)).astype(BF16)

    def post(rs, out):
        o_ref[0, rs, :] = (x_ref[0, rs, :]
                           + MACARON_WEIGHT * mod(gate, rs) * _rms(out, gpost_ref[...]))

    h = pre(rows[0])
    for i in range(parts):
        g = _dot(h, win_ref[:, :D_FF])
        h_next = pre(rows[i + 1]) if i + 1 < parts else None
        u = _dot(h, win_ref[:, D_FF:])
        a = (_silu(g) * u).astype(BF16)
        if i > 0:
            post(rows[i - 1], out)
        out = _dot(a, wout_ref[:, :D_MODEL])
        h = h_next
    post(rows[parts - 1], out)


def _ffn_kernel(x_ref, sh_ref, sc_ref, gt_ref, xs_ref, shs_ref, scs_ref, gts_ref, gpre_ref,
                gpost_ref, win_ref, wout_ref, *rest, sub, n_l, job_groups):
    n_jobs = len(job_groups)
    o_ref, os_ref = rest[n_jobs:n_jobs + 2]
    step = pl.program_id(0)
    _run_casts(job_groups, step, rest[:n_jobs], rest[n_jobs + 2:])
    last = pl.num_programs(0) - 1
    weights = (gpre_ref.at[sub:sub + 1], gpost_ref.at[sub:sub + 1], win_ref, wout_ref)

    @pl.when(step < last)
    def _():
        mods = [_load_mod(r, step // n_l) for r in (sh_ref, sc_ref, gt_ref)]
        _ffn_rows(x_ref, o_ref, mods, False, *weights)

    @pl.when(step == last)
    def _():
        mods = [_load_mod(r, None) for r in (shs_ref, scs_ref, gts_ref)]
        _ffn_rows(xs_ref, os_ref, mods, True, *weights)


def _ffn(x, xs, ada_p, ada_s, sub, g_pre, g_post, w_in, w_out, *, jobs=()):
    n, l, _ = x.shape
    n_l = l // FFN_TM
    n_tiles = n * n_l
    tile = lambda i: (jnp.minimum(i, n_tiles - 1) // n_l, jnp.minimum(i, n_tiles - 1) % n_l, 0)
    whole = lambda i: (0, 0, 0)
    kern = functools.partial(_ffn_kernel, sub=sub, n_l=n_l,
                             job_groups=tuple(j.static for j in jobs))
    cast_in, cast_out, cast_shapes = _cast_plumbing(jobs, lambda i: i)
    return pl.pallas_call(
        kern,
        grid=(n_tiles + 1,),
        in_specs=[pl.BlockSpec((1, FFN_TM, D_MODEL), tile)]
        + _mod_specs(sub, ada_p)
        + [pl.BlockSpec(xs.shape, whole)]
        + _mod_specs(sub, ada_s)
        + [_resident(g_pre.shape), _resident(g_post.shape),
           _resident(w_in.shape), _resident(w_out.shape)]
        + cast_in,
        out_specs=[pl.BlockSpec((1, FFN_TM, D_MODEL), tile), pl.BlockSpec(xs.shape, whole)]
        + cast_out,
        out_shape=[jax.ShapeDtypeStruct(x.shape, F32), jax.ShapeDtypeStruct(xs.shape, F32)]
        + cast_shapes,
        compiler_params=_params(("arbitrary",)),
        name="ffn",
    )(x, ada_p, ada_p, ada_p, xs, ada_s, ada_s, ada_s, g_pre, g_post, w_in, w_out,
      *[j.src for j in jobs])


def _log_decay(hb, wz_ref, walpha_ref, balpha_ref):
    z = _dot(hb, wz_ref[...]).astype(BF16)
    xg = _dot(z, walpha_ref[...]) + balpha_ref[...]
    return jax.nn.log_sigmoid(xg) * (1.0 / GATE_TAU)


def _merge_and_project(y_gla, y_conv, sig_a, sig_b, wbo_ref, wmo_ref):
    pg = _dot(y_gla, wbo_ref[0, :, :D_MODEL])
    pc = _dot(y_conv, wbo_ref[1, :, :D_MODEL])
    merged = (sig_a * pg + sig_b * pc).astype(BF16)
    return _dot(merged, wmo_ref[:, :D_MODEL])


_PAIR_LEVELS = tuple(GLA_CHUNK >> i for i in range(1, GLA_CHUNK.bit_length()))


def _pair_operands(qs, k, b, log_a, m, tm):
    if m >= SUBLANES:
        halves = lambda a: a.reshape(tm // (2 * m), 2, m, QK_WIDTH)
        b4, k4, q4 = halves(b), halves(k), halves(qs)
        mid = b4[:, 0:1, m - 1:m, :]
        w = jnp.exp(jnp.concatenate([mid - b4[:, 0:1], b4[:, 1:2] - mid], axis=1))
        src = jnp.concatenate([k4[:, 0:1], q4[:, 1:2]], axis=1)
        return (src * w).astype(BF16).reshape(tm, QK_WIDTH)
    tiles = lambda a: a.reshape(tm // SUBLANES, SUBLANES, QK_WIDTH)
    sub = lax.broadcasted_iota(jnp.int32, (1, SUBLANES, QK_WIDTH), 1)
    second = (sub & m) != 0
    if m == 1:
        w = jnp.exp(jnp.where(second, tiles(log_a), 0.0))
    else:
        b4 = tiles(b)
        mid = b4[:, m - 1:m, :]
        for start in range(2 * m, SUBLANES, 2 * m):
            mid = jnp.where(sub >= start, b4[:, start + m - 1:start + m, :], mid)
        w = jnp.exp((b4 - mid) * jnp.where(second, 1.0, -1.0))
    return (jnp.where(second, tiles(qs), tiles(k)) * w).astype(BF16).reshape(tm, QK_WIDTH)


def _pair_mask(m):
    t = lax.broadcasted_iota(jnp.int32, (GLA_CHUNK, GLA_CHUNK), 0)
    s = lax.broadcasted_iota(jnp.int32, (GLA_CHUNK, GLA_CHUNK), 1)
    return ((t & m) != 0) & ((s & m) == 0) & ((t // (2 * m)) == (s // (2 * m)))


def _mixer_kernel(x_ref, sh_ref, sc_ref, gt_ref, gpre_ref, gpost_ref, wa_ref, wz_ref, wb_ref,
                  walpha_ref, balpha_ref, ggla_ref, wconv_ref, wbo_ref, wmo_ref, *rest,
                  job_groups):
    n_jobs = len(job_groups)
    y_ref, sgla_ref, sconv_ref = rest[n_jobs:n_jobs + 3]
    st_scr, u_scr, ygla_scr = rest[-3:]
    l = pl.program_id(1)
    n_l = pl.num_programs(1)
    _run_casts(job_groups, pl.program_id(0) * n_l + l, rest[:n_jobs], rest[n_jobs + 3:-3])

    @pl.when(l == 0)
    def _():
        st_scr[...] = jnp.zeros_like(st_scr)
        u_scr[0:SUBLANES, :] = jnp.zeros((SUBLANES, D_CONV), F32)

    mods = [_load_mod(r, pl.program_id(0)) for r in (sh_ref, sc_ref, gt_ref)]
    weights = (gpre_ref.at[1:2], gpost_ref.at[1:2], wa_ref, wz_ref, wb_ref, walpha_ref,
               balpha_ref, ggla_ref, wconv_ref, wbo_ref, wmo_ref)
    n_parts = x_ref.shape[1] // MIX_PART
    parts = [_mixer_part(x_ref, y_ref, mods, weights, (st_scr, u_scr, ygla_scr),
                         slice(i * MIX_PART, (i + 1) * MIX_PART)) for i in range(n_parts)]
    next(parts[0])
    for i, p in enumerate(parts):
        next(p)
        if i + 1 < n_parts:
            next(parts[i + 1])
        if i > 0:
            next(parts[i - 1], None)
        next(p)
    next(parts[-1], None)

    @pl.when(l == n_l - 1)
    def _():
        sgla_ref[0, 0] = st_scr[...]
        sconv_ref[0, 0] = u_scr[SUBLANES - 2:SUBLANES, :]


def _mixer_part(x_ref, y_ref, mods, weights, scratch, prow):
    shift, scale, gate = mods
    (gpre_ref, gpost_ref, wa_ref, wz_ref, wb_ref, walpha_ref, balpha_ref, ggla_ref, wconv_ref,
     wbo_ref, wmo_ref) = weights
    st_scr, u_scr, ygla_scr = scratch
    ygla_scr = ygla_scr.at[prow]
    tm = MIX_PART
    x = x_ref[0, prow, :]
    hb = (_rms(x, gpre_ref[...]) * (1.0 + scale) + shift).astype(BF16)
    yield

    def proj_b(j):
        return _dot(hb, wb_ref[:, j * D_MODEL:(j + 1) * D_MODEL])

    nc = tm // GLA_CHUNK
    log_a = _log_decay(hb, wz_ref, walpha_ref, balpha_ref)
    pa = _dot(hb, wa_ref[:, :_A_WIDTH])
    qs = pa[:, 0:QK_WIDTH] * (DK ** -0.5)
    k = pa[:, QK_WIDTH:2 * QK_WIDTH]
    vb = pa[:, 2 * QK_WIDTH:2 * QK_WIDTH + V_WIDTH].astype(BF16)
    r = pa[:, 2 * QK_WIDTH + V_WIDTH:]

    row = lax.broadcasted_iota(jnp.int32, (tm, tm), 0)
    col = lax.broadcasted_iota(jnp.int32, (tm, tm), 1)
    tri = jnp.where((row >= col) & (row // GLA_CHUNK == col // GLA_CHUNK), 1.0, 0.0).astype(BF16)
    la_hi = log_a.astype(BF16)
    rem = log_a - la_hi.astype(F32)
    la_mid = rem.astype(BF16)
    la_lo = (rem - la_mid.astype(F32)).astype(BF16)
    b3 = (_dot(tri, la_hi) + _dot(tri, la_mid) + _dot(tri, la_lo)).reshape(
        nc, GLA_CHUNK, QK_WIDTH)

    u = proj_b(1) * proj_b(2)
    u_scr[SUBLANES:SUBLANES + tm, :] = u
    yield

    b_last = b3[:, GLA_CHUNK - 1:GLA_CHUNK, :]
    b = b3.reshape(tm, QK_WIDTH)
    qd = (qs * jnp.exp(b)).astype(BF16).reshape(nc, GLA_CHUNK, QK_WIDTH)
    kl = (k.reshape(nc, GLA_CHUNK, QK_WIDTH) * jnp.exp(b_last - b3)).astype(BF16)
    b_last_rows = jnp.concatenate(
        [b_last[c, :, hd * DK:(hd + 1) * DK] for c in range(nc) for hd in range(N_HEADS)]
        + [jnp.zeros((LANES - nc * N_HEADS, DK), F32)], axis=0)
    decay_cols = jnp.exp(b_last_rows.T)

    pair_ops = [_pair_operands(qs, k, b, log_a, m, tm) for m in _PAIR_LEVELS]
    pair_masks = [_pair_mask(m) for m in _PAIR_LEVELS]
    qs_b, k_b = qs.astype(BF16), k.astype(BF16)
    crow = lax.broadcasted_iota(jnp.int32, (GLA_CHUNK, GLA_CHUNK), 0)
    ccol = lax.broadcasted_iota(jnp.int32, (GLA_CHUNK, GLA_CHUNK), 1)
    scores, upd = {}, {}
    for c in range(nc):
        rows = slice(c * GLA_CHUNK, (c + 1) * GLA_CHUNK)
        for hd in range(N_HEADS):
            kc = slice(hd * DK, (hd + 1) * DK)
            s = jnp.where(crow == ccol, _dot_nt(qs_b[rows, kc], k_b[rows, kc]), 0.0)
            for p, mask in zip(pair_ops, pair_masks):
                s = jnp.where(mask, _dot_nt(p[rows, kc], p[rows, kc]), s)
            scores[c, hd] = s.astype(BF16)
            upd[c, hd] = _dot_tn(kl[c, :, kc], vb[rows, hd * DV:(hd + 1) * DV])

    conv = (wconv_ref[0:1, :] * u_scr[SUBLANES - 2:SUBLANES - 2 + tm, :]
            + wconv_ref[1:2, :] * u_scr[SUBLANES - 1:SUBLANES - 1 + tm, :]
            + wconv_ref[2:3, :] * u)
    u_scr[0:SUBLANES, :] = u_scr[tm:tm + SUBLANES, :]
    y_conv = (proj_b(0) * conv).astype(BF16)

    outs = {}
    for hd in range(N_HEADS):
        kc = slice(hd * DK, (hd + 1) * DK)
        vc = slice(hd * DV, (hd + 1) * DV)
        st = st_scr[hd]
        for c in range(nc):
            rows = slice(c * GLA_CHUNK, (c + 1) * GLA_CHUNK)
            lhs = jnp.concatenate([qd[c, :, kc], scores[c, hd]], axis=1)
            rhs = jnp.concatenate([st.astype(BF16), vb[rows, vc]], axis=0)
            outs[c, hd] = _dot(lhs, rhs)
            j = c * N_HEADS + hd
            st = st * decay_cols[:, j:j + 1] + upd[c, hd]
        st_scr[hd] = st

    sig_a = jax.nn.sigmoid(proj_b(3))
    sig_b = jax.nn.sigmoid(proj_b(4))

    for hd in range(N_HEADS):
        vc = slice(hd * DV, (hd + 1) * DV)
        for c in range(nc):
            rows = slice(c * GLA_CHUNK, (c + 1) * GLA_CHUNK)
            on = _rms(outs[c, hd], ggla_ref[:, vc])
            ygla_scr[rows, vc] = (on * _silu(r[rows, vc])).astype(BF16)

    yield
    mix = _merge_and_project(ygla_scr[...], y_conv, sig_a, sig_b, wbo_ref, wmo_ref)
    y_ref[0, prow, :] = x + gate * _rms(mix, gpost_ref[...])


def _mixer(x, ada, g_pre, g_post, wts, *, jobs=()):
    n, l, _ = x.shape
    tm = MIX_TM
    n_l = l // tm
    kern = functools.partial(_mixer_kernel, job_groups=tuple(j.static for j in jobs))
    cast_in, cast_out, cast_shapes = _cast_plumbing(jobs, lambda n, l: n * n_l + l)
    wa, wz, wb, walpha, balpha, ggla, wconv, wbo, wmo = wts
    return pl.pallas_call(
        kern,
        grid=(n, n_l),
        in_specs=[pl.BlockSpec((1, tm, D_MODEL), lambda n, l: (n, l, 0))]
        + _mod_specs(1, ada)
        + [_resident(a.shape) for a in (g_pre, g_post, wa, wz, wb, walpha, balpha, ggla, wconv,
                                        wbo, wmo)]
        + cast_in,
        out_specs=[
            pl.BlockSpec((1, tm, D_MODEL), lambda n, l: (n, l, 0)),
            pl.BlockSpec((1, 1, N_HEADS, DK, DV), lambda n, l: (0, n, 0, 0, 0)),
            pl.BlockSpec((1, 1, CONV_WIDTH - 1, D_CONV), lambda n, l: (0, n, 0, 0)),
        ] + cast_out,
        out_shape=[
            jax.ShapeDtypeStruct(x.shape, F32),
            jax.ShapeDtypeStruct((1, n, N_HEADS, DK, DV), F32),
            jax.ShapeDtypeStruct((1, n, CONV_WIDTH - 1, D_CONV), F32),
        ] + cast_shapes,
        scratch_shapes=[
            pltpu.VMEM((N_HEADS, DK, DV), F32),
            pltpu.VMEM((MIX_PART + SUBLANES, D_CONV), F32),
            pltpu.VMEM((tm, V_WIDTH), BF16),
        ],
        compiler_params=_params(("arbitrary", "arbitrary")),
        name="mixer",
    )(x, ada, ada, ada, g_pre, g_post, wa, wz, wb, walpha, balpha, ggla, wconv, wbo, wmo,
      *[j.src for j in jobs])


def _smix_pre_kernel(x_ref, sh_ref, sc_ref, gpre_ref, wa_ref, wz_ref, wb_ref, walpha_ref,
                     balpha_ref, wconv_ref, sconv_ref,
                     qt_ref, kt_ref, at_ref, v_ref, sr_ref, yconv_ref, sga_ref, sgb_ref,
                     sconv_new_ref):
    x = x_ref[...]
    hb = (_rms(x, gpre_ref[1:2, :]) * (1.0 + sc_ref[...]) + sh_ref[...]).astype(BF16)
    pa = _dot(hb, wa_ref[:, :_A_WIDTH])
    q = pa[:, 0:QK_WIDTH] * (DK ** -0.5)
    k = pa[:, QK_WIDTH:2 * QK_WIDTH]
    v_ref[...] = pa[:, 2 * QK_WIDTH:2 * QK_WIDTH + V_WIDTH]
    sr_ref[...] = _silu(pa[:, 2 * QK_WIDTH + V_WIDTH:])
    a = jnp.exp(_log_decay(hb, wz_ref, walpha_ref, balpha_ref))
    for hd in range(N_HEADS):
        kc = slice(hd * DK, (hd + 1) * DK)
        qt_ref[hd] = q[:, kc].T
        kt_ref[hd] = k[:, kc].T
        at_ref[hd] = a[:, kc].T

    pb = _dot(hb, wb_ref[:, :_B_WIDTH])
    u = pb[:, D_CONV:2 * D_CONV] * pb[:, 2 * D_CONV:3 * D_CONV]
    conv = (wconv_ref[0:1, :] * sconv_ref[:, 0, :] + wconv_ref[1:2, :] * sconv_ref[:, 1, :]
            + wconv_ref[2:3, :] * u)
    yconv_ref[...] = pb[:, 0:D_CONV] * conv
    sga_ref[...] = jax.nn.sigmoid(pb[:, 3 * D_CONV:3 * D_CONV + D_MODEL])
    sgb_ref[...] = jax.nn.sigmoid(pb[:, 3 * D_CONV + D_MODEL:])
    sconv_new_ref[:, 0, :] = sconv_ref[:, 1, :]
    sconv_new_ref[:, 1, :] = u


def _smix_state_kernel(s_ref, qt_ref, kt_ref, at_ref, v_ref, snew_ref, o_ref):
    base = pl.program_id(0) * SAMPLE_BLOCK
    lane = lax.broadcasted_iota(jnp.int32, (1, LANES), 1)
    for i in range(SAMPLE_BLOCK):
        pick = jnp.where(lane == base + i, 1.0, 0.0)
        for hd in range(N_HEADS):
            vc = slice(hd * DV, (hd + 1) * DV)
            a_col = jnp.sum(at_ref[hd] * pick, axis=1, keepdims=True)
            k_col = jnp.sum(kt_ref[hd] * pick, axis=1, keepdims=True)
            q_col = jnp.sum(qt_ref[hd] * pick, axis=1, keepdims=True)
            s_new = a_col * s_ref[0, i, hd] + k_col * v_ref[i:i + 1, vc]
            snew_ref[0, i, hd] = s_new
            o_ref[i:i + 1, vc] = jnp.sum(q_col * s_new, axis=0, keepdims=True)


def _smix_post_kernel(x_ref, gt_ref, gpost_ref, o_ref, sr_ref, yconv_ref, sga_ref, sgb_ref,
                      ggla_ref, wbo_ref, wmo_ref, y_ref, ygla_scr):
    for hd in range(N_HEADS):
        vc = slice(hd * DV, (hd + 1) * DV)
        on = _rms(o_ref[:, vc], ggla_ref[:, vc])
        ygla_scr[:, vc] = (on * sr_ref[:, vc]).astype(BF16)
    mix = _merge_and_project(ygla_scr[...], yconv_ref[...].astype(BF16), sga_ref[...],
                             sgb_ref[...], wbo_ref, wmo_ref)
    y_ref[...] = x_ref[...] + gt_ref[...] * _rms(mix, gpost_ref[1:2, :])


def _sample_mixer(x, ada, state_gla, state_conv, g_pre, g_post, wts):
    s = x.shape[0]
    wa, wz, wb, walpha, balpha, ggla, wconv, wbo, wmo = wts
    whole = lambda a: pl.BlockSpec(a.shape, lambda *_: (0,) * a.ndim)
    mod = lambda j: pl.BlockSpec((s, D_MODEL), lambda *_: (0, 3 + j))
    tok = jax.ShapeDtypeStruct((s, D_MODEL), F32)
    tr = jax.ShapeDtypeStruct((N_HEADS, DK, s), F32)

    pre_in = (x, ada, ada, g_pre, wa, wz, wb, walpha, balpha, wconv, state_conv)
    pre_specs = [whole(x), mod(0), mod(1)] + [whole(a) for a in pre_in[3:]]
    pre_out = [tr, tr, tr, tok, tok, tok, tok, tok, jax.ShapeDtypeStruct(state_conv.shape, F32)]
    qt, kt, at, v, sr, yconv, sga, sgb, sconv_new = pl.pallas_call(
        _smix_pre_kernel,
        grid=(1,),
        in_specs=pre_specs,
        out_specs=[whole(o) for o in pre_out],
        out_shape=pre_out,
        compiler_params=_params(("arbitrary",)),
        name="smix_pre",
    )(*pre_in)

    state_block = pl.BlockSpec((1, SAMPLE_BLOCK, N_HEADS, DK, DV), lambda j: (0, j, 0, 0, 0))
    snew, o = pl.pallas_call(
        _smix_state_kernel,
        grid=(s // SAMPLE_BLOCK,),
        in_specs=[state_block, whole(qt), whole(kt), whole(at),
                  pl.BlockSpec((SAMPLE_BLOCK, V_WIDTH), lambda j: (j, 0))],
        out_specs=[state_block, pl.BlockSpec((SAMPLE_BLOCK, V_WIDTH), lambda j: (j, 0))],
        out_shape=[jax.ShapeDtypeStruct(state_gla.shape, F32), tok],
        compiler_params=_params(("arbitrary",)),
        name="smix_state",
    )(state_gla, qt, kt, at, v)

    post_in = (x, ada, g_post, o, sr, yconv, sga, sgb, ggla, wbo, wmo)
    post_specs = [whole(x), mod(2)] + [whole(a) for a in post_in[2:]]
    y = pl.pallas_call(
        _smix_post_kernel,
        grid=(1,),
        in_specs=post_specs,
        out_specs=whole(tok),
        out_shape=tok,
        scratch_shapes=[pltpu.VMEM((s, V_WIDTH), BF16)],
        compiler_params=_params(("arbitrary",)),
        name="smix_post",
    )(*post_in)
    return y, snew, sconv_new


def kernel(x_prompt, x_sample, state_gla, state_conv, c_prompt, c_sample, w_ada, b_ada, g_pre,
           g_post, w_ffn1_in, w_ffn1_out, w_ffn2_in, w_ffn2_out, w_mix_in, w_alpha, b_alpha,
           g_gla_norm, w_conv, w_branch_out, w_mix_out):
    depth = w_ada.shape[0]
    n_s = x_sample.shape[0]
    assert depth == 1 and x_sample.shape[1] == 1 and n_s == LANES

    yp = x_prompt
    ys = x_sample.reshape(1, n_s, D_MODEL)
    outs = []
    for i in range(depth):
        ada_s, ada_p = _ada(c_sample, c_prompt, w_ada[i], b_ada[i])

        ffn1_w = (w_ffn1_in[i].astype(BF16),
                  jnp.pad(w_ffn1_out[i], ((0, 0), (0, LANES))).astype(BF16))
        n_ffn_steps = yp.shape[0] * (yp.shape[1] // FFN_TM)
        n_mix_steps = yp.shape[0] * (yp.shape[1] // MIX_TM)
        w_mix_t = jnp.swapaxes(w_mix_in[i], 0, 1)
        mixer_jobs = (
            _CastJob(w_mix_t, MIX_QKVR_CAST_ROWS, _A_WIDTH // MIX_QKVR_CAST_ROWS, transpose=True,
                     pad=1),
            _CastJob(w_mix_t, LANES, 1, transpose=True, row0=_A_WIDTH, valid=GATE_RANK),
            _CastJob(w_mix_t, MIX_CONV_CAST_ROWS, _B_WIDTH // MIX_CONV_CAST_ROWS, transpose=True,
                     row0=_A_WIDTH + GATE_RANK, pad=1),
            _whole_cast(w_branch_out[i].reshape(2 * D_MODEL, D_MODEL), n_ffn_steps, pad=LANES),
            _whole_cast(w_mix_out[i], n_ffn_steps, pad=LANES))
        assert all(j.n_steps <= n_ffn_steps for j in mixer_jobs)
        assert all((j.n_steps * j.block_rows if j.transpose else j.src.shape[1] + j.pad)
                   % (WEIGHT_PITCH_TILES * LANES) for j in mixer_jobs)
        yp, ys, wa, wz, wb, wbo, wmo = _ffn(yp, ys, ada_p, ada_s, 0, g_pre[i], g_post[i], *ffn1_w,
                                            jobs=mixer_jobs)
        walpha = jnp.pad(w_alpha[i], ((0, LANES - GATE_RANK), (0, 0))).astype(BF16)
        mix_w = (wa, wz, wb, walpha, b_alpha[i].reshape(1, QK_WIDTH),
                 g_gla_norm[i].reshape(1, V_WIDTH),
                 jnp.pad(w_conv[i], ((0, SUBLANES - CONV_WIDTH), (0, 0))),
                 wbo.reshape(2, D_MODEL, D_MODEL + LANES), wmo)
        ys, gla_s, sconv_new = _sample_mixer(ys[0], ada_s, state_gla[i:i + 1], state_conv[i],
                                             g_pre[i], g_post[i], mix_w)
        yp, gla_p, conv_p, *ffn2_w = _mixer(
            yp, ada_p, g_pre[i], g_post[i], mix_w,
            jobs=(_whole_cast(w_ffn2_in[i], n_mix_steps),
                  _whole_cast(w_ffn2_out[i], n_mix_steps, pad=LANES)))
        yp, ys = _ffn(yp, ys.reshape(1, n_s, D_MODEL), ada_p, ada_s, 2, g_pre[i], g_post[i],
                      *ffn2_w)
        outs.append((gla_p, conv_p, gla_s, sconv_new[None]))

    gla_p, conv_p, gla_s, conv_s = outs[0]
    return (yp, ys.reshape(n_s, 1, D_MODEL), gla_p, conv_p, gla_s, conv_s)
```

```python
import functools
from typing import NamedTuple

import jax
import jax.numpy as jnp
from jax import lax
from jax.experimental import pallas as pl
from jax.experimental.pallas import tpu as pltpu

F32 = jnp.float32
BF16 = jnp.bfloat16

D_MODEL = 1024
D_FF = 2816
N_HEADS = 4
DK = 128
DV = 256
QK_WIDTH = N_HEADS * DK
V_WIDTH = N_HEADS * DV
GATE_RANK = 16
GATE_TAU = 16.0
D_CONV = D_MODEL
CONV_WIDTH = 3
RMS_EPS = 1e-6
MACARON_WEIGHT = 0.5

LANES = 128
SUBLANES = 8
GLA_CHUNK = 64
VMEM_LIMIT_BYTES = 56 * 1024 * 1024

FFN_TM = 1024
FFN_PART = 256
MIX_TM = 512
MIX_PART = 256
ADA_TN = 1536
SAMPLE_BLOCK = 16

_A_WIDTH = 2 * QK_WIDTH + 2 * V_WIDTH
_B_WIDTH = 3 * D_CONV + 2 * D_MODEL


def _rms(x, g):
    return x * lax.rsqrt(jnp.mean(x * x, axis=-1, keepdims=True) + RMS_EPS) * g


def _silu(x):
    return x * jax.nn.sigmoid(x)


def _dot(a, b):
    return jnp.dot(a, b, preferred_element_type=F32)


def _dot_nt(a, b):
    return lax.dot_general(a, b, (((1,), (1,)), ((), ())), preferred_element_type=F32)


def _dot_tn(a, b):
    return lax.dot_general(a, b, (((0,), (0,)), ((), ())), preferred_element_type=F32)


def _resident(shape):
    zeros = (0,) * len(shape)
    return pl.BlockSpec(shape, lambda *_: zeros, pipeline_mode=pl.Buffered(1))


def _params(semantics):
    return pltpu.CompilerParams(dimension_semantics=semantics, vmem_limit_bytes=VMEM_LIMIT_BYTES)


def _ada_kernel(cs_ref, cp_ref, w_ref, b_ref, os_ref, op_ref):
    w = w_ref[...].astype(BF16)
    for c_ref, o_ref in ((cs_ref, os_ref), (cp_ref, op_ref)):
        o_ref[...] = _dot(_silu(c_ref[...]).astype(BF16), w) + b_ref[...]


def _ada(c_sample, c_prompt, w_ada, b_ada):
    width = w_ada.shape[1]
    rows = lambda c: pl.BlockSpec((c.shape[0], D_MODEL), lambda j: (0, 0))
    cols = lambda c: pl.BlockSpec((c.shape[0], ADA_TN), lambda j: (0, j))
    return pl.pallas_call(
        _ada_kernel,
        grid=(width // ADA_TN,),
        in_specs=[rows(c_sample), rows(c_prompt),
                  pl.BlockSpec((D_MODEL, ADA_TN), lambda j: (0, j)),
                  pl.BlockSpec((1, ADA_TN), lambda j: (0, j))],
        out_specs=[cols(c_sample), cols(c_prompt)],
        out_shape=[jax.ShapeDtypeStruct((c.shape[0], width), F32) for c in (c_sample, c_prompt)],
        compiler_params=_params(("arbitrary",)),
        name="ada",
    )(c_sample, c_prompt, w_ada, b_ada.reshape(1, width))


def _mod_specs(sub, ada):
    return [pl.BlockSpec((ada.shape[0], D_MODEL), lambda *_, col=sub * 3 + j: (0, col))
            for j in range(3)]


def _load_mod(ref, row):
    if row is None:
        return ref[...]
    return ref[pl.ds(row, 1), :]


BF16_SUBLANES = 16
WEIGHT_PITCH_TILES = 8
MIX_QKVR_CAST_ROWS = 2 * LANES
MIX_CONV_CAST_ROWS = 4 * LANES


class _CastJob(NamedTuple):
    src: jax.Array
    block_rows: int
    n_blocks: int
    transpose: bool = False
    row0: int = 0
    valid: int = 0
    pad: int = 0

    @property
    def n_steps(self):
        return self.n_blocks + (self.pad if self.transpose else 0)

    def specs(self, step_of):
        cols = self.src.shape[1]
        r, nb = self.block_rows, self.n_blocks
        block = lambda *g: jnp.minimum(step_of(*g), nb - 1)
        if not self.transpose:
            return (pl.BlockSpec((r, cols), lambda *g: (block(*g), 0)),
                    pl.BlockSpec((r, cols + self.pad), lambda *g: (block(*g), 0)),
                    jax.ShapeDtypeStruct((nb * r, cols + self.pad), BF16))
        assert self.row0 % SUBLANES == 0 and r % SUBLANES == 0
        out_block = lambda *g: jnp.minimum(step_of(*g), self.n_steps - 1)
        return (pl.BlockSpec((pl.Element(r), pl.Element(cols)),
                             lambda *g: (pl.multiple_of(self.row0 + r * block(*g), SUBLANES), 0)),
                pl.BlockSpec((cols, r), lambda *g: (0, out_block(*g))),
                jax.ShapeDtypeStruct((cols, self.n_steps * r), BF16))

    @property
    def static(self):
        return (self.transpose, self.valid or self.block_rows, self.n_blocks, self.pad)


def _whole_cast(w, n_steps, pad=0):
    rows = w.shape[0]
    r = next(r for r in range(BF16_SUBLANES, rows + 1, BF16_SUBLANES)
             if rows % r == 0 and rows // r <= n_steps)
    return _CastJob(w, r, rows // r, pad=pad)


def _cast_plumbing(jobs, step_of):
    specs = [job.specs(step_of) for job in jobs]
    return ([i for i, _, _ in specs], [o for _, o, _ in specs], [s for _, _, s in specs])


def _run_casts(statics, step, src_refs, dst_refs):
    for (transpose, valid, n_blocks, pad), src, dst in zip(statics, src_refs, dst_refs,
                                                           strict=True):
        blk = src[...]
        if not transpose:
            cols = blk.shape[1]
            dst[:, :cols] = blk.astype(BF16)
            if pad:
                dst[:, cols:] = jnp.zeros((blk.shape[0], pad), BF16)
            continue
        blk = blk.T
        if valid < blk.shape[1]:
            lane = lax.broadcasted_iota(jnp.int32, blk.shape, 1)
            blk = jnp.where(lane < valid, blk, 0.0)
        if pad:
            blk = jnp.where(step < n_blocks, blk, 0.0)
        dst[...] = blk.astype(BF16)


def _ffn_rows(x_ref, o_ref, mods, per_token, gpre_ref, gpost_ref, win_ref, wout_ref):
    shift, scale, gate = mods
    tm = x_ref.shape[1]
    parts = max(tm // FFN_PART, 1)
    rows = [slice(i * (tm // parts), (i + 1) * (tm // parts)) for i in range(parts)]
    mod = lambda m, rs: m[rs] if per_token else m

    def pre(rs):
        return (_rms(x_ref[0, rs, :], gpre_ref[...]) * (1.0 + mod(scale, rs))
                + mod(shift, rs)).astype(BF16)

    def post(rs, out):
        o_ref[0, rs, :] = (x_ref[0, rs, :]
                           + MACARON_WEIGHT * mod(gate, rs) * _rms(out, gpost_ref[...]))

    h = pre(rows[0])
    for i in range(parts):
        g = _dot(h, win_ref[:, :D_FF])
        h_next = pre(rows[i + 1]) if i + 1 < parts else None
        u = _dot(h, win_ref[:, D_FF:])
        a = (_silu(g) * u).astype(BF16)
        if i > 0:
            post(rows[i - 1], out)
        out = _dot(a, wout_ref[:, :D_MODEL])
        h = h_next
    post(rows[parts - 1], out)


def _ffn_kernel(x_ref, sh_ref, sc_ref, gt_ref, xs_ref, shs_ref, scs_ref, gts_ref, gpre_ref,
                gpost_ref, win_ref, wout_ref, *rest, sub, n_l, job_groups):
    n_jobs = len(job_groups)
    o_ref, os_ref = rest[n_jobs:n_jobs + 2]
    step = pl.program_id(0)
    _run_casts(job_groups, step, rest[:n_jobs], rest[n_jobs + 2:])
    last = pl.num_programs(0) - 1
    weights = (gpre_ref.at[sub:sub + 1], gpost_ref.at[sub:sub + 1], win_ref, wout_ref)

    @pl.when(step < last)
    def _():
        mods = [_load_mod(r, step // n_l) for r in (sh_ref, sc_ref, gt_ref)]
        _ffn_rows(x_ref, o_ref, mods, False, *weights)

    @pl.when(step == last)
    def _():
        mods = [_load_mod(r, None) for r in (shs_ref, scs_ref, gts_ref)]
        _ffn_rows(xs_ref, os_ref, mods, True, *weights)


def _ffn(x, xs, ada_p, ada_s, sub, g_pre, g_post, w_in, w_out, *, jobs=()):
    n, l, _ = x.shape
    n_l = l // FFN_TM
    n_tiles = n * n_l
    tile = lambda i: (jnp.minimum(i, n_tiles - 1) // n_l, jnp.minimum(i, n_tiles - 1) % n_l, 0)
    whole = lambda i: (0, 0, 0)
    kern = functools.partial(_ffn_kernel, sub=sub, n_l=n_l,
                             job_groups=tuple(j.static for j in jobs))
    cast_in, cast_out, cast_shapes = _cast_plumbing(jobs, lambda i: i)
    return pl.pallas_call(
        kern,
        grid=(n_tiles + 1,),
        in_specs=[pl.BlockSpec((1, FFN_TM, D_MODEL), tile)]
        + _mod_specs(sub, ada_p)
        + [pl.BlockSpec(xs.shape, whole)]
        + _mod_specs(sub, ada_s)
        + [_resident(g_pre.shape), _resident(g_post.shape),
           _resident(w_in.shape), _resident(w_out.shape)]
        + cast_in,
        out_specs=[pl.BlockSpec((1, FFN_TM, D_MODEL), tile), pl.BlockSpec(xs.shape, whole)]
        + cast_out,
        out_shape=[jax.ShapeDtypeStruct(x.shape, F32), jax.ShapeDtypeStruct(xs.shape, F32)]
        + cast_shapes,
        compiler_params=_params(("arbitrary",)),
        name="ffn",
    )(x, ada_p, ada_p, ada_p, xs, ada_s, ada_s, ada_s, g_pre, g_post, w_in, w_out,
      *[j.src for j in jobs])


def _log_decay(hb, wz_ref, walpha_ref, balpha_ref):
    z = _dot(hb, wz_ref[...]).astype(BF16)
    xg = _dot(z, walpha_ref[...]) + balpha_ref[...]
    return jax.nn.log_sigmoid(xg) * (1.0 / GATE_TAU)


def _merge_and_project(y_gla, y_conv, sig_a, sig_b, wbo_ref, wmo_ref):
    pg = _dot(y_gla, wbo_ref[0, :, :D_MODEL])
    pc = _dot(y_conv, wbo_ref[1, :, :D_MODEL])
    merged = (sig_a * pg + sig_b * pc).astype(BF16)
    return _dot(merged, wmo_ref[:, :D_MODEL])


_PAIR_LEVELS = tuple(GLA_CHUNK >> i for i in range(1, GLA_CHUNK.bit_length()))


def _pair_operands(qs, k, b, log_a, m, tm):
    if m >= SUBLANES:
        halves = lambda a: a.reshape(tm // (2 * m), 2, m, QK_WIDTH)
        b4, k4, q4 = halves(b), halves(k), halves(qs)
        mid = b4[:, 0:1, m - 1:m, :]
        w = jnp.exp(jnp.concatenate([mid - b4[:, 0:1], b4[:, 1:2] - mid], axis=1))
        src = jnp.concatenate([k4[:, 0:1], q4[:, 1:2]], axis=1)
        return (src * w).astype(BF16).reshape(tm, QK_WIDTH)
    tiles = lambda a: a.reshape(tm // SUBLANES, SUBLANES, QK_WIDTH)
    sub = lax.broadcasted_iota(jnp.int32, (1, SUBLANES, QK_WIDTH), 1)
    second = (sub & m) != 0
    if m == 1:
        w = jnp.exp(jnp.where(second, tiles(log_a), 0.0))
    else:
        b4 = tiles(b)
        mid = b4[:, m - 1:m, :]
        for start in range(2 * m, SUBLANES, 2 * m):
            mid = jnp.where(sub >= start, b4[:, start + m - 1:start + m, :], mid)
        w = jnp.exp((b4 - mid) * jnp.where(second, 1.0, -1.0))
    return (jnp.where(second, tiles(qs), tiles(k)) * w).astype(BF16).reshape(tm, QK_WIDTH)


def _pair_mask(m):
    t = lax.broadcasted_iota(jnp.int32, (GLA_CHUNK, GLA_CHUNK), 0)
    s = lax.broadcasted_iota(jnp.int32, (GLA_CHUNK, GLA_CHUNK), 1)
    return ((t & m) != 0) & ((s & m) == 0) & ((t // (2 * m)) == (s // (2 * m)))


def _mixer_kernel(x_ref, sh_ref, sc_ref, gt_ref, gpre_ref, gpost_ref, wa_ref, wz_ref, wb_ref,
                  walpha_ref, balpha_ref, ggla_ref, wconv_ref, wbo_ref, wmo_ref, *rest,
                  job_groups):
    n_jobs = len(job_groups)
    y_ref, sgla_ref, sconv_ref = rest[n_jobs:n_jobs + 3]
    st_scr, u_scr, ygla_scr = rest[-3:]
    l = pl.program_id(1)
    n_l = pl.num_programs(1)
    _run_casts(job_groups, pl.program_id(0) * n_l + l, rest[:n_jobs], rest[n_jobs + 3:-3])

    @pl.when(l == 0)
    def _():
        st_scr[...] = jnp.zeros_like(st_scr)
        u_scr[0:SUBLANES, :] = jnp.zeros((SUBLANES, D_CONV), F32)

    mods = [_load_mod(r, pl.program_id(0)) for r in (sh_ref, sc_ref, gt_ref)]
    weights = (gpre_ref.at[1:2], gpost_ref.at[1:2], wa_ref, wz_ref, wb_ref, walpha_ref,
               balpha_ref, ggla_ref, wconv_ref, wbo_ref, wmo_ref)
    n_parts = x_ref.shape[1] // MIX_PART
    parts = [_mixer_part(x_ref, y_ref, mods, weights, (st_scr, u_scr, ygla_scr),
                         slice(i * MIX_PART, (i + 1) * MIX_PART)) for i in range(n_parts)]
    next(parts[0])
    for i, p in enumerate(parts):
        next(p)
        if i + 1 < n_parts:
            next(parts[i + 1])
        if i > 0:
            next(parts[i - 1], None)
        next(p)
    next(parts[-1], None)

    @pl.when(l == n_l - 1)
    def _():
        sgla_ref[0, 0] = st_scr[...]
        sconv_ref[0, 0] = u_scr[SUBLANES - 2:SUBLANES, :]


def _mixer_part(x_ref, y_ref, mods, weights, scratch, prow):
    shift, scale, gate = mods
    (gpre_ref, gpost_ref, wa_ref, wz_ref, wb_ref, walpha_ref, balpha_ref, ggla_ref, wconv_ref,
     wbo_ref, wmo_ref) = weights
    st_scr, u_scr, ygla_scr = scratch
    ygla_scr = ygla_scr.at[prow]
    tm = MIX_PART
    x = x_ref[0, prow, :]
    hb = (_rms(x, gpre_ref[...]) * (1.0 + scale) + shift).astype(BF16)
    yield

    def proj_b(j):
        return _dot(hb, wb_ref[:, j * D_MODEL:(j + 1) * D_MODEL])

    nc = tm // GLA_CHUNK
    log_a = _log_decay(hb, wz_ref, walpha_ref, balpha_ref)
    pa = _dot(hb, wa_ref[:, :_A_WIDTH])
    qs = pa[:, 0:QK_WIDTH] * (DK ** -0.5)
    k = pa[:, QK_WIDTH:2 * QK_WIDTH]
    vb = pa[:, 2 * QK_WIDTH:2 * QK_WIDTH + V_WIDTH].astype(BF16)
    r = pa[:, 2 * QK_WIDTH + V_WIDTH:]

    row = lax.broadcasted_iota(jnp.int32, (tm, tm), 0)
    col = lax.broadcasted_iota(jnp.int32, (tm, tm), 1)
    tri = jnp.where((row >= col) & (row // GLA_CHUNK == col // GLA_CHUNK), 1.0, 0.0).astype(BF16)
    la_hi = log_a.astype(BF16)
    rem = log_a - la_hi.astype(F32)
    la_mid = rem.astype(BF16)
    la_lo = (rem - la_mid.astype(F32)).astype(BF16)
    b3 = (_dot(tri, la_hi) + _dot(tri, la_mid) + _dot(tri, la_lo)).reshape(
        nc, GLA_CHUNK, QK_WIDTH)

    u = proj_b(1) * proj_b(2)
    u_scr[SUBLANES:SUBLANES + tm, :] = u
    yield

    b_last = b3[:, GLA_CHUNK - 1:GLA_CHUNK, :]
    b = b3.reshape(tm, QK_WIDTH)
    qd = (qs * jnp.exp(b)).astype(BF16).reshape(nc, GLA_CHUNK, QK_WIDTH)
    kl = (k.reshape(nc, GLA_CHUNK, QK_WIDTH) * jnp.exp(b_last - b3)).astype(BF16)
    b_last_rows = jnp.concatenate(
        [b_last[c, :, hd * DK:(hd + 1) * DK] for c in range(nc) for hd in range(N_HEADS)]
        + [jnp.zeros((LANES - nc * N_HEADS, DK), F32)], axis=0)
    decay_cols = jnp.exp(b_last_rows.T)

    pair_ops = [_pair_operands(qs, k, b, log_a, m, tm) for m in _PAIR_LEVELS]
    pair_masks = [_pair_mask(m) for m in _PAIR_LEVELS]
    qk = qs * k
    crow = lax.broadcasted_iota(jnp.int32, (GLA_CHUNK, GLA_CHUNK), 0)
    ccol = lax.broadcasted_iota(jnp.int32, (GLA_CHUNK, GLA_CHUNK), 1)
    scores, upd = {}, {}
    for c in range(nc):
        rows = slice(c * GLA_CHUNK, (c + 1) * GLA_CHUNK)
        for hd in range(N_HEADS):
            kc = slice(hd * DK, (hd + 1) * DK)
            s = jnp.where(crow == ccol, jnp.sum(qk[rows, kc], axis=1, keepdims=True), 0.0)
            for p, mask in zip(pair_ops, pair_masks):
                s = jnp.where(mask, _dot_nt(p[rows, kc], p[rows, kc]), s)
            scores[c, hd] = s.astype(BF16)
            upd[c, hd] = _dot_tn(kl[c, :, kc], vb[rows, hd * DV:(hd + 1) * DV])

    conv = (wconv_ref[0:1, :] * u_scr[SUBLANES - 2:SUBLANES - 2 + tm, :]
            + wconv_ref[1:2, :] * u_scr[SUBLANES - 1:SUBLANES - 1 + tm, :]
            + wconv_ref[2:3, :] * u)
    u_scr[0:SUBLANES, :] = u_scr[tm:tm + SUBLANES, :]
    y_conv = (proj_b(0) * conv).astype(BF16)

    outs = {}
    for hd in range(N_HEADS):
        kc = slice(hd * DK, (hd + 1) * DK)
        vc = slice(hd * DV, (hd + 1) * DV)
        st = st_scr[hd]
        for c in range(nc):
            rows = slice(c * GLA_CHUNK, (c + 1) * GLA_CHUNK)
            lhs = jnp.concatenate([qd[c, :, kc], scores[c, hd]], axis=1)
            rhs = jnp.concatenate([st.astype(BF16), vb[rows, vc]], axis=0)
            outs[c, hd] = _dot(lhs, rhs)
            j = c * N_HEADS + hd
            st = st * decay_cols[:, j:j + 1] + upd[c, hd]
        st_scr[hd] = st

    sig_a = jax.nn.sigmoid(proj_b(3))
    sig_b = jax.nn.sigmoid(proj_b(4))

    for hd in range(N_HEADS):
        vc = slice(hd * DV, (hd + 1) * DV)
        for c in range(nc):
            rows = slice(c * GLA_CHUNK, (c + 1) * GLA_CHUNK)
            on = _rms(outs[c, hd], ggla_ref[:, vc])
            ygla_scr[rows, vc] = (on * _silu(r[rows, vc])).astype(BF16)

    yield
    mix = _merge_and_project(ygla_scr[...], y_conv, sig_a, sig_b, wbo_ref, wmo_ref)
    y_ref[0, prow, :] = x + gate * _rms(mix, gpost_ref[...])


def _mixer(x, ada, g_pre, g_post, wts, *, jobs=()):
    n, l, _ = x.shape
    tm = MIX_TM
    n_l = l // tm
    kern = functools.partial(_mixer_kernel, job_groups=tuple(j.static for j in jobs))
    cast_in, cast_out, cast_shapes = _cast_plumbing(jobs, lambda n, l: n * n_l + l)
    wa, wz, wb, walpha, balpha, ggla, wconv, wbo, wmo = wts
    return pl.pallas_call(
        kern,
        grid=(n, n_l),
        in_specs=[pl.BlockSpec((1, tm, D_MODEL), lambda n, l: (n, l, 0))]
        + _mod_specs(1, ada)
        + [_resident(a.shape) for a in (g_pre, g_post, wa, wz, wb, walpha, balpha, ggla, wconv,
                                        wbo, wmo)]
        + cast_in,
        out_specs=[
            pl.BlockSpec((1, tm, D_MODEL), lambda n, l: (n, l, 0)),
            pl.BlockSpec((1, 1, N_HEADS, DK, DV), lambda n, l: (0, n, 0, 0, 0)),
            pl.BlockSpec((1, 1, CONV_WIDTH - 1, D_CONV), lambda n, l: (0, n, 0, 0)),
        ] + cast_out,
        out_shape=[
            jax.ShapeDtypeStruct(x.shape, F32),
            jax.ShapeDtypeStruct((1, n, N_HEADS, DK, DV), F32),
            jax.ShapeDtypeStruct((1, n, CONV_WIDTH - 1, D_CONV), F32),
        ] + cast_shapes,
        scratch_shapes=[
            pltpu.VMEM((N_HEADS, DK, DV), F32),
            pltpu.VMEM((MIX_PART + SUBLANES, D_CONV), F32),
            pltpu.VMEM((tm, V_WIDTH), BF16),
        ],
        compiler_params=_params(("arbitrary", "arbitrary")),
        name="mixer",
    )(x, ada, ada, ada, g_pre, g_post, wa, wz, wb, walpha, balpha, ggla, wconv, wbo, wmo,
      *[j.src for j in jobs])


def _smix_pre_kernel(x_ref, sh_ref, sc_ref, gpre_ref, wa_ref, wz_ref, wb_ref, walpha_ref,
                     balpha_ref, wconv_ref, sconv_ref,
                     qt_ref, kt_ref, at_ref, v_ref, sr_ref, yconv_ref, sga_ref, sgb_ref,
                     sconv_new_ref):
    x = x_ref[...]
    hb = (_rms(x, gpre_ref[1:2, :]) * (1.0 + sc_ref[...]) + sh_ref[...]).astype(BF16)
    pa = _dot(hb, wa_ref[:, :_A_WIDTH])
    q = pa[:, 0:QK_WIDTH] * (DK ** -0.5)
    k = pa[:, QK_WIDTH:2 * QK_WIDTH]
    v_ref[...] = pa[:, 2 * QK_WIDTH:2 * QK_WIDTH + V_WIDTH]
    sr_ref[...] = _silu(pa[:, 2 * QK_WIDTH + V_WIDTH:])
    a = jnp.exp(_log_decay(hb, wz_ref, walpha_ref, balpha_ref))
    for hd in range(N_HEADS):
        kc = slice(hd * DK, (hd + 1) * DK)
        qt_ref[hd] = q[:, kc].T
        kt_ref[hd] = k[:, kc].T
        at_ref[hd] = a[:, kc].T

    pb = _dot(hb, wb_ref[:, :_B_WIDTH])
    u = pb[:, D_CONV:2 * D_CONV] * pb[:, 2 * D_CONV:3 * D_CONV]
    conv = (wconv_ref[0:1, :] * sconv_ref[:, 0, :] + wconv_ref[1:2, :] * sconv_ref[:, 1, :]
            + wconv_ref[2:3, :] * u)
    yconv_ref[...] = pb[:, 0:D_CONV] * conv
    sga_ref[...] = jax.nn.sigmoid(pb[:, 3 * D_CONV:3 * D_CONV + D_MODEL])
    sgb_ref[...] = jax.nn.sigmoid(pb[:, 3 * D_CONV + D_MODEL:])
    sconv_new_ref[:, 0, :] = sconv_ref[:, 1, :]
    sconv_new_ref[:, 1, :] = u


def _smix_state_kernel(s_ref, qt_ref, kt_ref, at_ref, v_ref, snew_ref, o_ref):
    base = pl.program_id(0) * SAMPLE_BLOCK
    lane = lax.broadcasted_iota(jnp.int32, (1, LANES), 1)
    for i in range(SAMPLE_BLOCK):
        pick = jnp.where(lane == base + i, 1.0, 0.0)
        for hd in range(N_HEADS):
            vc = slice(hd * DV, (hd + 1) * DV)
            a_col = jnp.sum(at_ref[hd] * pick, axis=1, keepdims=True)
            k_col = jnp.sum(kt_ref[hd] * pick, axis=1, keepdims=True)
            q_col = jnp.sum(qt_ref[hd] * pick, axis=1, keepdims=True)
            s_new = a_col * s_ref[0, i, hd] + k_col * v_ref[i:i + 1, vc]
            snew_ref[0, i, hd] = s_new
            o_ref[i:i + 1, vc] = jnp.sum(q_col * s_new, axis=0, keepdims=True)


def _smix_post_kernel(x_ref, gt_ref, gpost_ref, o_ref, sr_ref, yconv_ref, sga_ref, sgb_ref,
                      ggla_ref, wbo_ref, wmo_ref, y_ref, ygla_scr):
    for hd in range(N_HEADS):
        vc = slice(hd * DV, (hd + 1) * DV)
        on = _rms(o_ref[:, vc], ggla_ref[:, vc])
        ygla_scr[:, vc] = (on * sr_ref[:, vc]).astype(BF16)
    mix = _merge_and_project(ygla_scr[...], yconv_ref[...].astype(BF16), sga_ref[...],
                             sgb_ref[...], wbo_ref, wmo_ref)
    y_ref[...] = x_ref[...] + gt_ref[...] * _rms(mix, gpost_ref[1:2, :])


def _sample_mixer(x, ada, state_gla, state_conv, g_pre, g_post, wts):
    s = x.shape[0]
    wa, wz, wb, walpha, balpha, ggla, wconv, wbo, wmo = wts
    whole = lambda a: pl.BlockSpec(a.shape, lambda *_: (0,) * a.ndim)
    mod = lambda j: pl.BlockSpec((s, D_MODEL), lambda *_: (0, 3 + j))
    tok = jax.ShapeDtypeStruct((s, D_MODEL), F32)
    tr = jax.ShapeDtypeStruct((N_HEADS, DK, s), F32)

    pre_in = (x, ada, ada, g_pre, wa, wz, wb, walpha, balpha, wconv, state_conv)
    pre_specs = [whole(x), mod(0), mod(1)] + [whole(a) for a in pre_in[3:]]
    pre_out = [tr, tr, tr, tok, tok, tok, tok, tok, jax.ShapeDtypeStruct(state_conv.shape, F32)]
    qt, kt, at, v, sr, yconv, sga, sgb, sconv_new = pl.pallas_call(
        _smix_pre_kernel,
        grid=(1,),
        in_specs=pre_specs,
        out_specs=[whole(o) for o in pre_out],
        out_shape=pre_out,
        compiler_params=_params(("arbitrary",)),
        name="smix_pre",
    )(*pre_in)

    state_block = pl.BlockSpec((1, SAMPLE_BLOCK, N_HEADS, DK, DV), lambda j: (0, j, 0, 0, 0))
    snew, o = pl.pallas_call(
        _smix_state_kernel,
        grid=(s // SAMPLE_BLOCK,),
        in_specs=[state_block, whole(qt), whole(kt), whole(at),
                  pl.BlockSpec((SAMPLE_BLOCK, V_WIDTH), lambda j: (j, 0))],
        out_specs=[state_block, pl.BlockSpec((SAMPLE_BLOCK, V_WIDTH), lambda j: (j, 0))],
        out_shape=[jax.ShapeDtypeStruct(state_gla.shape, F32), tok],
        compiler_params=_params(("arbitrary",)),
        name="smix_state",
    )(state_gla, qt, kt, at, v)

    post_in = (x, ada, g_post, o, sr, yconv, sga, sgb, ggla, wbo, wmo)
    post_specs = [whole(x), mod(2)] + [whole(a) for a in post_in[2:]]
    y = pl.pallas_call(
        _smix_post_kernel,
        grid=(1,),
        in_specs=post_specs,
        out_specs=whole(tok),
        out_shape=tok,
        scratch_shapes=[pltpu.VMEM((s, V_WIDTH), BF16)],
        compiler_params=_params(("arbitrary",)),
        name="smix_post",
    )(*post_in)
    return y, snew, sconv_new


def kernel(x_prompt, x_sample, state_gla, state_conv, c_prompt, c_sample, w_ada, b_ada, g_pre,
           g_post, w_ffn1_in, w_ffn1_out, w_ffn2_in, w_ffn2_out, w_mix_in, w_alpha, b_alpha,
           g_gla_norm, w_conv, w_branch_out, w_mix_out):
    depth = w_ada.shape[0]
    n_s = x_sample.shape[0]
    assert depth == 1 and x_sample.shape[1] == 1 and n_s == LANES

    yp = x_prompt
    ys = x_sample.reshape(1, n_s, D_MODEL)
    outs = []
    for i in range(depth):
        ada_s, ada_p = _ada(c_sample, c_prompt, w_ada[i], b_ada[i])

        ffn1_w = (w_ffn1_in[i].astype(BF16),
                  jnp.pad(w_ffn1_out[i], ((0, 0), (0, LANES))).astype(BF16))
        n_ffn_steps = yp.shape[0] * (yp.shape[1] // FFN_TM)
        n_mix_steps = yp.shape[0] * (yp.shape[1] // MIX_TM)
        w_mix_t = jnp.swapaxes(w_mix_in[i], 0, 1)
        mixer_jobs = (
            _CastJob(w_mix_t, MIX_QKVR_CAST_ROWS, _A_WIDTH // MIX_QKVR_CAST_ROWS, transpose=True,
                     pad=1),
            _CastJob(w_mix_t, LANES, 1, transpose=True, row0=_A_WIDTH, valid=GATE_RANK),
            _CastJob(w_mix_t, MIX_CONV_CAST_ROWS, _B_WIDTH // MIX_CONV_CAST_ROWS, transpose=True,
                     row0=_A_WIDTH + GATE_RANK, pad=1),
            _whole_cast(w_branch_out[i].reshape(2 * D_MODEL, D_MODEL), n_ffn_steps, pad=LANES),
            _whole_cast(w_mix_out[i], n_ffn_steps, pad=LANES))
        assert all(j.n_steps <= n_ffn_steps for j in mixer_jobs)
        assert all((j.n_steps * j.block_rows if j.transpose else j.src.shape[1] + j.pad)
                   % (WEIGHT_PITCH_TILES * LANES) for j in mixer_jobs)
        yp, ys, wa, wz, wb, wbo, wmo = _ffn(yp, ys, ada_p, ada_s, 0, g_pre[i], g_post[i], *ffn1_w,
                                            jobs=mixer_jobs)
        walpha = jnp.pad(w_alpha[i], ((0, LANES - GATE_RANK), (0, 0))).astype(BF16)
        mix_w = (wa, wz, wb, walpha, b_alpha[i].reshape(1, QK_WIDTH),
                 g_gla_norm[i].reshape(1, V_WIDTH),
                 jnp.pad(w_conv[i], ((0, SUBLANES - CONV_WIDTH), (0, 0))),
                 wbo.reshape(2, D_MODEL, D_MODEL + LANES), wmo)
        ys, gla_s, sconv_new = _sample_mixer(ys[0], ada_s, state_gla[i:i + 1], state_conv[i],
                                             g_pre[i], g_post[i], mix_w)
        yp, gla_p, conv_p, *ffn2_w = _mixer(
            yp, ada_p, g_pre[i], g_post[i], mix_w,
            jobs=(_whole_cast(w_ffn2_in[i], n_mix_steps),
                  _whole_cast(w_ffn2_out[i], n_mix_steps, pad=LANES)))
        yp, ys = _ffn(yp, ys.reshape(1, n_s, D_MODEL), ada_p, ada_s, 2, g_pre[i], g_post[i],
                      *ffn2_w)
        outs.append((gla_p, conv_p, gla_s, sconv_new[None]))

    gla_p, conv_p, gla_s, conv_s = outs[0]
    return (yp, ys.reshape(n_s, 1, D_MODEL), gla_p, conv_p, gla_s, conv_s)
```

```python
import functools
from typing import NamedTuple

import jax
import jax.numpy as jnp
from jax import lax
from jax.experimental import pallas as pl
from jax.experimental.pallas import tpu as pltpu

F32 = jnp.float32
BF16 = jnp.bfloat16

D_MODEL = 1024
D_FF = 2816
N_HEADS = 4
DK = 128
DV = 256
QK_WIDTH = N_HEADS * DK
V_WIDTH = N_HEADS * DV
GATE_RANK = 16
GATE_TAU = 16.0
D_CONV = D_MODEL
CONV_WIDTH = 3
RMS_EPS = 1e-6
MACARON_WEIGHT = 0.5

LANES = 128
SUBLANES = 8
GLA_CHUNK = 64
VMEM_LIMIT_BYTES = 56 * 1024 * 1024

FFN_TM = 1024
FFN_PART = 256
MIX_TM = 512
MIX_PART = 256
ADA_TN = 1536
SAMPLE_BLOCK = 16

_A_WIDTH = 2 * QK_WIDTH + 2 * V_WIDTH
_B_WIDTH = 3 * D_CONV + 2 * D_MODEL


def _rms(x, g):
    return x * lax.rsqrt(jnp.mean(x * x, axis=-1, keepdims=True) + RMS_EPS) * g


def _silu(x):
    return x * jax.nn.sigmoid(x)


def _dot(a, b):
    return jnp.dot(a, b, preferred_element_type=F32)


def _dot_nt(a, b):
    return lax.dot_general(a, b, (((1,), (1,)), ((), ())), preferred_element_type=F32)


def _dot_tn(a, b):
    return lax.dot_general(a, b, (((0,), (0,)), ((), ())), preferred_element_type=F32)


def _resident(shape):
    zeros = (0,) * len(shape)
    return pl.BlockSpec(shape, lambda *_: zeros, pipeline_mode=pl.Buffered(1))


def _params(semantics):
    return pltpu.CompilerParams(dimension_semantics=semantics, vmem_limit_bytes=VMEM_LIMIT_BYTES)


def _ada_kernel(cs_ref, cp_ref, w_ref, b_ref, os_ref, op_ref):
    w = w_ref[...].astype(BF16)
    for c_ref, o_ref in ((cs_ref, os_ref), (cp_ref, op_ref)):
        o_ref[...] = _dot(_silu(c_ref[...]).astype(BF16), w) + b_ref[...]


def _ada(c_sample, c_prompt, w_ada, b_ada):
    width = w_ada.shape[1]
    rows = lambda c: pl.BlockSpec((c.shape[0], D_MODEL), lambda j: (0, 0))
    cols = lambda c: pl.BlockSpec((c.shape[0], ADA_TN), lambda j: (0, j))
    return pl.pallas_call(
        _ada_kernel,
        grid=(width // ADA_TN,),
        in_specs=[rows(c_sample), rows(c_prompt),
                  pl.BlockSpec((D_MODEL, ADA_TN), lambda j: (0, j)),
                  pl.BlockSpec((1, ADA_TN), lambda j: (0, j))],
        out_specs=[cols(c_sample), cols(c_prompt)],
        out_shape=[jax.ShapeDtypeStruct((c.shape[0], width), F32) for c in (c_sample, c_prompt)],
        compiler_params=_params(("arbitrary",)),
        name="ada",
    )(c_sample, c_prompt, w_ada, b_ada.reshape(1, width))


def _mod_specs(sub, ada):
    return [pl.BlockSpec((ada.shape[0], D_MODEL), lambda *_, col=sub * 3 + j: (0, col))
            for j in range(3)]


def _load_mod(ref, row):
    if row is None:
        return ref[...]
    return ref[pl.ds(row, 1), :]


BF16_SUBLANES = 16
WEIGHT_PITCH_TILES = 8
MIX_QKVR_CAST_ROWS = 2 * LANES
MIX_CONV_CAST_ROWS = 4 * LANES


class _CastJob(NamedTuple):
    src: jax.Array
    block_rows: int
    n_blocks: int
    transpose: bool = False
    row0: int = 0
    valid: int = 0
    pad: int = 0

    @property
    def n_steps(self):
        return self.n_blocks + (self.pad if self.transpose else 0)

    def specs(self, step_of):
        cols = self.src.shape[1]
        r, nb = self.block_rows, self.n_blocks
        block = lambda *g: jnp.minimum(step_of(*g), nb - 1)
        if not self.transpose:
            return (pl.BlockSpec((r, cols), lambda *g: (block(*g), 0)),
                    pl.BlockSpec((r, cols + self.pad), lambda *g: (block(*g), 0)),
                    jax.ShapeDtypeStruct((nb * r, cols + self.pad), BF16))
        assert self.row0 % SUBLANES == 0 and r % SUBLANES == 0
        out_block = lambda *g: jnp.minimum(step_of(*g), self.n_steps - 1)
        return (pl.BlockSpec((pl.Element(r), pl.Element(cols)),
                             lambda *g: (pl.multiple_of(self.row0 + r * block(*g), SUBLANES), 0)),
                pl.BlockSpec((cols, r), lambda *g: (0, out_block(*g))),
                jax.ShapeDtypeStruct((cols, self.n_steps * r), BF16))

    @property
    def static(self):
        return (self.transpose, self.valid or self.block_rows, self.n_blocks, self.pad)


def _whole_cast(w, n_steps, pad=0):
    rows = w.shape[0]
    r = next(r for r in range(BF16_SUBLANES, rows + 1, BF16_SUBLANES)
             if rows % r == 0 and rows // r <= n_steps)
    return _CastJob(w, r, rows // r, pad=pad)


def _cast_plumbing(jobs, step_of):
    specs = [job.specs(step_of) for job in jobs]
    return ([i for i, _, _ in specs], [o for _, o, _ in specs], [s for _, _, s in specs])


def _run_casts(statics, step, src_refs, dst_refs):
    for (transpose, valid, n_blocks, pad), src, dst in zip(statics, src_refs, dst_refs,
                                                           strict=True):
        blk = src[...]
        if not transpose:
            cols = blk.shape[1]
            dst[:, :cols] = blk.astype(BF16)
            if pad:
                dst[:, cols:] = jnp.zeros((blk.shape[0], pad), BF16)
            continue
        blk = blk.T
        if valid < blk.shape[1]:
            lane = lax.broadcasted_iota(jnp.int32, blk.shape, 1)
            blk = jnp.where(lane < valid, blk, 0.0)
        if pad:
            blk = jnp.where(step < n_blocks, blk, 0.0)
        dst[...] = blk.astype(BF16)


def _ffn_rows(x_ref, o_ref, mods, per_token, gpre_ref, gpost_ref, win_ref, wout_ref):
    shift, scale, gate = mods
    tm = x_ref.shape[1]
    parts = max(tm // FFN_PART, 1)
    rows = [slice(i * (tm // parts), (i + 1) * (tm // parts)) for i in range(parts)]
    mod = lambda m, rs: m[rs] if per_token else m

    def pre(rs):
        return (_rms(x_ref[0, rs, :], gpre_ref[...]) * (1.0 + mod(scale, rs))
                + mod(shift, rs)).astype(BF16)

    def post(rs, out):
        o_ref[0, rs, :] = (x_ref[0, rs, :]
                           + MACARON_WEIGHT * mod(gate, rs) * _rms(out, gpost_ref[...]))

    h = pre(rows[0])
    for i in range(parts):
        gu = _dot(h, win_ref[...])
        h_next = pre(rows[i + 1]) if i + 1 < parts else None
        a = (_silu(gu[:, :D_FF]) * gu[:, D_FF:]).astype(BF16)
        if i > 0:
            post(rows[i - 1], out)
        out = _dot(a, wout_ref[:, :D_MODEL])
        h = h_next
    post(rows[parts - 1], out)


def _ffn_kernel(x_ref, sh_ref, sc_ref, gt_ref, xs_ref, shs_ref, scs_ref, gts_ref, gpre_ref,
                gpost_ref, win_ref, wout_ref, *rest, sub, n_l, job_groups):
    n_jobs = len(job_groups)
    o_ref, os_ref = rest[n_jobs:n_jobs + 2]
    step = pl.program_id(0)
    _run_casts(job_groups, step, rest[:n_jobs], rest[n_jobs + 2:])
    last = pl.num_programs(0) - 1
    weights = (gpre_ref.at[sub:sub + 1], gpost_ref.at[sub:sub + 1], win_ref, wout_ref)

    @pl.when(step < last)
    def _():
        mods = [_load_mod(r, step // n_l) for r in (sh_ref, sc_ref, gt_ref)]
        _ffn_rows(x_ref, o_ref, mods, False, *weights)

    @pl.when(step == last)
    def _():
        mods = [_load_mod(r, None) for r in (shs_ref, scs_ref, gts_ref)]
        _ffn_rows(xs_ref, os_ref, mods, True, *weights)


def _ffn(x, xs, ada_p, ada_s, sub, g_pre, g_post, w_in, w_out, *, jobs=()):
    n, l, _ = x.shape
    n_l = l // FFN_TM
    n_tiles = n * n_l
    tile = lambda i: (jnp.minimum(i, n_tiles - 1) // n_l, jnp.minimum(i, n_tiles - 1) % n_l, 0)
    whole = lambda i: (0, 0, 0)
    kern = functools.partial(_ffn_kernel, sub=sub, n_l=n_l,
                             job_groups=tuple(j.static for j in jobs))
    cast_in, cast_out, cast_shapes = _cast_plumbing(jobs, lambda i: i)
    return pl.pallas_call(
        kern,
        grid=(n_tiles + 1,),
        in_specs=[pl.BlockSpec((1, FFN_TM, D_MODEL), tile)]
        + _mod_specs(sub, ada_p)
        + [pl.BlockSpec(xs.shape, whole)]
        + _mod_specs(sub, ada_s)
        + [_resident(g_pre.shape), _resident(g_post.shape),
           _resident(w_in.shape), _resident(w_out.shape)]
        + cast_in,
        out_specs=[pl.BlockSpec((1, FFN_TM, D_MODEL), tile), pl.BlockSpec(xs.shape, whole)]
        + cast_out,
        out_shape=[jax.ShapeDtypeStruct(x.shape, F32), jax.ShapeDtypeStruct(xs.shape, F32)]
        + cast_shapes,
        compiler_params=_params(("arbitrary",)),
        name="ffn",
    )(x, ada_p, ada_p, ada_p, xs, ada_s, ada_s, ada_s, g_pre, g_post, w_in, w_out,
      *[j.src for j in jobs])


def _log_decay(hb, wz_ref, walpha_ref, balpha_ref):
    z = _dot(hb, wz_ref[...]).astype(BF16)
    xg = _dot(z, walpha_ref[...]) + balpha_ref[...]
    return jax.nn.log_sigmoid(xg) * (1.0 / GATE_TAU)


def _merge_and_project(y_gla, y_conv, sig_a, sig_b, wbo_ref, wmo_ref):
    pg = _dot(y_gla, wbo_ref[0, :, :D_MODEL])
    pc = _dot(y_conv, wbo_ref[1, :, :D_MODEL])
    merged = (sig_a * pg + sig_b * pc).astype(BF16)
    return _dot(merged, wmo_ref[:, :D_MODEL])


_PAIR_LEVELS = tuple(GLA_CHUNK >> i for i in range(1, GLA_CHUNK.bit_length()))


def _pair_operands(qs, k, b, log_a, m, tm):
    if m >= SUBLANES:
        halves = lambda a: a.reshape(tm // (2 * m), 2, m, QK_WIDTH)
        b4, k4, q4 = halves(b), halves(k), halves(qs)
        mid = b4[:, 0:1, m - 1:m, :]
        w = jnp.exp(jnp.concatenate([mid - b4[:, 0:1], b4[:, 1:2] - mid], axis=1))
        src = jnp.concatenate([k4[:, 0:1], q4[:, 1:2]], axis=1)
        return (src * w).astype(BF16).reshape(tm, QK_WIDTH)
    tiles = lambda a: a.reshape(tm // SUBLANES, SUBLANES, QK_WIDTH)
    sub = lax.broadcasted_iota(jnp.int32, (1, SUBLANES, QK_WIDTH), 1)
    second = (sub & m) != 0
    if m == 1:
        w = jnp.exp(jnp.where(second, tiles(log_a), 0.0))
    else:
        b4 = tiles(b)
        mid = b4[:, m - 1:m, :]
        for start in range(2 * m, SUBLANES, 2 * m):
            mid = jnp.where(sub >= start, b4[:, start + m - 1:start + m, :], mid)
        w = jnp.exp((b4 - mid) * jnp.where(second, 1.0, -1.0))
    return (jnp.where(second, tiles(qs), tiles(k)) * w).astype(BF16).reshape(tm, QK_WIDTH)


def _pair_mask(m):
    t = lax.broadcasted_iota(jnp.int32, (GLA_CHUNK, GLA_CHUNK), 0)
    s = lax.broadcasted_iota(jnp.int32, (GLA_CHUNK, GLA_CHUNK), 1)
    return ((t & m) != 0) & ((s & m) == 0) & ((t // (2 * m)) == (s // (2 * m)))


def _mixer_kernel(x_ref, sh_ref, sc_ref, gt_ref, gpre_ref, gpost_ref, wa_ref, wz_ref, wb_ref,
                  walpha_ref, balpha_ref, ggla_ref, wconv_ref, wbo_ref, wmo_ref, *rest,
                  job_groups):
    n_jobs = len(job_groups)
    y_ref, sgla_ref, sconv_ref = rest[n_jobs:n_jobs + 3]
    st_scr, u_scr, ygla_scr = rest[-3:]
    l = pl.program_id(1)
    n_l = pl.num_programs(1)
    _run_casts(job_groups, pl.program_id(0) * n_l + l, rest[:n_jobs], rest[n_jobs + 3:-3])

    @pl.when(l == 0)
    def _():
        st_scr[...] = jnp.zeros_like(st_scr)
        u_scr[0:SUBLANES, :] = jnp.zeros((SUBLANES, D_CONV), F32)

    mods = [_load_mod(r, pl.program_id(0)) for r in (sh_ref, sc_ref, gt_ref)]
    weights = (gpre_ref.at[1:2], gpost_ref.at[1:2], wa_ref, wz_ref, wb_ref, walpha_ref,
               balpha_ref, ggla_ref, wconv_ref, wbo_ref, wmo_ref)
    n_parts = x_ref.shape[1] // MIX_PART
    parts = [_mixer_part(x_ref, y_ref, mods, weights, (st_scr, u_scr, ygla_scr),
                         slice(i * MIX_PART, (i + 1) * MIX_PART)) for i in range(n_parts)]
    next(parts[0])
    for i, p in enumerate(parts):
        next(p)
        if i + 1 < n_parts:
            next(parts[i + 1])
        if i > 0:
            next(parts[i - 1], None)
        next(p)
    next(parts[-1], None)

    @pl.when(l == n_l - 1)
    def _():
        sgla_ref[0, 0] = st_scr[...]
        sconv_ref[0, 0] = u_scr[SUBLANES - 2:SUBLANES, :]


def _mixer_part(x_ref, y_ref, mods, weights, scratch, prow):
    shift, scale, gate = mods
    (gpre_ref, gpost_ref, wa_ref, wz_ref, wb_ref, walpha_ref, balpha_ref, ggla_ref, wconv_ref,
     wbo_ref, wmo_ref) = weights
    st_scr, u_scr, ygla_scr = scratch
    ygla_scr = ygla_scr.at[prow]
    tm = MIX_PART
    x = x_ref[0, prow, :]
    hb = (_rms(x, gpre_ref[...]) * (1.0 + scale) + shift).astype(BF16)
    yield

    def proj_b(j):
        return _dot(hb, wb_ref[:, j * D_MODEL:(j + 1) * D_MODEL])

    nc = tm // GLA_CHUNK
    log_a = _log_decay(hb, wz_ref, walpha_ref, balpha_ref)
    pa = _dot(hb, wa_ref[:, :_A_WIDTH])
    qs = pa[:, 0:QK_WIDTH] * (DK ** -0.5)
    k = pa[:, QK_WIDTH:2 * QK_WIDTH]
    vb = pa[:, 2 * QK_WIDTH:2 * QK_WIDTH + V_WIDTH].astype(BF16)
    r = pa[:, 2 * QK_WIDTH + V_WIDTH:]

    row = lax.broadcasted_iota(jnp.int32, (tm, tm), 0)
    col = lax.broadcasted_iota(jnp.int32, (tm, tm), 1)
    tri = jnp.where((row >= col) & (row // GLA_CHUNK == col // GLA_CHUNK), 1.0, 0.0).astype(BF16)
    la_hi = log_a.astype(BF16)
    rem = log_a - la_hi.astype(F32)
    la_mid = rem.astype(BF16)
    la_lo = (rem - la_mid.astype(F32)).astype(BF16)
    b3 = (_dot(tri, la_hi) + _dot(tri, la_mid) + _dot(tri, la_lo)).reshape(
        nc, GLA_CHUNK, QK_WIDTH)

    u = proj_b(1) * proj_b(2)
    u_scr[SUBLANES:SUBLANES + tm, :] = u
    yield

    b_last = b3[:, GLA_CHUNK - 1:GLA_CHUNK, :]
    b = b3.reshape(tm, QK_WIDTH)
    qd = (qs * jnp.exp(b)).astype(BF16).reshape(nc, GLA_CHUNK, QK_WIDTH)
    kl = (k.reshape(nc, GLA_CHUNK, QK_WIDTH) * jnp.exp(b_last - b3)).astype(BF16)
    b_last_rows = jnp.concatenate(
        [b_last[c, :, hd * DK:(hd + 1) * DK] for c in range(nc) for hd in range(N_HEADS)]
        + [jnp.zeros((LANES - nc * N_HEADS, DK), F32)], axis=0)
    decay_cols = jnp.exp(b_last_rows.T)

    pair_ops = [_pair_operands(qs, k, b, log_a, m, tm) for m in _PAIR_LEVELS]
    pair_masks = [_pair_mask(m) for m in _PAIR_LEVELS]
    qs_b, k_b = qs.astype(BF16), k.astype(BF16)
    crow = lax.broadcasted_iota(jnp.int32, (GLA_CHUNK, GLA_CHUNK), 0)
    ccol = lax.broadcasted_iota(jnp.int32, (GLA_CHUNK, GLA_CHUNK), 1)
    scores, upd = {}, {}
    for c in range(nc):
        rows = slice(c * GLA_CHUNK, (c + 1) * GLA_CHUNK)
        for hd in range(N_HEADS):
            kc = slice(hd * DK, (hd + 1) * DK)
            s = jnp.where(crow == ccol, _dot_nt(qs_b[rows, kc], k_b[rows, kc]), 0.0)
            for p, mask in zip(pair_ops, pair_masks):
                s = jnp.where(mask, _dot_nt(p[rows, kc], p[rows, kc]), s)
            scores[c, hd] = s.astype(BF16)
            upd[c, hd] = _dot_tn(kl[c, :, kc], vb[rows, hd * DV:(hd + 1) * DV])

    conv = (wconv_ref[0:1, :] * u_scr[SUBLANES - 2:SUBLANES - 2 + tm, :]
            + wconv_ref[1:2, :] * u_scr[SUBLANES - 1:SUBLANES - 1 + tm, :]
            + wconv_ref[2:3, :] * u)
    u_scr[0:SUBLANES, :] = u_scr[tm:tm + SUBLANES, :]
    y_conv = (proj_b(0) * conv).astype(BF16)

    outs = {}
    for hd in range(N_HEADS):
        kc = slice(hd * DK, (hd + 1) * DK)
        vc = slice(hd * DV, (hd + 1) * DV)
        st = st_scr[hd]
        for c in range(nc):
            rows = slice(c * GLA_CHUNK, (c + 1) * GLA_CHUNK)
            lhs = jnp.concatenate([qd[c, :, kc], scores[c, hd]], axis=1)
            rhs = jnp.concatenate([st.astype(BF16), vb[rows, vc]], axis=0)
            outs[c, hd] = _dot(lhs, rhs)
            j = c * N_HEADS + hd
            st = st * decay_cols[:, j:j + 1] + upd[c, hd]
        st_scr[hd] = st

    sig_a = jax.nn.sigmoid(proj_b(3))
    sig_b = jax.nn.sigmoid(proj_b(4))

    for hd in range(N_HEADS):
        vc = slice(hd * DV, (hd + 1) * DV)
        for c in range(nc):
            rows = slice(c * GLA_CHUNK, (c + 1) * GLA_CHUNK)
            on = _rms(outs[c, hd], ggla_ref[:, vc])
            ygla_scr[rows, vc] = (on * _silu(r[rows, vc])).astype(BF16)

    yield
    mix = _merge_and_project(ygla_scr[...], y_conv, sig_a, sig_b, wbo_ref, wmo_ref)
    y_ref[0, prow, :] = x + gate * _rms(mix, gpost_ref[...])


def _mixer(x, ada, g_pre, g_post, wts, *, jobs=()):
    n, l, _ = x.shape
    tm = MIX_TM
    n_l = l // tm
    kern = functools.partial(_mixer_kernel, job_groups=tuple(j.static for j in jobs))
    cast_in, cast_out, cast_shapes = _cast_plumbing(jobs, lambda n, l: n * n_l + l)
    wa, wz, wb, walpha, balpha, ggla, wconv, wbo, wmo = wts
    return pl.pallas_call(
        kern,
        grid=(n, n_l),
        in_specs=[pl.BlockSpec((1, tm, D_MODEL), lambda n, l: (n, l, 0))]
        + _mod_specs(1, ada)
        + [_resident(a.shape) for a in (g_pre, g_post, wa, wz, wb, walpha, balpha, ggla, wconv,
                                        wbo, wmo)]
        + cast_in,
        out_specs=[
            pl.BlockSpec((1, tm, D_MODEL), lambda n, l: (n, l, 0)),
            pl.BlockSpec((1, 1, N_HEADS, DK, DV), lambda n, l: (0, n, 0, 0, 0)),
            pl.BlockSpec((1, 1, CONV_WIDTH - 1, D_CONV), lambda n, l: (0, n, 0, 0)),
        ] + cast_out,
        out_shape=[
            jax.ShapeDtypeStruct(x.shape, F32),
            jax.ShapeDtypeStruct((1, n, N_HEADS, DK, DV), F32),
            jax.ShapeDtypeStruct((1, n, CONV_WIDTH - 1, D_CONV), F32),
        ] + cast_shapes,
        scratch_shapes=[
            pltpu.VMEM((N_HEADS, DK, DV), F32),
            pltpu.VMEM((MIX_PART + SUBLANES, D_CONV), F32),
            pltpu.VMEM((tm, V_WIDTH), BF16),
        ],
        compiler_params=_params(("arbitrary", "arbitrary")),
        name="mixer",
    )(x, ada, ada, ada, g_pre, g_post, wa, wz, wb, walpha, balpha, ggla, wconv, wbo, wmo,
      *[j.src for j in jobs])


def _smix_pre_kernel(x_ref, sh_ref, sc_ref, gpre_ref, wa_ref, wz_ref, wb_ref, walpha_ref,
                     balpha_ref, wconv_ref, sconv_ref,
                     qt_ref, kt_ref, at_ref, v_ref, sr_ref, yconv_ref, sga_ref, sgb_ref,
                     sconv_new_ref):
    x = x_ref[...]
    hb = (_rms(x, gpre_ref[1:2, :]) * (1.0 + sc_ref[...]) + sh_ref[...]).astype(BF16)
    pa = _dot(hb, wa_ref[:, :_A_WIDTH])
    q = pa[:, 0:QK_WIDTH] * (DK ** -0.5)
    k = pa[:, QK_WIDTH:2 * QK_WIDTH]
    v_ref[...] = pa[:, 2 * QK_WIDTH:2 * QK_WIDTH + V_WIDTH]
    sr_ref[...] = _silu(pa[:, 2 * QK_WIDTH + V_WIDTH:])
    a = jnp.exp(_log_decay(hb, wz_ref, walpha_ref, balpha_ref))
    for hd in range(N_HEADS):
        kc = slice(hd * DK, (hd + 1) * DK)
        qt_ref[hd] = q[:, kc].T
        kt_ref[hd] = k[:, kc].T
        at_ref[hd] = a[:, kc].T

    pb = _dot(hb, wb_ref[:, :_B_WIDTH])
    u = pb[:, D_CONV:2 * D_CONV] * pb[:, 2 * D_CONV:3 * D_CONV]
    conv = (wconv_ref[0:1, :] * sconv_ref[:, 0, :] + wconv_ref[1:2, :] * sconv_ref[:, 1, :]
            + wconv_ref[2:3, :] * u)
    yconv_ref[...] = pb[:, 0:D_CONV] * conv
    sga_ref[...] = jax.nn.sigmoid(pb[:, 3 * D_CONV:3 * D_CONV + D_MODEL])
    sgb_ref[...] = jax.nn.sigmoid(pb[:, 3 * D_CONV + D_MODEL:])
    sconv_new_ref[:, 0, :] = sconv_ref[:, 1, :]
    sconv_new_ref[:, 1, :] = u


def _smix_state_kernel(s_ref, qt_ref, kt_ref, at_ref, v_ref, snew_ref, o_ref):
    base = pl.program_id(0) * SAMPLE_BLOCK
    lane = lax.broadcasted_iota(jnp.int32, (1, LANES), 1)
    for i in range(SAMPLE_BLOCK):
        pick = jnp.where(lane == base + i, 1.0, 0.0)
        for hd in range(N_HEADS):
            vc = slice(hd * DV, (hd + 1) * DV)
            a_col = jnp.sum(at_ref[hd] * pick, axis=1, keepdims=True)
            k_col = jnp.sum(kt_ref[hd] * pick, axis=1, keepdims=True)
            q_col = jnp.sum(qt_ref[hd] * pick, axis=1, keepdims=True)
            s_new = a_col * s_ref[0, i, hd] + k_col * v_ref[i:i + 1, vc]
            snew_ref[0, i, hd] = s_new
            o_ref[i:i + 1, vc] = jnp.sum(q_col * s_new, axis=0, keepdims=True)


def _smix_post_kernel(x_ref, gt_ref, gpost_ref, o_ref, sr_ref, yconv_ref, sga_ref, sgb_ref,
                      ggla_ref, wbo_ref, wmo_ref, y_ref, ygla_scr):
    for hd in range(N_HEADS):
        vc = slice(hd * DV, (hd + 1) * DV)
        on = _rms(o_ref[:, vc], ggla_ref[:, vc])
        ygla_scr[:, vc] = (on * sr_ref[:, vc]).astype(BF16)
    mix = _merge_and_project(ygla_scr[...], yconv_ref[...].astype(BF16), sga_ref[...],
                             sgb_ref[...], wbo_ref, wmo_ref)
    y_ref[...] = x_ref[...] + gt_ref[...] * _rms(mix, gpost_ref[1:2, :])


def _sample_mixer(x, ada, state_gla, state_conv, g_pre, g_post, wts):
    s = x.shape[0]
    wa, wz, wb, walpha, balpha, ggla, wconv, wbo, wmo = wts
    whole = lambda a: pl.BlockSpec(a.shape, lambda *_: (0,) * a.ndim)
    mod = lambda j: pl.BlockSpec((s, D_MODEL), lambda *_: (0, 3 + j))
    tok = jax.ShapeDtypeStruct((s, D_MODEL), F32)
    tr = jax.ShapeDtypeStruct((N_HEADS, DK, s), F32)

    pre_in = (x, ada, ada, g_pre, wa, wz, wb, walpha, balpha, wconv, state_conv)
    pre_specs = [whole(x), mod(0), mod(1)] + [whole(a) for a in pre_in[3:]]
    pre_out = [tr, tr, tr, tok, tok, tok, tok, tok, jax.ShapeDtypeStruct(state_conv.shape, F32)]
    qt, kt, at, v, sr, yconv, sga, sgb, sconv_new = pl.pallas_call(
        _smix_pre_kernel,
        grid=(1,),
        in_specs=pre_specs,
        out_specs=[whole(o) for o in pre_out],
        out_shape=pre_out,
        compiler_params=_params(("arbitrary",)),
        name="smix_pre",
    )(*pre_in)

    state_block = pl.BlockSpec((1, SAMPLE_BLOCK, N_HEADS, DK, DV), lambda j: (0, j, 0, 0, 0))
    snew, o = pl.pallas_call(
        _smix_state_kernel,
        grid=(s // SAMPLE_BLOCK,),
        in_specs=[state_block, whole(qt), whole(kt), whole(at),
                  pl.BlockSpec((SAMPLE_BLOCK, V_WIDTH), lambda j: (j, 0))],
        out_specs=[state_block, pl.BlockSpec((SAMPLE_BLOCK, V_WIDTH), lambda j: (j, 0))],
        out_shape=[jax.ShapeDtypeStruct(state_gla.shape, F32), tok],
        compiler_params=_params(("arbitrary",)),
        name="smix_state",
    )(state_gla, qt, kt, at, v)

    post_in = (x, ada, g_post, o, sr, yconv, sga, sgb, ggla, wbo, wmo)
    post_specs = [whole(x), mod(2)] + [whole(a) for a in post_in[2:]]
    y = pl.pallas_call(
        _smix_post_kernel,
        grid=(1,),
        in_specs=post_specs,
        out_specs=whole(tok),
        out_shape=tok,
        scratch_shapes=[pltpu.VMEM((s, V_WIDTH), BF16)],
        compiler_params=_params(("arbitrary",)),
        name="smix_post",
    )(*post_in)
    return y, snew, sconv_new


def kernel(x_prompt, x_sample, state_gla, state_conv, c_prompt, c_sample, w_ada, b_ada, g_pre,
           g_post, w_ffn1_in, w_ffn1_out, w_ffn2_in, w_ffn2_out, w_mix_in, w_alpha, b_alpha,
           g_gla_norm, w_conv, w_branch_out, w_mix_out):
    depth = w_ada.shape[0]
    n_s = x_sample.shape[0]
    assert depth == 1 and x_sample.shape[1] == 1 and n_s == LANES

    yp = x_prompt
    ys = x_sample.reshape(1, n_s, D_MODEL)
    outs = []
    for i in range(depth):
        ada_s, ada_p = _ada(c_sample, c_prompt, w_ada[i], b_ada[i])

        ffn1_w = (w_ffn1_in[i].astype(BF16),
                  jnp.pad(w_ffn1_out[i], ((0, 0), (0, LANES))).astype(BF16))
        n_ffn_steps = yp.shape[0] * (yp.shape[1] // FFN_TM)
        n_mix_steps = yp.shape[0] * (yp.shape[1] // MIX_TM)
        w_mix_t = jnp.swapaxes(w_mix_in[i], 0, 1)
        mixer_jobs = (
            _CastJob(w_mix_t, MIX_QKVR_CAST_ROWS, _A_WIDTH // MIX_QKVR_CAST_ROWS, transpose=True,
                     pad=1),
            _CastJob(w_mix_t, LANES, 1, transpose=True, row0=_A_WIDTH, valid=GATE_RANK),
            _CastJob(w_mix_t, MIX_CONV_CAST_ROWS, _B_WIDTH // MIX_CONV_CAST_ROWS, transpose=True,
                     row0=_A_WIDTH + GATE_RANK, pad=1),
            _whole_cast(w_branch_out[i].reshape(2 * D_MODEL, D_MODEL), n_ffn_steps, pad=LANES),
            _whole_cast(w_mix_out[i], n_ffn_steps, pad=LANES))
        assert all(j.n_steps <= n_ffn_steps for j in mixer_jobs)
        assert all((j.n_steps * j.block_rows if j.transpose else j.src.shape[1] + j.pad)
                   % (WEIGHT_PITCH_TILES * LANES) for j in mixer_jobs)
        yp, ys, wa, wz, wb, wbo, wmo = _ffn(yp, ys, ada_p, ada_s, 0, g_pre[i], g_post[i], *ffn1_w,
                                            jobs=mixer_jobs)
        walpha = jnp.pad(w_alpha[i], ((0, LANES - GATE_RANK), (0, 0))).astype(BF16)
        mix_w = (wa, wz, wb, walpha, b_alpha[i].reshape(1, QK_WIDTH),
                 g_gla_norm[i].reshape(1, V_WIDTH),
                 jnp.pad(w_conv[i], ((0, SUBLANES - CONV_WIDTH), (0, 0))),
                 wbo.reshape(2, D_MODEL, D_MODEL + LANES), wmo)
        ys, gla_s, sconv_new = _sample_mixer(ys[0], ada_s, state_gla[i:i + 1], state_conv[i],
                                             g_pre[i], g_post[i], mix_w)
        yp, gla_p, conv_p, *ffn2_w = _mixer(
            yp, ada_p, g_pre[i], g_post[i], mix_w,
            jobs=(_whole_cast(w_ffn2_in[i], n_mix_steps),
                  _whole_cast(w_ffn2_out[i], n_mix_steps, pad=LANES)))
        yp, ys = _ffn(yp, ys.reshape(1, n_s, D_MODEL), ada_p, ada_s, 2, g_pre[i], g_post[i],
                      *ffn2_w)
        outs.append((gla_p, conv_p, gla_s, sconv_new[None]))

    gla_p, conv_p, gla_s, conv_s = outs[0]
    return (yp, ys.reshape(n_s, 1, D_MODEL), gla_p, conv_p, gla_s, conv_s)
```
